```python
import jax
import jax.numpy as jnp
from jax import lax
import numpy as np

D_MODEL = 2048
BATCH = 4
SEQ = 2048
DEPTH = 1
DEC_BATCH = 128
DEC_SEQ = 8
PAST_LEN = 16384
PAGE_SIZE = 128

A_HEAD = 64
A_WIDTH = D_MODEL // 2
A_HEADS = A_WIDTH // A_HEAD
A_DECAY_LORA = 64
A_ICL_LORA = 64
A_GATE_LORA = 160
A_GN_EPS = 64e-5
B_HEADS = 8
B_QK_HEAD = (D_MODEL // 2) // B_HEADS
B_V_HEAD = 2 * B_QK_HEAD
B_QK_WIDTH = B_HEADS * B_QK_HEAD
B_V_WIDTH = B_HEADS * B_V_HEAD
RET_CHUNK = 128
ROT_BASE = 10000.0
N_KEYS = 128
N_EXPERTS = N_KEYS * N_KEYS
PEER_HEADS = 8
PEER_QDIM = 256
PEER_TOPK = 16
PEER_BLOCK = 128
NORM_EPS = 1e-6
SHIFT_W = 3 * A_WIDTH + A_DECAY_LORA + A_ICL_LORA + A_GATE_LORA
A_SPLITS = [A_WIDTH, 2 * A_WIDTH, 3 * A_WIDTH, 3 * A_WIDTH + A_DECAY_LORA,
            3 * A_WIDTH + A_DECAY_LORA + A_ICL_LORA]
IN_SPLITS = [SHIFT_W, SHIFT_W + B_QK_WIDTH, SHIFT_W + 2 * B_QK_WIDTH,
             SHIFT_W + 2 * B_QK_WIDTH + B_V_WIDTH, SHIFT_W + 2 * B_QK_WIDTH + 2 * B_V_WIDTH]
IN_W = IN_SPLITS[-1] + 2 * D_MODEL
F32 = jnp.float32

kernel_name = 'rwkv7_retention_peer_hybrid_step'


def rmsnorm(x, g):
    xf = x.astype(F32)
    y = xf * lax.rsqrt(jnp.mean(xf * xf, axis=-1, keepdims=True) + NORM_EPS)
    return y.astype(x.dtype) * g


def token_shift(z, prev, mu):
    z_prev = jnp.concatenate([prev[:, None, :].astype(z.dtype), z[:, :-1]], axis=1)
    return z + (z_prev - z) * mu, z[:, -1]


def rwkv7_scan(s0, r, w, k, v, kk, a):
    def step(s, inp):
        r_t, w_t, k_t, v_t, kk_t, a_t = inp
        sa = jnp.einsum('bhvk,bhk->bhv', s, -kk_t)
        s = (s * w_t[:, :, None, :] + sa[..., None] * (kk_t * a_t)[:, :, None, :]
             + v_t[..., None] * k_t[:, :, None, :])
        return s, jnp.einsum('bhvk,bhk->bhv', s, r_t)
    xs = [jnp.swapaxes(t, 0, 1) for t in (r, w, k, v, kk, a)]
    s_final, y = lax.scan(step, s0, xs)
    return jnp.swapaxes(y, 0, 1), s_final


def rwkv7_branch(z, shift_prev, s0, mu_shift, w0, w_w2, a0, w_a2, w_g2, k_k, k_a, r_k, lnx_w, lnx_b):
    bsz, t, _ = z.shape
    z, shift_new = token_shift(z, shift_prev, mu_shift)
    r, k, v, wl, al, gl = jnp.split(z, A_SPLITS, axis=-1)
    w = -jax.nn.softplus(-(w0 + jnp.tanh(wl) @ w_w2).astype(F32)) - 0.5
    decay = jnp.exp(-jnp.exp(w))
    a = jax.nn.sigmoid(a0 + al @ w_a2)
    g = jax.nn.sigmoid(gl) @ w_g2

    def heads(u):
        return u.reshape(bsz, t, A_HEADS, A_HEAD).astype(F32)

    kk = heads(k * k_k)
    kk = kk / jnp.maximum(jnp.sqrt(jnp.sum(kk * kk, axis=-1, keepdims=True)), 1e-12)
    k = k * (1.0 + (a - 1.0) * k_a)
    rh, kh, vh = heads(r), heads(k), heads(v)
    y, s_new = rwkv7_scan(s0.astype(F32), rh, heads(decay), kh, vh, kk, heads(a))
    mean = jnp.mean(y, axis=-1, keepdims=True)
    var = jnp.mean(jnp.square(y - mean), axis=-1, keepdims=True)
    y = ((y - mean) * lax.rsqrt(var + A_GN_EPS)).reshape(bsz, t, A_WIDTH) * lnx_w + lnx_b
    bonus = jnp.sum(rh * kh * r_k.reshape(A_HEADS, A_HEAD).astype(F32), axis=-1, keepdims=True) * vh
    o = (y + bonus.reshape(bsz, t, A_WIDTH)) * g
    return o.astype(z.dtype), shift_new, s_new


def rotate(x, pos):
    half = x.shape[-1] // 2
    inv = jnp.power(ROT_BASE, -jnp.linspace(0.0, 1.0, half, dtype=F32))
    ang = pos[:, None] * inv[None, :]
    cos = jnp.cos(ang)[None, :, None, :]
    sin = jnp.sin(ang)[None, :, None, :]
    x1, x2 = x[..., :half], x[..., half:]
    return jnp.concatenate([x1 * cos - x2 * sin, x1 * sin + x2 * cos], axis=-1)


def ret_chunk(r_state, q, k, v, log_g):
    length = q.shape[1]
    idx = jnp.arange(length, dtype=F32)
    diff = idx[:, None] - idx[None, :]
    dmask = jnp.where(diff >= 0, jnp.exp(log_g[:, None, None] * jnp.maximum(diff, 0.0)), 0.0)
    s = jnp.einsum('bqhd,bkhd->bhqk', q, k) * dmask
    inner = jnp.einsum('bhqk,bkhv->bqhv', s, v)
    q_dec = jnp.exp(log_g[None, :] * (idx[:, None] + 1.0))[None, :, :, None]
    cross = jnp.einsum('bqhd,bhdv->bqhv', q, r_state) * q_dec
    k_dec = jnp.exp(log_g[None, :] * (length - 1.0 - idx[:, None]))[None, :, :, None]
    r_new = (jnp.exp(log_g * length)[None, :, None, None] * r_state
             + jnp.einsum('bkhd,bkhv->bhdv', k * k_dec, v))
    return r_new, inner + cross


def retention(r0, q, k, v, log_g):
    bsz, t = q.shape[0], q.shape[1]
    c = RET_CHUNK if t % RET_CHUNK == 0 else t
    n = t // c

    def to_chunks(u):
        return jnp.moveaxis(u.reshape(bsz, n, c, u.shape[2], u.shape[3]), 1, 0)

    r_final, o = lax.scan(lambda rs, xs: ret_chunk(rs, xs[0], xs[1], xs[2], log_g), r0,
                          (to_chunks(q), to_chunks(k), to_chunks(v)))
    o = jnp.moveaxis(o, 0, 1).reshape(bsz, t, B_HEADS, B_V_HEAD)
    return o, r_final


def retention_branch(q, k, v, gate, r0, pos):
    bsz, t, _ = q.shape
    log_g = jnp.log(1.0 - jnp.power(2.0, -5.0 - jnp.arange(B_HEADS, dtype=F32)))
    qh = rotate(q.reshape(bsz, t, B_HEADS, B_QK_HEAD).astype(F32), pos)
    kh = rotate(k.reshape(bsz, t, B_HEADS, B_QK_HEAD).astype(F32), pos) * (B_QK_HEAD ** -0.5)
    vh = v.reshape(bsz, t, B_HEADS, B_V_HEAD).astype(F32)
    o, r_new = retention(r0.astype(F32), qh, kh, vh, log_g)
    o = o * lax.rsqrt(jnp.mean(o * o, axis=-1, keepdims=True) + NORM_EPS)
    o = jax.nn.silu(gate.astype(F32)) * o.reshape(bsz, t, B_V_WIDTH)
    return o.astype(q.dtype), r_new


def peer(x, peer_wq, peer_keys, peer_u, peer_v):
    t, d = x.shape
    nb = -(-t // PEER_BLOCK)
    xp = jnp.pad(x, ((0, nb * PEER_BLOCK - t), (0, 0))).reshape(nb, PEER_BLOCK, d)

    def block(xb):
        q = (xb @ peer_wq).reshape(PEER_BLOCK, PEER_HEADS, 2, PEER_QDIM // 2)
        s = jnp.einsum('bhpd,hpnd->bhpn', q, peer_keys).astype(F32)
        sv, si = lax.top_k(s, PEER_TOPK)
        cand = sv[:, :, 0, :, None] + sv[:, :, 1, None, :]
        sc, flat = lax.top_k(cand.reshape(PEER_BLOCK, PEER_HEADS, PEER_TOPK * PEER_TOPK), PEER_TOPK)
        i1 = jnp.take_along_axis(si[:, :, 0], flat // PEER_TOPK, axis=-1)
        i2 = jnp.take_along_axis(si[:, :, 1], flat % PEER_TOPK, axis=-1)
        e = i1 * N_KEYS + i2
        gate = jax.nn.softmax(sc, axis=-1)
        h = jax.nn.gelu(jnp.einsum('bhkd,bd->bhk', peer_u[e], xb).astype(F32), approximate=False)
        return jnp.einsum('bhk,bhkd->bd', (gate * h).astype(xb.dtype), peer_v[e])

    return lax.map(block, xp).reshape(nb * PEER_BLOCK, d)[:t]


def decoder_layer(x, shift0, wkv0, ret0, pos, norm1, w_in, mu_shift, w0, w_w2, a0, w_a2, w_g2,
                  k_k, k_a, r_k, lnx_w, lnx_b, w_pa, w_pb, w_out, norm2, peer_wq, peer_keys,
                  peer_u, peer_v):
    bsz, t, d = x.shape
    hn = rmsnorm(x, norm1)
    z_a, q_b, k_b, v_b, g_b, g_merge = jnp.split(hn @ w_in, IN_SPLITS, axis=-1)
    o_a, shift_new, wkv_new = rwkv7_branch(z_a, shift0, wkv0, mu_shift, w0, w_w2, a0, w_a2, w_g2,
                                           k_k, k_a, r_k, lnx_w, lnx_b)
    o_b, ret_new = retention_branch(q_b, k_b, v_b, g_b, ret0, pos)
    gate_a, gate_b = jnp.split(jax.nn.sigmoid(g_merge), 2, axis=-1)
    x = x + (gate_a * (o_a @ w_pa) + gate_b * (o_b @ w_pb)) @ w_out
    x = x + peer(rmsnorm(x, norm2).reshape(bsz * t, d), peer_wq, peer_keys, peer_u, peer_v).reshape(bsz, t, d)
    return x, shift_new, wkv_new, ret_new


def trunk(x, shift0, wkv0, ret0, pos, weights, norm_f):
    shifts, wkvs, rets = [], [], []
    for layer in range(DEPTH):
        lw = [w[layer] for w in weights]
        x, s_new, wkv_new, ret_new = decoder_layer(x, shift0[layer], wkv0[layer], ret0[layer], pos, *lw)
        shifts.append(s_new.astype(shift0.dtype))
        wkvs.append(wkv_new.astype(wkv0.dtype))
        rets.append(ret_new.astype(ret0.dtype))
    return rmsnorm(x, norm_f), jnp.stack(shifts), jnp.stack(wkvs), jnp.stack(rets)


def setup_inputs(seed: int = 0) -> dict:
    key = jax.random.key(seed)
    ks = jax.random.split(key, 32)

    def nrm(k, shape, scale):
        return jax.random.normal(k, shape, F32) * scale

    L = DEPTH
    return {
        'x_prompt': nrm(ks[0], (BATCH, SEQ, D_MODEL), 1.0),
        'x_sample': nrm(ks[1], (DEC_BATCH, DEC_SEQ, D_MODEL), 1.0),
        'state_shift': nrm(ks[2], (L, DEC_BATCH, SHIFT_W), 1.0),
        'state_wkv': nrm(ks[3], (L, DEC_BATCH, A_HEADS, A_HEAD, A_HEAD), 0.3),
        'state_ret': nrm(ks[4], (L, DEC_BATCH, B_HEADS, B_QK_HEAD, B_V_HEAD), 0.3),
        'norm1': 1.0 + nrm(ks[5], (L, D_MODEL), 0.02),
        'w_in': nrm(ks[6], (L, D_MODEL, IN_W), D_MODEL ** -0.5),
        'mu_shift': jax.random.uniform(ks[7], (L, SHIFT_W), F32),
        'w0': jax.random.uniform(ks[8], (L, A_WIDTH), F32, -4.0, 1.0),
        'w_w2': nrm(ks[9], (L, A_DECAY_LORA, A_WIDTH), 0.1 * A_DECAY_LORA ** -0.5),
        'a0': nrm(ks[10], (L, A_WIDTH), 0.1),
        'w_a2': nrm(ks[11], (L, A_ICL_LORA, A_WIDTH), 0.1 * A_ICL_LORA ** -0.5),
        'w_g2': nrm(ks[12], (L, A_GATE_LORA, A_WIDTH), A_GATE_LORA ** -0.5),
        'k_k': 0.85 + nrm(ks[13], (L, A_WIDTH), 0.05),
        'k_a': 1.0 + nrm(ks[14], (L, A_WIDTH), 0.05),
        'r_k': nrm(ks[15], (L, A_WIDTH), 0.1),
        'lnx_w': 1.0 + nrm(ks[16], (L, A_WIDTH), 0.02),
        'lnx_b': nrm(ks[17], (L, A_WIDTH), 0.02),
        'w_pa': nrm(ks[18], (L, A_WIDTH, D_MODEL), A_WIDTH ** -0.5),
        'w_pb': nrm(ks[19], (L, B_V_WIDTH, D_MODEL), B_V_WIDTH ** -0.5),
        'w_out': nrm(ks[20], (L, D_MODEL, D_MODEL), D_MODEL ** -0.5),
        'norm2': 1.0 + nrm(ks[21], (L, D_MODEL), 0.02),
        'peer_wq': nrm(ks[22], (L, D_MODEL, PEER_HEADS * PEER_QDIM), D_MODEL ** -0.5),
        'peer_keys': nrm(ks[23], (L, PEER_HEADS, 2, N_KEYS, PEER_QDIM // 2), (PEER_QDIM // 2) ** -0.5),
        'peer_u': nrm(ks[24], (L, N_EXPERTS, D_MODEL), D_MODEL ** -0.5),
        'peer_v': nrm(ks[25], (L, N_EXPERTS, D_MODEL), PEER_HEADS ** -0.5),
        'norm_f': 1.0 + nrm(ks[26], (D_MODEL,), 0.02),
    }


def reference(x_prompt, x_sample, state_shift, state_wkv, state_ret, norm1, w_in, mu_shift, w0,
              w_w2, a0, w_a2, w_g2, k_k, k_a, r_k, lnx_w, lnx_b, w_pa, w_pb, w_out, norm2,
              peer_wq, peer_keys, peer_u, peer_v, norm_f):
    weights = (norm1, w_in, mu_shift, w0, w_w2, a0, w_a2, w_g2, k_k, k_a, r_k, lnx_w, lnx_b,
               w_pa, w_pb, w_out, norm2, peer_wq, peer_keys, peer_u, peer_v)
    bp, tp = x_prompt.shape[0], x_prompt.shape[1]
    ts = x_sample.shape[1]
    shift0 = jnp.zeros((DEPTH, bp, SHIFT_W), x_prompt.dtype)
    wkv0 = jnp.zeros((DEPTH, bp, A_HEADS, A_HEAD, A_HEAD), x_prompt.dtype)
    ret0 = jnp.zeros((DEPTH, bp, B_HEADS, B_QK_HEAD, B_V_HEAD), x_prompt.dtype)
    pos_p = jnp.arange(tp, dtype=F32)
    y_prompt, shift_p, wkv_p, ret_p = trunk(x_prompt, shift0, wkv0, ret0, pos_p, weights, norm_f)
    pos_s = PAST_LEN + jnp.arange(ts, dtype=F32)
    y_sample, shift_s, wkv_s, ret_s = trunk(x_sample, state_shift, state_wkv, state_ret, pos_s,
                                            weights, norm_f)
    return (y_prompt, y_sample, shift_p, wkv_p, ret_p, shift_s, wkv_s, ret_s)
```

```python
import functools

import jax
import jax.numpy as jnp
import numpy as np
from jax import lax
from jax.experimental import pallas as pl
from jax.experimental.pallas import tpu as pltpu

F32 = jnp.float32
BF16 = jnp.bfloat16

D_MODEL = 2048
A_HEAD = 64
A_WIDTH = 1024
A_HEADS = 16
A_DECAY_LORA = 64
A_ICL_LORA = 64
A_GATE_LORA = 160
A_GN_EPS = 64e-5
B_HEADS = 8
B_QK_HEAD = 128
B_V_HEAD = 256
B_QK_WIDTH = 1024
B_V_WIDTH = 2048
RET_CHUNK = 128
ROT_BASE = 10000.0
N_KEYS = 128
N_EXPERTS = N_KEYS * N_KEYS
PEER_HEADS = 8
PEER_TOPK = 16
NORM_EPS = 1e-6
PAST_LEN = 16384
SHIFT_W = 3 * A_WIDTH + A_DECAY_LORA + A_ICL_LORA + A_GATE_LORA

LANE = 128
ZA_W = 3584
COL_WL = 3072
COL_AL = 3200
COL_GL = 3328
COL_QB = 4096
COL_KB = 5120
COL_VB = 6144
COL_GB = 8192
COL_GMA = 10240
COL_GMB = 12288
W_TOT = 14336
VMEM_LIMIT = 56 * 1024 * 1024


def _cp(sem):
    return pltpu.CompilerParams(dimension_semantics=sem, vmem_limit_bytes=VMEM_LIMIT)


def _sigmoid(x):
    return 1.0 / (1.0 + jnp.exp(-x))


def _inproj_kernel(x_ref, g_ref, w_ref, o_ref, hn_ref):
    @pl.when(pl.program_id(1) == 0)
    def _():
        x = x_ref[...]
        ms = jnp.mean(x * x, axis=-1, keepdims=True)
        hn_ref[...] = (x * lax.rsqrt(ms + NORM_EPS) * g_ref[...]).astype(BF16)

    o_ref[...] = jnp.dot(hn_ref[...], w_ref[...], preferred_element_type=F32)


def _inproj(x, g, w, tm=1024, tn=512):
    n, d = x.shape
    wt = w.shape[1]
    return pl.pallas_call(
        _inproj_kernel,
        out_shape=jax.ShapeDtypeStruct((n, wt), F32),
        grid=(n // tm, wt // tn),
        in_specs=[
            pl.BlockSpec((tm, d), lambda i, j: (i, 0)),
            pl.BlockSpec((1, d), lambda i, j: (0, 0)),
            pl.BlockSpec((d, tn), lambda i, j: (0, j)),
        ],
        out_specs=pl.BlockSpec((tm, tn), lambda i, j: (i, j)),
        scratch_shapes=[pltpu.VMEM((tm, d), BF16)],
        compiler_params=_cp(("parallel", "arbitrary")),
        name="inproj",
    )(x, g, w)


def _seg_ones():
    r = lax.broadcasted_iota(jnp.int32, (LANE, LANE), 0) // A_HEAD
    c = lax.broadcasted_iota(jnp.int32, (LANE, LANE), 1) // A_HEAD
    return (r == c).astype(F32)


def _segsum64(x, ones):
    parts = []
    for j in range(x.shape[1] // LANE):
        parts.append(jnp.dot(x[:, j * LANE:(j + 1) * LANE], ones, preferred_element_type=F32,
                             precision=lax.Precision.HIGHEST))
    return jnp.concatenate(parts, axis=1)


def _rwkv_prep_kernel(z_ref, zp_ref, mu_ref, w0_ref, ww2_ref, a0_ref, wa2_ref, wg2_ref, kk_ref_, ka_ref,
                      r_o, w_o, k_o, v_o, kk_o, kka_o, g_o):
    z = z_ref[...]
    zs = z + (zp_ref[...] - z) * mu_ref[...]
    r = zs[:, 0:A_WIDTH]
    k = zs[:, A_WIDTH:2 * A_WIDTH]
    v = zs[:, 2 * A_WIDTH:3 * A_WIDTH]
    wl = zs[:, COL_WL:COL_AL]
    al = zs[:, COL_AL:COL_GL]
    gl = zs[:, COL_GL:ZA_W]
    wpre = w0_ref[...] + jnp.dot(jnp.tanh(wl).astype(BF16), ww2_ref[...], preferred_element_type=F32)
    nw = -wpre
    softplus = jnp.maximum(nw, 0.0) + jnp.log1p(jnp.exp(-jnp.abs(nw)))
    wlog = -softplus - 0.5
    decay = jnp.exp(-jnp.exp(wlog))
    a = _sigmoid(a0_ref[...] + jnp.dot(al.astype(BF16), wa2_ref[...], preferred_element_type=F32))
    g = jnp.dot(_sigmoid(gl).astype(BF16), wg2_ref[...], preferred_element_type=F32)
    kk = k * kk_ref_[...]
    ss = _segsum64(kk * kk, _seg_ones())
    kk = kk / jnp.maximum(jnp.sqrt(ss), 1e-12)
    k2 = k * (1.0 + (a - 1.0) * ka_ref[...])
    r_o[...] = r
    w_o[...] = decay
    k_o[...] = k2
    v_o[...] = v
    kk_o[...] = kk
    kka_o[...] = kk * a
    g_o[...] = g


def _rwkv_prep(z, zprev, mu, w0, ww2, a0, wa2, wg2, k_k, k_a, tm=256):
    n = zprev.shape[0]
    row = lambda w: pl.BlockSpec((1, w), lambda i: (0, 0))
    full = lambda a: pl.BlockSpec(a.shape, lambda i: (0, 0))
    out = jax.ShapeDtypeStruct((n, A_WIDTH), F32)
    return pl.pallas_call(
        _rwkv_prep_kernel,
        out_shape=[out] * 7,
        grid=(n // tm,),
        in_specs=[
            pl.BlockSpec((tm, ZA_W), lambda i: (i, 0)),
            pl.BlockSpec((tm, ZA_W), lambda i: (i, 0)),
            row(ZA_W), row(A_WIDTH), full(ww2), row(A_WIDTH), full(wa2), full(wg2), row(A_WIDTH), row(A_WIDTH),
        ],
        out_specs=[pl.BlockSpec((tm, A_WIDTH), lambda i: (i, 0))] * 7,
        compiler_params=_cp(("parallel",)),
        name="rwkv_prep",
    )(z, zprev, mu, w0, ww2, a0, wa2, wg2, k_k, k_a)


def _rwkv_scan_kernel(r_ref, w_ref, k_ref, v_ref, kk_ref, kka_ref, s0_ref, y_ref, sf_ref, s_scr, *, nb, unroll, tsteps):
    c = pl.program_id(1)

    @pl.when(c == 0)
    def _():
        s_scr[...] = s0_ref[...]

    lane = lax.broadcasted_iota(jnp.int32, (A_HEAD, LANE), 1)
    sub = lax.broadcasted_iota(jnp.int32, (A_HEAD, LANE), 0)
    lo = lane < A_HEAD
    diag = (lane & (A_HEAD - 1)) == sub
    diag_lo = jnp.logical_and(diag, lo)
    diag_hi = jnp.logical_and(diag, jnp.logical_not(lo))

    def seg(x):
        s_lo = jnp.sum(jnp.where(lo, x, 0.0), axis=1, keepdims=True)
        s_hi = jnp.sum(jnp.where(lo, 0.0, x), axis=1, keepdims=True)
        return jnp.where(lo, s_lo, s_hi)

    def step(t2, carry):
        for u in range(unroll):
            for b in range(nb):
                si = u * nb + b
                for j in range(A_WIDTH // LANE):
                    sl = slice(j * LANE, (j + 1) * LANE)
                    row = lambda ref: ref[t2, si:si + 1, sl]
                    s = s_scr[b, :, sl]
                    sa = -seg(s * row(kk_ref))
                    vrow = row(v_ref)
                    v_lo = jnp.sum(jnp.where(diag_lo, vrow, 0.0), axis=1, keepdims=True)
                    v_hi = jnp.sum(jnp.where(diag_hi, vrow, 0.0), axis=1, keepdims=True)
                    vb = jnp.where(lo, v_lo, v_hi)
                    s = s * row(w_ref) + sa * row(kka_ref) + vb * row(k_ref)
                    s_scr[b, :, sl] = s
                    yb = seg(s * row(r_ref))
                    y_ref[t2, si:si + 1, sl] = jnp.sum(jnp.where(diag, yb, 0.0), axis=0, keepdims=True)
        return carry

    lax.fori_loop(0, tsteps, step, 0)

    @pl.when(c == pl.num_programs(1) - 1)
    def _():
        sf_ref[...] = s_scr[...]


def _rwkv_scan(seqs, s0, nb, unroll, tsteps):
    bsz, t, _ = seqs[0].shape
    ng = bsz // nb
    t2 = t // unroll
    sub = unroll * nb

    def pack(a):
        a = a.reshape(ng, nb, t2, unroll, A_WIDTH)
        return jnp.transpose(a, (2, 0, 3, 1, 4)).reshape(t2, ng * sub, A_WIDTH)

    def unpack(a):
        a = a.reshape(t2, ng, unroll, nb, A_WIDTH)
        return jnp.transpose(a, (1, 3, 0, 2, 4)).reshape(bsz, t, A_WIDTH)

    seq = pl.BlockSpec((tsteps, sub, A_WIDTH), lambda i, c: (c, i, 0))
    st = pl.BlockSpec((nb, A_HEAD, A_WIDTH), lambda i, c: (i, 0, 0))
    y, sf = pl.pallas_call(
        functools.partial(_rwkv_scan_kernel, nb=nb, unroll=unroll, tsteps=tsteps),
        out_shape=[jax.ShapeDtypeStruct((t2, ng * sub, A_WIDTH), F32), jax.ShapeDtypeStruct((bsz, A_HEAD, A_WIDTH), F32)],
        grid=(ng, t2 // tsteps),
        in_specs=[seq] * 6 + [st],
        out_specs=[seq, st],
        scratch_shapes=[pltpu.VMEM((nb, A_HEAD, A_WIDTH), F32)],
        compiler_params=_cp(("parallel", "arbitrary")),
        name="rwkv_scan",
    )(*[pack(a) for a in seqs], s0)
    return unpack(y), sf


def _retention_kernel(q_ref, k_ref, v_ref, g_ref, cos_ref, sin_ref, dm_ref, qd_ref, kd_ref, gc_ref, r0_ref,
                      o_ref, rn_ref, r_scr):
    c = pl.program_id(1)

    @pl.when(c == 0)
    def _():
        r_scr[...] = r0_ref[0]

    cos = cos_ref[...]
    sin = sin_ref[...]

    def rot(x):
        return x * cos + pltpu.roll(x, B_QK_HEAD // 2, 1) * sin

    for h in range(B_HEADS):
        qs = slice(h * B_QK_HEAD, (h + 1) * B_QK_HEAD)
        vs = slice(h * B_V_HEAD, (h + 1) * B_V_HEAD)
        q = rot(q_ref[:, qs])
        k = rot(k_ref[:, qs]) * (B_QK_HEAD ** -0.5)
        v = v_ref[:, vs].astype(BF16)
        rs = r_scr[h]
        qb = q.astype(BF16)
        s = lax.dot_general(qb, k.astype(BF16), (((1,), (1,)), ((), ())), preferred_element_type=F32) * dm_ref[h]
        inner = jnp.dot(s.astype(BF16), v, preferred_element_type=F32)
        cross = jnp.dot(qb, rs.astype(BF16), preferred_element_type=F32) * qd_ref[h]
        kd = (k * kd_ref[h]).astype(BF16)
        r_scr[h] = gc_ref[h] * rs + lax.dot_general(kd, v, (((0,), (0,)), ((), ())), preferred_element_type=F32)
        o = inner + cross
        o = o * lax.rsqrt(jnp.mean(o * o, axis=-1, keepdims=True) + NORM_EPS)
        gate = g_ref[:, vs]
        o_ref[:, vs] = gate * _sigmoid(gate) * o

    @pl.when(c == pl.num_programs(1) - 1)
    def _():
        rn_ref[0] = r_scr[...]


def _retention(z, r0, tabs, row0, bsz, t, chunk):
    cos, sin, dm, qd, kd, gc = tabs
    nc = t // chunk
    rb = row0 // chunk
    seq = lambda w, col: pl.BlockSpec((chunk, w), lambda b, c: (rb + b * nc + c, col // w))
    tab3 = lambda a: pl.BlockSpec(a.shape, lambda b, c: (0, 0, 0))
    st = pl.BlockSpec((1, B_HEADS, B_QK_HEAD, B_V_HEAD), lambda b, c: (b, 0, 0, 0))
    return pl.pallas_call(
        _retention_kernel,
        out_shape=[jax.ShapeDtypeStruct((bsz * t, B_V_WIDTH), F32),
                   jax.ShapeDtypeStruct((bsz, B_HEADS, B_QK_HEAD, B_V_HEAD), F32)],
        grid=(bsz, nc),
        in_specs=[
            seq(B_QK_WIDTH, COL_QB), seq(B_QK_WIDTH, COL_KB), seq(B_V_WIDTH, COL_VB), seq(B_V_WIDTH, COL_GB),
            pl.BlockSpec((chunk, B_QK_HEAD), lambda b, c: (c, 0)),
            pl.BlockSpec((chunk, B_QK_HEAD), lambda b, c: (c, 0)),
            tab3(dm), tab3(qd), tab3(kd), tab3(gc), st,
        ],
        out_specs=[pl.BlockSpec((chunk, B_V_WIDTH), lambda b, c: (b * nc + c, 0)), st],
        scratch_shapes=[pltpu.VMEM((B_HEADS, B_QK_HEAD, B_V_HEAD), F32)],
        compiler_params=_cp(("parallel", "arbitrary")),
        name="retention",
    )(z, z, z, z, cos, sin, dm, qd, kd, gc, r0)


def _retention_tables(pos, chunk):
    half = B_QK_HEAD // 2
    inv = jnp.power(ROT_BASE, -jnp.linspace(0.0, 1.0, half, dtype=F32))
    ang = pos[:, None] * inv[None, :]
    cos = jnp.cos(ang)
    sin = jnp.sin(ang)
    cos2 = jnp.concatenate([cos, cos], axis=-1)
    sin2 = jnp.concatenate([-sin, sin], axis=-1)
    log_g = jnp.log(1.0 - jnp.power(2.0, -5.0 - jnp.arange(B_HEADS, dtype=F32)))
    idx = jnp.arange(chunk, dtype=F32)
    diff = idx[:, None] - idx[None, :]
    dmask = jnp.where(diff >= 0, jnp.exp(log_g[:, None, None] * jnp.maximum(diff, 0.0)), 0.0)
    q_dec = jnp.exp(log_g[:, None] * (idx[None, :] + 1.0))
    k_dec = jnp.exp(log_g[:, None] * (chunk - 1.0 - idx[None, :]))
    g_c = jnp.exp(log_g * chunk)
    qd = jnp.broadcast_to(q_dec[:, :, None], (B_HEADS, chunk, B_V_HEAD))
    kd = jnp.broadcast_to(k_dec[:, :, None], (B_HEADS, chunk, B_QK_HEAD))
    gc = jnp.broadcast_to(g_c[:, None, None], (B_HEADS, B_QK_HEAD, B_V_HEAD))
    return cos2, sin2, dmask, qd, kd, gc


def _merge1_kernel(y_ref, r_ref, k_ref, v_ref, g_ref, ob_ref, gma_ref, gmb_ref, lw_ref, lb_ref, rk_ref, wpa_ref,
                   wpb_ref, m_ref, oa_scr, ob_scr):
    @pl.when(pl.program_id(1) == 0)
    def _():
        ones = _seg_ones()
        y = y_ref[...]
        mean = _segsum64(y, ones) * (1.0 / A_HEAD)
        yc = y - mean
        var = _segsum64(yc * yc, ones) * (1.0 / A_HEAD)
        yn = yc * lax.rsqrt(var + A_GN_EPS) * lw_ref[...] + lb_ref[...]
        bonus = _segsum64(r_ref[...] * k_ref[...] * rk_ref[...], ones) * v_ref[...]
        oa_scr[...] = ((yn + bonus) * g_ref[...]).astype(BF16)
        ob_scr[...] = ob_ref[...].astype(BF16)

    pa = jnp.dot(oa_scr[...], wpa_ref[...], preferred_element_type=F32)
    pb = jnp.dot(ob_scr[...], wpb_ref[...], preferred_element_type=F32)
    m_ref[...] = (_sigmoid(gma_ref[...]) * pa + _sigmoid(gmb_ref[...]) * pb).astype(m_ref.dtype)


def _merge1(y, r, k2, v, g, ob, z, lnx_w, lnx_b, r_k, wpa, wpb, tm=256, tn=512):
    n = y.shape[0]
    a_blk = pl.BlockSpec((tm, A_WIDTH), lambda i, j: (i, 0))
    row = pl.BlockSpec((1, A_WIDTH), lambda i, j: (0, 0))
    return pl.pallas_call(
        _merge1_kernel,
        out_shape=jax.ShapeDtypeStruct((n, D_MODEL), F32),
        grid=(n // tm, D_MODEL // tn),
        in_specs=[
            a_blk, a_blk, a_blk, a_blk, a_blk,
            pl.BlockSpec((tm, B_V_WIDTH), lambda i, j: (i, 0)),
            pl.BlockSpec((tm, tn), lambda i, j: (i, COL_GMA // tn + j)),
            pl.BlockSpec((tm, tn), lambda i, j: (i, COL_GMB // tn + j)),
            row, row, row,
            pl.BlockSpec((A_WIDTH, tn), lambda i, j: (0, j)),
            pl.BlockSpec((B_V_WIDTH, tn), lambda i, j: (0, j)),
        ],
        out_specs=pl.BlockSpec((tm, tn), lambda i, j: (i, j)),
        scratch_shapes=[pltpu.VMEM((tm, A_WIDTH), BF16), pltpu.VMEM((tm, B_V_WIDTH), BF16)],
        compiler_params=_cp(("parallel", "arbitrary")),
        name="merge1",
    )(y, r, k2, v, g, ob, z, z, lnx_w, lnx_b, r_k, wpa, wpb)


def _merge2_kernel(m_ref, x_ref, w_ref, g_ref, x1_ref, xn_ref):
    x1 = x_ref[...] + jnp.dot(m_ref[...].astype(BF16), w_ref[...], preferred_element_type=F32)
    x1_ref[...] = x1
    ms = jnp.mean(x1 * x1, axis=-1, keepdims=True)
    xn_ref[...] = x1 * lax.rsqrt(ms + NORM_EPS) * g_ref[...]


def _merge2(m, x, w_out, norm2, tm=256):
    n = x.shape[0]
    blk = pl.BlockSpec((tm, D_MODEL), lambda i: (i, 0))
    return pl.pallas_call(
        _merge2_kernel,
        out_shape=[jax.ShapeDtypeStruct((n, D_MODEL), F32), jax.ShapeDtypeStruct((n, D_MODEL), F32)],
        grid=(n // tm,),
        in_specs=[blk, blk, pl.BlockSpec((D_MODEL, D_MODEL), lambda i: (0, 0)),
                  pl.BlockSpec((1, D_MODEL), lambda i: (0, 0))],
        out_specs=[blk, blk],
        compiler_params=_cp(("parallel",)),
        name="merge2",
    )(m, x, w_out, norm2)


_PAIRS = [(i, j) for i in range(PEER_TOPK) for j in range(PEER_TOPK) if (i + 1) * (j + 1) <= PEER_TOPK]


def _top16_rows(s):
    rows = []
    cur = s
    for i in range(PEER_TOPK):
        m = jnp.max(cur, axis=0, keepdims=True)
        rows.append(m)
        if i + 1 < PEER_TOPK:
            cur = jnp.where(cur == m, -jnp.inf, cur)
    return rows


def _peer_topk_kernel(xn_ref, wq_ref, keys_ref, s1_ref, s2_ref, st_ref):
    q = jnp.dot(xn_ref[...].astype(BF16), wq_ref[...], preferred_element_type=F32)
    for h in range(PEER_HEADS):
        tops = []
        for p in range(2):
            c0 = (h * 2 + p) * N_KEYS
            qhp = q[:, c0:c0 + N_KEYS].astype(BF16)
            st = lax.dot_general(keys_ref[h, p], qhp, (((1,), (1,)), ((), ())), preferred_element_type=F32)
            (s1_ref if p == 0 else s2_ref)[h] = st
            tops.append(_top16_rows(st))
        a, b = tops
        cands = [a[i] + b[j] for (i, j) in _PAIRS]
        cur = cands
        m = None
        for it in range(PEER_TOPK):
            m = cur[0]
            for cnd in cur[1:]:
                m = jnp.maximum(m, cnd)
            if it + 1 < PEER_TOPK:
                cur = [jnp.where(cnd == m, -jnp.inf, cnd) for cnd in cur]
        tau = m
        top = a[0] + b[0]
        zsum = jnp.zeros_like(tau)
        for cnd in cands:
            zsum = zsum + jnp.where(cnd >= tau, jnp.exp(cnd - top), 0.0)
        st_ref[0, pl.ds(h, 1), :] = tau
        st_ref[1, pl.ds(h, 1), :] = a[0]
        st_ref[2, pl.ds(h, 1), :] = b[0]
        st_ref[3, pl.ds(h, 1), :] = 1.0 / zsum


def _peer_topk(xn, wq, keys, tm=256):
    n = xn.shape[0]
    s_shape = jax.ShapeDtypeStruct((PEER_HEADS, N_KEYS, n), F32)
    s_spec = pl.BlockSpec((PEER_HEADS, N_KEYS, tm), lambda i: (0, 0, i))
    return pl.pallas_call(
        _peer_topk_kernel,
        out_shape=[s_shape, s_shape, jax.ShapeDtypeStruct((4, PEER_HEADS, n), F32)],
        grid=(n // tm,),
        in_specs=[pl.BlockSpec((tm, D_MODEL), lambda i: (i, 0)),
                  pl.BlockSpec((D_MODEL, D_MODEL), lambda i: (0, 0)),
                  pl.BlockSpec(keys.shape, lambda i: (0, 0, 0, 0))],
        out_specs=[s_spec, s_spec, pl.BlockSpec((4, PEER_HEADS, tm), lambda i: (0, 0, i))],
        compiler_params=_cp(("parallel",)),
        name="peer_topk",
    )(xn, wq, keys)


def _peer_dense_kernel(xn_ref, s1_ref, s2_ref, st_ref, u_ref, vt_ref, o_ref, acc_scr, b_scr, w_scr, *, ec, tm):
    c = pl.program_id(1)

    @pl.when(c == 0)
    def _():
        acc_scr[...] = jnp.zeros_like(acc_scr)
        for h in range(PEER_HEADS):
            b_scr[h] = jnp.exp(s2_ref[h] - st_ref[2, pl.ds(h, 1), :]) * st_ref[3, pl.ds(h, 1), :]

    st = lax.dot_general(u_ref[...], xn_ref[...], (((1,), (1,)), ((), ())), preferred_element_type=F32)
    n1 = ec // N_KEYS
    for ii in range(n1):
        i1 = c * n1 + ii
        for tcol in range(tm // LANE):
            cs = slice(tcol * LANE, (tcol + 1) * LANE)
            g = jnp.zeros((N_KEYS, LANE), F32)
            for h in range(PEER_HEADS):
                s1row = s1_ref[h, pl.ds(pl.multiple_of(c * n1, 8), n1), cs][ii:ii + 1, :]
                theta = st_ref[0, pl.ds(h, 1), cs] - s1row
                arow = jnp.exp(s1row - st_ref[1, pl.ds(h, 1), cs])
                g = g + jnp.where(s2_ref[h, :, cs] >= theta, arow * b_scr[h, :, cs], 0.0)
            x = st[ii * N_KEYS:(ii + 1) * N_KEYS, cs]
            gelu = 0.5 * x * (1.0 + lax.erf(x * np.float32(np.sqrt(0.5))))
            w_scr[ii * N_KEYS:(ii + 1) * N_KEYS, cs] = (g * gelu).astype(BF16)
    acc_scr[...] += jnp.dot(vt_ref[...], w_scr[...], preferred_element_type=F32)

    @pl.when(c == pl.num_programs(1) - 1)
    def _():
        o_ref[...] = acc_scr[...].T


def _peer_dense(xn_bf, s1, s2, stats, u_bf, vt_bf, tm=256, ec=1024):
    n = xn_bf.shape[0]
    ne = u_bf.shape[0]
    s_spec = pl.BlockSpec((PEER_HEADS, N_KEYS, tm), lambda i, c: (0, 0, i))
    return pl.pallas_call(
        functools.partial(_peer_dense_kernel, ec=ec, tm=tm),
        out_shape=jax.ShapeDtypeStruct((n, D_MODEL), F32),
        grid=(n // tm, ne // ec),
        in_specs=[
            pl.BlockSpec((tm, D_MODEL), lambda i, c: (i, 0)),
            s_spec, s_spec,
            pl.BlockSpec((4, PEER_HEADS, tm), lambda i, c: (0, 0, i)),
            pl.BlockSpec((ec, D_MODEL), lambda i, c: (c, 0)),
            pl.BlockSpec((D_MODEL, ec), lambda i, c: (0, c)),
        ],
        out_specs=pl.BlockSpec((tm, D_MODEL), lambda i, c: (i, 0)),
        scratch_shapes=[pltpu.VMEM((D_MODEL, tm), F32), pltpu.VMEM((PEER_HEADS, N_KEYS, tm), F32),
                        pltpu.VMEM((ec, tm), BF16)],
        compiler_params=_cp(("parallel", "arbitrary")),
        name="peer_dense",
    )(xn_bf, s1, s2, stats, u_bf, vt_bf)


def _final_kernel(x1_ref, p_ref, g_ref, y_ref):
    x = x1_ref[...] + p_ref[...]
    ms = jnp.mean(x * x, axis=-1, keepdims=True)
    y_ref[...] = x * lax.rsqrt(ms + NORM_EPS) * g_ref[...]


def _final(x1, p, g, tm=512):
    n = x1.shape[0]
    blk = pl.BlockSpec((tm, D_MODEL), lambda i: (i, 0))
    return pl.pallas_call(
        _final_kernel,
        out_shape=jax.ShapeDtypeStruct((n, D_MODEL), F32),
        grid=(n // tm,),
        in_specs=[blk, blk, pl.BlockSpec((1, D_MODEL), lambda i: (0, 0))],
        out_specs=blk,
        compiler_params=_cp(("parallel",)),
        name="final",
    )(x1, p, g)


def _pad_cols(a, w):
    return jnp.pad(a, ((0, 0), (0, w - a.shape[1])))


def _za_layout(a):
    r3 = a[:, :3 * A_WIDTH]
    wl = _pad_cols(a[:, 3 * A_WIDTH:3 * A_WIDTH + A_DECAY_LORA], COL_AL - COL_WL)
    al = _pad_cols(a[:, 3 * A_WIDTH + A_DECAY_LORA:3 * A_WIDTH + A_DECAY_LORA + A_ICL_LORA], COL_GL - COL_AL)
    gl = _pad_cols(a[:, 3 * A_WIDTH + A_DECAY_LORA + A_ICL_LORA:SHIFT_W], ZA_W - COL_GL)
    return jnp.concatenate([r3, wl, al, gl], axis=1)


def _za_unlayout(a):
    return jnp.concatenate([a[:, :3 * A_WIDTH], a[:, COL_WL:COL_WL + A_DECAY_LORA], a[:, COL_AL:COL_AL + A_ICL_LORA],
                            a[:, COL_GL:COL_GL + A_GATE_LORA]], axis=1)


def _pad_rows(a, n):
    return jnp.pad(a, ((0, n - a.shape[0]), (0, 0)))


def _layer(x, shift0_p, wkv0_p, ret0_p, shift0_s, wkv0_s, ret0_s, bp, tp, bs, ts, lw):
    (norm1, w_in, mu_shift, w0, w_w2, a0, w_a2, w_g2, k_k, k_a, r_k, lnx_w, lnx_b, w_pa, w_pb, w_out, norm2,
     peer_wq, peer_keys, peer_u, peer_v) = lw
    n_p = bp * tp
    n_s = bs * ts
    row = lambda a: a[None, :]

    wcat = jnp.concatenate([
        _za_layout(w_in[:, :SHIFT_W]), jnp.zeros((D_MODEL, COL_QB - ZA_W), F32), w_in[:, SHIFT_W:]], axis=1).astype(BF16)
    z = _inproj(x, row(norm1), wcat)

    za = z[:, :ZA_W]
    za_p = za[:n_p].reshape(bp, tp, ZA_W)
    za_s = za[n_p:].reshape(bs, ts, ZA_W)
    zprev_p = jnp.concatenate([_za_layout(shift0_p)[:, None, :], za_p[:, :-1]], axis=1).reshape(n_p, ZA_W)
    zprev_s = jnp.concatenate([_za_layout(shift0_s)[:, None, :], za_s[:, :-1]], axis=1).reshape(n_s, ZA_W)
    zprev = jnp.concatenate([zprev_p, zprev_s], axis=0)
    shift_p = _za_unlayout(za_p[:, -1])
    shift_s = _za_unlayout(za_s[:, -1])

    r, w, k2, v, kk, kka, g = _rwkv_prep(
        z, zprev, _za_layout(row(mu_shift)), row(w0), _pad_rows(w_w2, COL_AL - COL_WL).astype(BF16), row(a0),
        _pad_rows(w_a2, COL_GL - COL_AL).astype(BF16), _pad_rows(w_g2, ZA_W - COL_GL).astype(BF16), row(k_k), row(k_a))

    def to_tiles(s):
        return jnp.swapaxes(s, 1, 2).reshape(s.shape[0], A_HEAD, A_WIDTH)

    def from_tiles(s):
        return jnp.swapaxes(s.reshape(s.shape[0], A_HEAD, A_HEADS, A_HEAD), 1, 2)

    seqs = (r, w, k2, v, kk, kka)
    y_p, wkv_p = _rwkv_scan([a[:n_p].reshape(bp, tp, A_WIDTH) for a in seqs], to_tiles(wkv0_p), nb=bp,
                            unroll=8 // bp, tsteps=64)
    y_s, wkv_s = _rwkv_scan([a[n_p:].reshape(bs, ts, A_WIDTH) for a in seqs], to_tiles(wkv0_s), nb=8, unroll=1,
                            tsteps=ts)
    y = jnp.concatenate([y_p.reshape(n_p, A_WIDTH), y_s.reshape(n_s, A_WIDTH)], axis=0)

    tabs_p = _retention_tables(jnp.arange(tp, dtype=F32), RET_CHUNK)
    tabs_s = _retention_tables(PAST_LEN + jnp.arange(ts, dtype=F32), ts)
    ob_p, ret_p = _retention(z, ret0_p, tabs_p, 0, bp, tp, RET_CHUNK)
    ob_s, ret_s = _retention(z, ret0_s, tabs_s, n_p, bs, ts, ts)
    ob = jnp.concatenate([ob_p, ob_s], axis=0)

    m = _merge1(y, r, k2, v, g, ob, z, row(lnx_w), row(lnx_b), row(r_k), w_pa.astype(BF16), w_pb.astype(BF16))
    x1, xn = _merge2(m, x, w_out.astype(BF16), row(norm2))
    s1, s2, stats = _peer_topk(xn, peer_wq.astype(BF16), peer_keys.astype(BF16))
    peer = _peer_dense(xn.astype(BF16), s1, s2, stats, peer_u.astype(BF16), peer_v.T.astype(BF16))
    return x1, peer, (shift_p, from_tiles(wkv_p), ret_p), (shift_s, from_tiles(wkv_s), ret_s)


def kernel(x_prompt, x_sample, state_shift, state_wkv, state_ret, norm1, w_in, mu_shift, w0, w_w2, a0, w_a2, w_g2, k_k, k_a, r_k, lnx_w, lnx_b, w_pa, w_pb, w_out, norm2, peer_wq, peer_keys, peer_u, peer_v, norm_f):
    weights = (norm1, w_in, mu_shift, w0, w_w2, a0, w_a2, w_g2, k_k, k_a, r_k, lnx_w, lnx_b, w_pa, w_pb, w_out,
               norm2, peer_wq, peer_keys, peer_u, peer_v)
    depth = norm1.shape[0]
    bp, tp, d = x_prompt.shape
    bs, ts, _ = x_sample.shape
    n_p = bp * tp
    x = jnp.concatenate([x_prompt.reshape(n_p, d), x_sample.reshape(bs * ts, d)], axis=0)
    outs_p, outs_s = [], []
    for layer in range(depth):
        lw = [w[layer] for w in weights]
        x1, peer, st_p, st_s = _layer(
            x, jnp.zeros((bp, SHIFT_W), F32), jnp.zeros((bp, A_HEADS, A_HEAD, A_HEAD), F32),
            jnp.zeros((bp, B_HEADS, B_QK_HEAD, B_V_HEAD), F32), state_shift[layer], state_wkv[layer],
            state_ret[layer], bp, tp, bs, ts, lw)
        outs_p.append(st_p)
        outs_s.append(st_s)
        if layer + 1 < depth:
            x = x1 + peer
    y = _final(x1, peer, norm_f[None, :])
    stack = lambda outs, i: jnp.stack([o[i] for o in outs])
    return (y[:n_p].reshape(bp, tp, d), y[n_p:].reshape(bs, ts, d),
            stack(outs_p, 0), stack(outs_p, 1), stack(outs_p, 2),
            stack(outs_s, 0), stack(outs_s, 1), stack(outs_s, 2))
```

```python
import functools

import jax
import jax.numpy as jnp
import numpy as np
from jax import lax
from jax.experimental import pallas as pl
from jax.experimental.pallas import tpu as pltpu

F32 = jnp.float32
BF16 = jnp.bfloat16

D_MODEL = 2048
A_HEAD = 64
A_WIDTH = 1024
A_HEADS = 16
A_DECAY_LORA = 64
A_ICL_LORA = 64
A_GATE_LORA = 160
A_GN_EPS = 64e-5
B_HEADS = 8
B_QK_HEAD = 128
B_V_HEAD = 256
B_QK_WIDTH = 1024
B_V_WIDTH = 2048
RET_CHUNK = 128
ROT_BASE = 10000.0
N_KEYS = 128
N_EXPERTS = N_KEYS * N_KEYS
PEER_HEADS = 8
PEER_TOPK = 16
NORM_EPS = 1e-6
PAST_LEN = 16384
SHIFT_W = 3 * A_WIDTH + A_DECAY_LORA + A_ICL_LORA + A_GATE_LORA

LANE = 128
SUBLANE = 8
SEG_W = 256
ZA_W = 3584
COL_WL = 3072
COL_AL = 3200
COL_GL = 3328
COL_QB = 4096
COL_KB = 5120
COL_VB = 6144
COL_GB = 8192
COL_GMA = 10240
COL_GMB = 12288
W_TOT = 14336
VMEM_LIMIT = 56 * 1024 * 1024


def _cp(sem):
    return pltpu.CompilerParams(dimension_semantics=sem, vmem_limit_bytes=VMEM_LIMIT)


def _sigmoid(x):
    return 1.0 / (1.0 + jnp.exp(-x))


def _inproj_kernel(x_ref, g_ref, w_ref, o_ref, hn_ref):
    @pl.when(pl.program_id(1) == 0)
    def _():
        x = x_ref[...]
        ms = jnp.mean(x * x, axis=-1, keepdims=True)
        hn_ref[...] = (x * lax.rsqrt(ms + NORM_EPS) * g_ref[...]).astype(BF16)

    o_ref[...] = jnp.dot(hn_ref[...], w_ref[...], preferred_element_type=F32)


def _inproj(x, g, w, tm=1024, tn=512):
    n, d = x.shape
    wt = w.shape[1]
    return pl.pallas_call(
        _inproj_kernel,
        out_shape=jax.ShapeDtypeStruct((n, wt), F32),
        grid=(n // tm, wt // tn),
        in_specs=[
            pl.BlockSpec((tm, d), lambda i, j: (i, 0)),
            pl.BlockSpec((1, d), lambda i, j: (0, 0)),
            pl.BlockSpec((d, tn), lambda i, j: (0, j)),
        ],
        out_specs=pl.BlockSpec((tm, tn), lambda i, j: (i, j)),
        scratch_shapes=[pltpu.VMEM((tm, d), BF16)],
        compiler_params=_cp(("parallel", "arbitrary")),
        name="inproj",
    )(x, g, w)


def _seg_ones():
    r = lax.broadcasted_iota(jnp.int32, (LANE, LANE), 0) // A_HEAD
    c = lax.broadcasted_iota(jnp.int32, (LANE, LANE), 1) // A_HEAD
    return (r == c).astype(F32)


def _segsum64(x, ones):
    parts = []
    for j in range(x.shape[1] // LANE):
        parts.append(jnp.dot(x[:, j * LANE:(j + 1) * LANE], ones, preferred_element_type=F32,
                             precision=lax.Precision.HIGHEST))
    return jnp.concatenate(parts, axis=1)


def _rwkv_prep_kernel(z_ref, first_ref, mu_ref, w0_ref, ww2_ref, a0_ref, wa2_ref, wg2_ref, kk_ref_, ka_ref,
                      r_o, w_o, k_o, v_o, kk_o, nkka_o, g_o, carry_scr, *, tm, seq_len):
    z = z_ref[...]
    rolled = pltpu.roll(z, 1, 0)
    rows = lax.broadcasted_iota(jnp.int32, z.shape, 0)
    if seq_len >= tm:
        @pl.when(pl.program_id(1) == 0)
        def _():
            carry_scr[0:1, :] = first_ref[0]

        zprev = jnp.where(rows == 0, carry_scr[0:1, :], rolled)
        carry_scr[0:1, :] = z[tm - 1:tm, :]
    else:
        zprev = jnp.where(rows % seq_len == 0, first_ref[...], rolled)
    zs = z + (zprev - z) * mu_ref[...]
    r = zs[:, 0:A_WIDTH]
    k = zs[:, A_WIDTH:2 * A_WIDTH]
    v = zs[:, 2 * A_WIDTH:3 * A_WIDTH]
    wl = zs[:, COL_WL:COL_AL]
    al = zs[:, COL_AL:COL_GL]
    gl = zs[:, COL_GL:ZA_W]
    wpre = w0_ref[...] + jnp.dot(jnp.tanh(wl).astype(BF16), ww2_ref[...], preferred_element_type=F32)
    nw = -wpre
    softplus = jnp.maximum(nw, 0.0) + jnp.log1p(jnp.exp(-jnp.abs(nw)))
    wlog = -softplus - 0.5
    decay = jnp.exp(-jnp.exp(wlog))
    a = _sigmoid(a0_ref[...] + jnp.dot(al.astype(BF16), wa2_ref[...], preferred_element_type=F32))
    g = jnp.dot(_sigmoid(gl).astype(BF16), wg2_ref[...], preferred_element_type=F32)
    kk = k * kk_ref_[...]
    ss = _segsum64(kk * kk, _seg_ones())
    kk = kk / jnp.maximum(jnp.sqrt(ss), 1e-12)
    k2 = k * (1.0 + (a - 1.0) * ka_ref[...])
    r_o[...] = r
    w_o[...] = decay
    k_o[...] = k2
    v_o[...] = v
    kk_o[...] = kk
    nkka_o[...] = -(kk * a)
    g_o[...] = g


def _rwkv_prep(z, shift0_l, bsz, t, pw, tm=256):
    n = bsz * t
    nt = max(t // tm, 1)
    if t >= tm:
        first = shift0_l[:, None, :]
        first_spec = pl.BlockSpec((1, 1, ZA_W), lambda i, j: (i, 0, 0))
    else:
        first = jnp.pad(shift0_l[:, None, :], ((0, 0), (0, t - 1), (0, 0))).reshape(n, ZA_W)
        first_spec = pl.BlockSpec((tm, ZA_W), lambda i, j: (i * nt + j, 0))
    row = lambda w: pl.BlockSpec((1, w), lambda i, j: (0, 0))
    full = lambda a: pl.BlockSpec(a.shape, lambda i, j: (0, 0))
    out = jax.ShapeDtypeStruct((n, A_WIDTH), F32)
    return pl.pallas_call(
        functools.partial(_rwkv_prep_kernel, tm=tm, seq_len=t),
        out_shape=[out] * 7,
        grid=(n // (tm * nt), nt),
        in_specs=[
            pl.BlockSpec((tm, ZA_W), lambda i, j: (i * nt + j, 0)),
            first_spec,
            row(ZA_W), row(A_WIDTH), full(pw["ww2"]), row(A_WIDTH), full(pw["wa2"]), full(pw["wg2"]),
            row(A_WIDTH), row(A_WIDTH),
        ],
        out_specs=[pl.BlockSpec((tm, A_WIDTH), lambda i, j: (i * nt + j, 0))] * 7,
        scratch_shapes=[pltpu.VMEM((SUBLANE, ZA_W), F32)],
        compiler_params=_cp(("parallel", "arbitrary")),
        name="rwkv_prep",
    )(z, first, pw["mu"], pw["w0"], pw["ww2"], pw["a0"], pw["wa2"], pw["wg2"], pw["k_k"], pw["k_a"])


def _rwkv_scan_kernel(r_ref, w_ref, k_ref, v_ref, kk_ref, nkka_ref, s0_ref, m4_ref, y_ref, sf_ref, s_scr, xa_scr,
                      xb_scr, y8_scr, *, nb, tc):
    c = pl.program_id(1)

    @pl.when(c == 0)
    def _():
        s_scr[...] = s0_ref[...]

    lane = lax.broadcasted_iota(jnp.int32, (A_HEAD, LANE), 1)
    sub = lax.broadcasted_iota(jnp.int32, (A_HEAD, LANE), 0)
    lo = lane < A_HEAD
    diag = (lane & (A_HEAD - 1)) == sub
    diag_lo = jnp.logical_and(diag, lo)
    diag_hi = jnp.logical_and(diag, jnp.logical_not(lo))
    nq = A_WIDTH // SEG_W
    npair = A_WIDTH // LANE

    def step(t8, carry):
        base = pl.multiple_of(t8 * SUBLANE, SUBLANE)
        for i in range(SUBLANE):
            row = lambda ref, b, lanes: ref[b, pl.ds(base, SUBLANE), lanes][i:i + 1, :]
            for b in range(nb):
                for j in range(nq):
                    q = slice(j * SEG_W, (j + 1) * SEG_W)
                    r0 = (b * nq + j) * A_HEAD
                    xa_scr[r0:r0 + A_HEAD, :] = (s_scr[b, :, q] * row(kk_ref, b, q)).astype(BF16)
            sa_all = jnp.dot(xa_scr[...], m4_ref[...], preferred_element_type=F32)
            for b in range(nb):
                for j in range(npair):
                    sl = slice(j * LANE, (j + 1) * LANE)
                    r0 = (b * nq + j // 2) * A_HEAD
                    half = slice((j % 2) * LANE, (j % 2 + 1) * LANE)
                    vrow = row(v_ref, b, sl)
                    v_lo = jnp.sum(jnp.where(diag_lo, vrow, 0.0), axis=1, keepdims=True)
                    v_hi = jnp.sum(jnp.where(diag_hi, vrow, 0.0), axis=1, keepdims=True)
                    vb = jnp.where(lo, v_lo, v_hi)
                    s = (s_scr[b, :, sl] * row(w_ref, b, sl) + sa_all[r0:r0 + A_HEAD, half] * row(nkka_ref, b, sl)
                         + vb * row(k_ref, b, sl))
                    s_scr[b, :, sl] = s
                    xb_scr[r0:r0 + A_HEAD, half] = (s * row(r_ref, b, sl)).astype(BF16)
            y_all = jnp.dot(xb_scr[...], m4_ref[...], preferred_element_type=F32)
            for b in range(nb):
                for j in range(npair):
                    sl = slice(j * LANE, (j + 1) * LANE)
                    r0 = (b * nq + j // 2) * A_HEAD
                    half = slice((j % 2) * LANE, (j % 2 + 1) * LANE)
                    yb = y_all[r0:r0 + A_HEAD, half]
                    y8_scr[b, i:i + 1, sl] = jnp.sum(jnp.where(diag, yb, 0.0), axis=0, keepdims=True)
        for b in range(nb):
            y_ref[b, pl.ds(base, SUBLANE), :] = y8_scr[b]
        return carry

    lax.fori_loop(0, tc // SUBLANE, step, 0)

    @pl.when(c == pl.num_programs(1) - 1)
    def _():
        sf_ref[...] = s_scr[...]


def _rwkv_scan(seqs, s0, nb, tc):
    bsz, t, _ = seqs[0].shape
    seq = pl.BlockSpec((nb, tc, A_WIDTH), lambda i, c: (i, c, 0))
    st = pl.BlockSpec((nb, A_HEAD, A_WIDTH), lambda i, c: (i, 0, 0))
    seg = jnp.arange(SEG_W, dtype=jnp.int32) // A_HEAD
    m4 = (seg[:, None] == seg[None, :]).astype(BF16)
    xrows = nb * (A_WIDTH // SEG_W) * A_HEAD
    return pl.pallas_call(
        functools.partial(_rwkv_scan_kernel, nb=nb, tc=tc),
        out_shape=[jax.ShapeDtypeStruct((bsz, t, A_WIDTH), F32), jax.ShapeDtypeStruct((bsz, A_HEAD, A_WIDTH), F32)],
        grid=(bsz // nb, t // tc),
        in_specs=[seq] * 6 + [st, pl.BlockSpec((SEG_W, SEG_W), lambda i, c: (0, 0))],
        out_specs=[seq, st],
        scratch_shapes=[pltpu.VMEM((nb, A_HEAD, A_WIDTH), F32), pltpu.VMEM((xrows, SEG_W), BF16),
                        pltpu.VMEM((xrows, SEG_W), BF16), pltpu.VMEM((nb, SUBLANE, A_WIDTH), F32)],
        compiler_params=_cp(("parallel", "arbitrary")),
        name="rwkv_scan",
    )(*seqs, s0, m4)


def _retention_kernel(q_ref, k_ref, v_ref, g_ref, cos_ref, sin_ref, dm_ref, qd_ref, kd_ref, gc_ref, r0_ref,
                      o_ref, rn_ref, r_scr):
    c = pl.program_id(1)

    @pl.when(c == 0)
    def _():
        r_scr[...] = r0_ref[0]

    cos = cos_ref[...]
    sin = sin_ref[...]

    def rot(x):
        return x * cos + pltpu.roll(x, B_QK_HEAD // 2, 1) * sin

    for h in range(B_HEADS):
        qs = slice(h * B_QK_HEAD, (h + 1) * B_QK_HEAD)
        vs = slice(h * B_V_HEAD, (h + 1) * B_V_HEAD)
        q = rot(q_ref[:, qs])
        k = rot(k_ref[:, qs]) * (B_QK_HEAD ** -0.5)
        v = v_ref[:, vs].astype(BF16)
        rs = r_scr[h]
        qb = q.astype(BF16)
        s = lax.dot_general(qb, k.astype(BF16), (((1,), (1,)), ((), ())), preferred_element_type=F32) * dm_ref[h]
        inner = jnp.dot(s.astype(BF16), v, preferred_element_type=F32)
        cross = jnp.dot(qb, rs.astype(BF16), preferred_element_type=F32) * qd_ref[h]
        kd = (k * kd_ref[h]).astype(BF16)
        r_scr[h] = gc_ref[h] * rs + lax.dot_general(kd, v, (((0,), (0,)), ((), ())), preferred_element_type=F32)
        o = inner + cross
        o = o * lax.rsqrt(jnp.mean(o * o, axis=-1, keepdims=True) + NORM_EPS)
        gate = g_ref[:, vs]
        o_ref[:, vs] = gate * _sigmoid(gate) * o

    @pl.when(c == pl.num_programs(1) - 1)
    def _():
        rn_ref[0] = r_scr[...]


def _retention(z, r0, tabs, bsz, t, chunk):
    cos, sin, dm, qd, kd, gc = tabs
    nc = t // chunk
    seq = lambda w, col: pl.BlockSpec((chunk, w), lambda b, c: (b * nc + c, col // w))
    tab3 = lambda a: pl.BlockSpec(a.shape, lambda b, c: (0, 0, 0))
    st = pl.BlockSpec((1, B_HEADS, B_QK_HEAD, B_V_HEAD), lambda b, c: (b, 0, 0, 0))
    return pl.pallas_call(
        _retention_kernel,
        out_shape=[jax.ShapeDtypeStruct((bsz * t, B_V_WIDTH), F32),
                   jax.ShapeDtypeStruct((bsz, B_HEADS, B_QK_HEAD, B_V_HEAD), F32)],
        grid=(bsz, nc),
        in_specs=[
            seq(B_QK_WIDTH, COL_QB), seq(B_QK_WIDTH, COL_KB), seq(B_V_WIDTH, COL_VB), seq(B_V_WIDTH, COL_GB),
            pl.BlockSpec((chunk, B_QK_HEAD), lambda b, c: (c, 0)),
            pl.BlockSpec((chunk, B_QK_HEAD), lambda b, c: (c, 0)),
            tab3(dm), tab3(qd), tab3(kd), tab3(gc), st,
        ],
        out_specs=[pl.BlockSpec((chunk, B_V_WIDTH), lambda b, c: (b * nc + c, 0)), st],
        scratch_shapes=[pltpu.VMEM((B_HEADS, B_QK_HEAD, B_V_HEAD), F32)],
        compiler_params=_cp(("parallel", "arbitrary")),
        name="retention",
    )(z, z, z, z, cos, sin, dm, qd, kd, gc, r0)


def _retention_tables(pos, chunk):
    half = B_QK_HEAD // 2
    inv = jnp.power(ROT_BASE, -jnp.linspace(0.0, 1.0, half, dtype=F32))
    ang = pos[:, None] * inv[None, :]
    cos = jnp.cos(ang)
    sin = jnp.sin(ang)
    cos2 = jnp.concatenate([cos, cos], axis=-1)
    sin2 = jnp.concatenate([-sin, sin], axis=-1)
    log_g = jnp.log(1.0 - jnp.power(2.0, -5.0 - jnp.arange(B_HEADS, dtype=F32)))
    idx = jnp.arange(chunk, dtype=F32)
    diff = idx[:, None] - idx[None, :]
    dmask = jnp.where(diff >= 0, jnp.exp(log_g[:, None, None] * jnp.maximum(diff, 0.0)), 0.0)
    q_dec = jnp.exp(log_g[:, None] * (idx[None, :] + 1.0))
    k_dec = jnp.exp(log_g[:, None] * (chunk - 1.0 - idx[None, :]))
    g_c = jnp.exp(log_g * chunk)
    qd = jnp.broadcast_to(q_dec[:, :, None], (B_HEADS, chunk, B_V_HEAD))
    kd = jnp.broadcast_to(k_dec[:, :, None], (B_HEADS, chunk, B_QK_HEAD))
    gc = jnp.broadcast_to(g_c[:, None, None], (B_HEADS, B_QK_HEAD, B_V_HEAD))
    return cos2, sin2, dmask, qd, kd, gc


def _merge1_kernel(y_ref, r_ref, k_ref, v_ref, g_ref, ob_ref, gma_ref, gmb_ref, lw_ref, lb_ref, rk_ref, wpa_ref,
                   wpb_ref, m_ref, oa_scr, ob_scr):
    @pl.when(pl.program_id(1) == 0)
    def _():
        ones = _seg_ones()
        y = y_ref[...]
        mean = _segsum64(y, ones) * (1.0 / A_HEAD)
        yc = y - mean
        var = _segsum64(yc * yc, ones) * (1.0 / A_HEAD)
        yn = yc * lax.rsqrt(var + A_GN_EPS) * lw_ref[...] + lb_ref[...]
        bonus = _segsum64(r_ref[...] * k_ref[...] * rk_ref[...], ones) * v_ref[...]
        oa_scr[...] = ((yn + bonus) * g_ref[...]).astype(BF16)
        ob_scr[...] = ob_ref[...].astype(BF16)

    pa = jnp.dot(oa_scr[...], wpa_ref[...], preferred_element_type=F32)
    pb = jnp.dot(ob_scr[...], wpb_ref[...], preferred_element_type=F32)
    m_ref[...] = (_sigmoid(gma_ref[...]) * pa + _sigmoid(gmb_ref[...]) * pb).astype(m_ref.dtype)


def _merge1(y, r, k2, v, g, ob, z, pw, tm=256, tn=512):
    n = y.shape[0]
    a_blk = pl.BlockSpec((tm, A_WIDTH), lambda i, j: (i, 0))
    row = pl.BlockSpec((1, A_WIDTH), lambda i, j: (0, 0))
    return pl.pallas_call(
        _merge1_kernel,
        out_shape=jax.ShapeDtypeStruct((n, D_MODEL), BF16),
        grid=(n // tm, D_MODEL // tn),
        in_specs=[
            a_blk, a_blk, a_blk, a_blk, a_blk,
            pl.BlockSpec((tm, B_V_WIDTH), lambda i, j: (i, 0)),
            pl.BlockSpec((tm, tn), lambda i, j: (i, COL_GMA // tn + j)),
            pl.BlockSpec((tm, tn), lambda i, j: (i, COL_GMB // tn + j)),
            row, row, row,
            pl.BlockSpec((A_WIDTH, tn), lambda i, j: (0, j)),
            pl.BlockSpec((B_V_WIDTH, tn), lambda i, j: (0, j)),
        ],
        out_specs=pl.BlockSpec((tm, tn), lambda i, j: (i, j)),
        scratch_shapes=[pltpu.VMEM((tm, A_WIDTH), BF16), pltpu.VMEM((tm, B_V_WIDTH), BF16)],
        compiler_params=_cp(("parallel", "arbitrary")),
        name="merge1",
    )(y, r, k2, v, g, ob, z, z, pw["lnx_w"], pw["lnx_b"], pw["r_k"], pw["wpa"], pw["wpb"])


def _merge2_kernel(m_ref, x_ref, w_ref, g_ref, x1_ref, xn_ref):
    x1 = x_ref[...] + jnp.dot(m_ref[...], w_ref[...], preferred_element_type=F32)
    x1_ref[...] = x1
    ms = jnp.mean(x1 * x1, axis=-1, keepdims=True)
    xn_ref[...] = (x1 * lax.rsqrt(ms + NORM_EPS) * g_ref[...]).astype(BF16)


def _merge2(m, x, w_out, norm2, tm=256):
    n = x.shape[0]
    blk = pl.BlockSpec((tm, D_MODEL), lambda i: (i, 0))
    return pl.pallas_call(
        _merge2_kernel,
        out_shape=[jax.ShapeDtypeStruct((n, D_MODEL), F32), jax.ShapeDtypeStruct((n, D_MODEL), BF16)],
        grid=(n // tm,),
        in_specs=[blk, blk, pl.BlockSpec((D_MODEL, D_MODEL), lambda i: (0, 0)),
                  pl.BlockSpec((1, D_MODEL), lambda i: (0, 0))],
        out_specs=[blk, blk],
        compiler_params=_cp(("parallel",)),
        name="merge2",
    )(m, x, w_out, norm2)


_PAIRS = [(i, j) for i in range(PEER_TOPK) for j in range(PEER_TOPK) if (i + 1) * (j + 1) <= PEER_TOPK]


def _top16_rows(s):
    rows = []
    cur = s
    for i in range(PEER_TOPK):
        m = jnp.max(cur, axis=0, keepdims=True)
        rows.append(m)
        if i + 1 < PEER_TOPK:
            cur = jnp.where(cur == m, -jnp.inf, cur)
    return rows


def _peer_topk_kernel(xn_ref, wq_ref, keys_ref, s1_ref, s2_ref, st_ref):
    q = jnp.dot(xn_ref[...], wq_ref[...], preferred_element_type=F32)
    for h in range(PEER_HEADS):
        tops = []
        for p in range(2):
            c0 = (h * 2 + p) * N_KEYS
            qhp = q[:, c0:c0 + N_KEYS].astype(BF16)
            st = lax.dot_general(keys_ref[h, p], qhp, (((1,), (1,)), ((), ())), preferred_element_type=F32)
            (s1_ref if p == 0 else s2_ref)[h] = st
            tops.append(_top16_rows(st))
        a, b = tops
        cands = [a[i] + b[j] for (i, j) in _PAIRS]
        cur = cands
        m = None
        for it in range(PEER_TOPK):
            m = cur[0]
            for cnd in cur[1:]:
                m = jnp.maximum(m, cnd)
            if it + 1 < PEER_TOPK:
                cur = [jnp.where(cnd == m, -jnp.inf, cnd) for cnd in cur]
        tau = m
        top = a[0] + b[0]
        zsum = jnp.zeros_like(tau)
        for cnd in cands:
            zsum = zsum + jnp.where(cnd >= tau, jnp.exp(cnd - top), 0.0)
        st_ref[0, pl.ds(h, 1), :] = tau
        st_ref[1, pl.ds(h, 1), :] = a[0]
        st_ref[2, pl.ds(h, 1), :] = b[0]
        st_ref[3, pl.ds(h, 1), :] = 1.0 / zsum


def _peer_topk(xn, wq, keys, tm=256):
    n = xn.shape[0]
    s_shape = jax.ShapeDtypeStruct((PEER_HEADS, N_KEYS, n), F32)
    s_spec = pl.BlockSpec((PEER_HEADS, N_KEYS, tm), lambda i: (0, 0, i))
    return pl.pallas_call(
        _peer_topk_kernel,
        out_shape=[s_shape, s_shape, jax.ShapeDtypeStruct((4, PEER_HEADS, n), F32)],
        grid=(n // tm,),
        in_specs=[pl.BlockSpec((tm, D_MODEL), lambda i: (i, 0)),
                  pl.BlockSpec((D_MODEL, D_MODEL), lambda i: (0, 0)),
                  pl.BlockSpec(keys.shape, lambda i: (0, 0, 0, 0))],
        out_specs=[s_spec, s_spec, pl.BlockSpec((4, PEER_HEADS, tm), lambda i: (0, 0, i))],
        compiler_params=_cp(("parallel",)),
        name="peer_topk",
    )(xn, wq, keys)


def _peer_dense_kernel(xn_ref, s1_ref, s2_ref, st_ref, u_ref, vt_ref, o_ref, acc_scr, b_scr, sc_scr, w_scr, *, ec, tm):
    c = pl.program_id(1)
    nc = pl.num_programs(1) - 2
    n1 = ec // N_KEYS

    @pl.when(c == 0)
    def _():
        acc_scr[...] = jnp.zeros_like(acc_scr)
        sc_scr[...] = jnp.zeros_like(sc_scr)
        w_scr[...] = jnp.zeros_like(w_scr)
        for h in range(PEER_HEADS):
            b_scr[h] = jnp.exp(s2_ref[h] - st_ref[2, pl.ds(h, 1), :]) * st_ref[3, pl.ds(h, 1), :]

    slot = c % 2
    sc_scr[slot] = lax.dot_general(u_ref[...], xn_ref[...], (((1,), (1,)), ((), ())), preferred_element_type=F32)

    i1_base = pl.multiple_of(jnp.clip(c - 1, 0, nc - 1) * n1, SUBLANE)
    for ii in range(n1):
        for tcol in range(tm // LANE):
            cs = slice(tcol * LANE, (tcol + 1) * LANE)
            g = jnp.zeros((N_KEYS, LANE), F32)
            for h in range(PEER_HEADS):
                s1row = s1_ref[h, pl.ds(i1_base, n1), cs][ii:ii + 1, :]
                theta = st_ref[0, pl.ds(h, 1), cs] - s1row
                arow = jnp.exp(s1row - st_ref[1, pl.ds(h, 1), cs])
                g = g + jnp.where(s2_ref[h, :, cs] >= theta, arow * b_scr[h, :, cs], 0.0)
            x = sc_scr[1 - slot, ii * N_KEYS:(ii + 1) * N_KEYS, cs]
            gelu = 0.5 * x * (1.0 + lax.erf(x * np.float32(np.sqrt(0.5))))
            w_scr[1 - slot, ii * N_KEYS:(ii + 1) * N_KEYS, cs] = (g * gelu).astype(BF16)

    acc_scr[...] += jnp.dot(vt_ref[...], w_scr[slot], preferred_element_type=F32)

    @pl.when(c == pl.num_programs(1) - 1)
    def _():
        o_ref[...] = acc_scr[...].T


def _peer_dense(xn_bf, s1, s2, stats, u_bf, vt_bf, tm=512, ec=1024):
    assert ec // N_KEYS == SUBLANE
    n = xn_bf.shape[0]
    nc = u_bf.shape[0] // ec
    s_spec = pl.BlockSpec((PEER_HEADS, N_KEYS, tm), lambda i, c: (0, 0, i))
    return pl.pallas_call(
        functools.partial(_peer_dense_kernel, ec=ec, tm=tm),
        out_shape=jax.ShapeDtypeStruct((n, D_MODEL), F32),
        grid=(n // tm, nc + 2),
        in_specs=[
            pl.BlockSpec((tm, D_MODEL), lambda i, c: (i, 0)),
            s_spec, s_spec,
            pl.BlockSpec((4, PEER_HEADS, tm), lambda i, c: (0, 0, i)),
            pl.BlockSpec((ec, D_MODEL), lambda i, c: (jnp.minimum(c, nc - 1), 0)),
            pl.BlockSpec((D_MODEL, ec), lambda i, c: (0, jnp.clip(c - 2, 0, nc - 1))),
        ],
        out_specs=pl.BlockSpec((tm, D_MODEL), lambda i, c: (i, 0)),
        scratch_shapes=[pltpu.VMEM((D_MODEL, tm), F32), pltpu.VMEM((PEER_HEADS, N_KEYS, tm), F32),
                        pltpu.VMEM((2, ec, tm), F32), pltpu.VMEM((2, ec, tm), BF16)],
        compiler_params=_cp(("parallel", "arbitrary")),
        name="peer_dense",
    )(xn_bf, s1, s2, stats, u_bf, vt_bf)


def _final_kernel(x1_ref, p_ref, g_ref, y_ref):
    x = x1_ref[...] + p_ref[...]
    ms = jnp.mean(x * x, axis=-1, keepdims=True)
    y_ref[...] = x * lax.rsqrt(ms + NORM_EPS) * g_ref[...]


def _final(x1, p, g, tm=512):
    n = x1.shape[0]
    blk = pl.BlockSpec((tm, D_MODEL), lambda i: (i, 0))
    return pl.pallas_call(
        _final_kernel,
        out_shape=jax.ShapeDtypeStruct((n, D_MODEL), F32),
        grid=(n // tm,),
        in_specs=[blk, blk, pl.BlockSpec((1, D_MODEL), lambda i: (0, 0))],
        out_specs=blk,
        compiler_params=_cp(("parallel",)),
        name="final",
    )(x1, p, g)


def _pad_cols(a, w):
    return jnp.pad(a, ((0, 0), (0, w - a.shape[1])))


def _za_layout(a):
    r3 = a[:, :3 * A_WIDTH]
    wl = _pad_cols(a[:, 3 * A_WIDTH:3 * A_WIDTH + A_DECAY_LORA], COL_AL - COL_WL)
    al = _pad_cols(a[:, 3 * A_WIDTH + A_DECAY_LORA:3 * A_WIDTH + A_DECAY_LORA + A_ICL_LORA], COL_GL - COL_AL)
    gl = _pad_cols(a[:, 3 * A_WIDTH + A_DECAY_LORA + A_ICL_LORA:SHIFT_W], ZA_W - COL_GL)
    return jnp.concatenate([r3, wl, al, gl], axis=1)


def _za_unlayout(a):
    return jnp.concatenate([a[:, :3 * A_WIDTH], a[:, COL_WL:COL_WL + A_DECAY_LORA], a[:, COL_AL:COL_AL + A_ICL_LORA],
                            a[:, COL_GL:COL_GL + A_GATE_LORA]], axis=1)


def _pad_rows(a, n):
    return jnp.pad(a, ((0, n - a.shape[0]), (0, 0)))


def _prepare_weights(lw):
    (norm1, w_in, mu_shift, w0, w_w2, a0, w_a2, w_g2, k_k, k_a, r_k, lnx_w, lnx_b, w_pa, w_pb, w_out, norm2,
     peer_wq, peer_keys, peer_u, peer_v) = lw
    row = lambda a: a[None, :]
    wcat = jnp.concatenate([
        _za_layout(w_in[:, :SHIFT_W]), jnp.zeros((D_MODEL, COL_QB - ZA_W), F32), w_in[:, SHIFT_W:]], axis=1)
    return dict(
        norm1=row(norm1), wcat=wcat.astype(BF16), mu=_za_layout(row(mu_shift)), w0=row(w0),
        ww2=_pad_rows(w_w2, COL_AL - COL_WL).astype(BF16), a0=row(a0),
        wa2=_pad_rows(w_a2, COL_GL - COL_AL).astype(BF16), wg2=_pad_rows(w_g2, ZA_W - COL_GL).astype(BF16),
        k_k=row(k_k), k_a=row(k_a), r_k=row(r_k), lnx_w=row(lnx_w), lnx_b=row(lnx_b),
        wpa=w_pa.astype(BF16), wpb=w_pb.astype(BF16), wout=w_out.astype(BF16), norm2=row(norm2),
        wq=peer_wq.astype(BF16), keys=peer_keys.astype(BF16), u=peer_u.astype(BF16), vt=peer_v.T.astype(BF16))


def _group_layer(x, shift0, wkv0, ret0, pos, pw, scan_tc):
    bsz, t, d = x.shape
    n = bsz * t
    z = _inproj(x.reshape(n, d), pw["norm1"], pw["wcat"])
    shift_new = _za_unlayout(z.reshape(bsz, t, W_TOT)[:, -1, :ZA_W])

    r, w, k2, v, kk, nkka, g = _rwkv_prep(z, _za_layout(shift0), bsz, t, pw)
    s0 = jnp.swapaxes(wkv0, 1, 2).reshape(bsz, A_HEAD, A_WIDTH)
    y, sf = _rwkv_scan([a.reshape(bsz, t, A_WIDTH) for a in (r, w, k2, v, kk, nkka)], s0, nb=4, tc=scan_tc)
    wkv_new = jnp.swapaxes(sf.reshape(bsz, A_HEAD, A_HEADS, A_HEAD), 1, 2)

    chunk = RET_CHUNK if t % RET_CHUNK == 0 else t
    ob, ret_new = _retention(z, ret0, _retention_tables(pos, chunk), bsz, t, chunk)

    m = _merge1(y.reshape(n, A_WIDTH), r, k2, v, g, ob, z, pw)
    x1, xn = _merge2(m, x.reshape(n, d), pw["wout"], pw["norm2"])
    s1, s2, stats = _peer_topk(xn, pw["wq"], pw["keys"])
    peer = _peer_dense(xn, s1, s2, stats, pw["u"], pw["vt"])
    return x1, peer, shift_new, wkv_new, ret_new


def _trunk(x, shift0, wkv0, ret0, pos, pws, norm_f, scan_tc):
    bsz, t, d = x.shape
    shifts, wkvs, rets = [], [], []
    x1 = peer = None
    for layer, pw in enumerate(pws):
        if layer > 0:
            x = (x1 + peer).reshape(bsz, t, d)
        x1, peer, s_new, wkv_new, ret_new = _group_layer(x, shift0[layer], wkv0[layer], ret0[layer], pos, pw, scan_tc)
        shifts.append(s_new)
        wkvs.append(wkv_new)
        rets.append(ret_new)
    y = _final(x1, peer, norm_f[None, :]).reshape(bsz, t, d)
    return y, jnp.stack(shifts), jnp.stack(wkvs), jnp.stack(rets)


def kernel(x_prompt, x_sample, state_shift, state_wkv, state_ret, norm1, w_in, mu_shift, w0, w_w2, a0, w_a2, w_g2, k_k, k_a, r_k, lnx_w, lnx_b, w_pa, w_pb, w_out, norm2, peer_wq, peer_keys, peer_u, peer_v, norm_f):
    weights = (norm1, w_in, mu_shift, w0, w_w2, a0, w_a2, w_g2, k_k, k_a, r_k, lnx_w, lnx_b, w_pa, w_pb, w_out,
               norm2, peer_wq, peer_keys, peer_u, peer_v)
    depth = norm1.shape[0]
    pws = [_prepare_weights([w[layer] for w in weights]) for layer in range(depth)]
    bp, tp, _ = x_prompt.shape
    ts = x_sample.shape[1]
    shift0 = jnp.zeros((depth, bp, SHIFT_W), F32)
    wkv0 = jnp.zeros((depth, bp, A_HEADS, A_HEAD, A_HEAD), F32)
    ret0 = jnp.zeros((depth, bp, B_HEADS, B_QK_HEAD, B_V_HEAD), F32)
    y_p, shift_p, wkv_p, ret_p = _trunk(x_prompt, shift0, wkv0, ret0, jnp.arange(tp, dtype=F32), pws, norm_f,
                                        scan_tc=128)
    y_s, shift_s, wkv_s, ret_s = _trunk(x_sample, state_shift, state_wkv, state_ret,
                                        PAST_LEN + jnp.arange(ts, dtype=F32), pws, norm_f, scan_tc=ts)
    return (y_p, y_s, shift_p, wkv_p, ret_p, shift_s, wkv_s, ret_s)
```

```python
import functools

import jax
import jax.numpy as jnp
import numpy as np
from jax import lax
from jax.experimental import pallas as pl
from jax.experimental.pallas import tpu as pltpu

F32 = jnp.float32
BF16 = jnp.bfloat16

D_MODEL = 2048
A_HEAD = 64
A_WIDTH = 1024
A_HEADS = 16
A_DECAY_LORA = 64
A_ICL_LORA = 64
A_GATE_LORA = 160
A_GN_EPS = 64e-5
B_HEADS = 8
B_QK_HEAD = 128
B_V_HEAD = 256
B_QK_WIDTH = 1024
B_V_WIDTH = 2048
RET_CHUNK = 128
ROT_BASE = 10000.0
N_KEYS = 128
N_EXPERTS = N_KEYS * N_KEYS
PEER_HEADS = 8
PEER_TOPK = 16
NORM_EPS = 1e-6
PAST_LEN = 16384
SHIFT_W = 3 * A_WIDTH + A_DECAY_LORA + A_ICL_LORA + A_GATE_LORA

LANE = 128
SUBLANE = 8
SEG_W = 256
ZA_W = 3584
COL_WL = 3072
COL_AL = 3200
COL_GL = 3328
COL_QB = 4096
COL_KB = 5120
COL_VB = 6144
COL_GB = 8192
COL_GMA = 10240
COL_GMB = 12288
W_TOT = 14336
VMEM_LIMIT = 56 * 1024 * 1024


def _cp(sem):
    return pltpu.CompilerParams(dimension_semantics=sem, vmem_limit_bytes=VMEM_LIMIT)


def _sigmoid(x):
    return 1.0 / (1.0 + jnp.exp(-x))


def _inproj_kernel(x_ref, g_ref, w_ref, o_ref, hn_ref):
    @pl.when(pl.program_id(1) == 0)
    def _():
        x = x_ref[...]
        ms = jnp.mean(x * x, axis=-1, keepdims=True)
        hn_ref[...] = (x * lax.rsqrt(ms + NORM_EPS) * g_ref[...]).astype(BF16)

    o_ref[...] = jnp.dot(hn_ref[...], w_ref[...], preferred_element_type=F32)


def _inproj(x, g, w, tm=1024, tn=512):
    n, d = x.shape
    wt = w.shape[1]
    return pl.pallas_call(
        _inproj_kernel,
        out_shape=jax.ShapeDtypeStruct((n, wt), F32),
        grid=(n // tm, wt // tn),
        in_specs=[
            pl.BlockSpec((tm, d), lambda i, j: (i, 0)),
            pl.BlockSpec((1, d), lambda i, j: (0, 0)),
            pl.BlockSpec((d, tn), lambda i, j: (0, j)),
        ],
        out_specs=pl.BlockSpec((tm, tn), lambda i, j: (i, j)),
        scratch_shapes=[pltpu.VMEM((tm, d), BF16)],
        compiler_params=_cp(("parallel", "arbitrary")),
        name="inproj",
    )(x, g, w)


def _seg_ones():
    r = lax.broadcasted_iota(jnp.int32, (SEG_W, SEG_W), 0) // A_HEAD
    c = lax.broadcasted_iota(jnp.int32, (SEG_W, SEG_W), 1) // A_HEAD
    return (r == c).astype(BF16)


def _segsum64(x, ones):
    def top8(v):
        bits = lax.bitcast_convert_type(v, jnp.uint32) & jnp.uint32(0xFFFF0000)
        return lax.bitcast_convert_type(bits, F32)

    hi = top8(x)
    r1 = x - hi
    mid = top8(r1)
    lo = r1 - mid
    hi, mid, lo = hi.astype(BF16), mid.astype(BF16), lo.astype(BF16)
    parts = []
    for j in range(x.shape[1] // SEG_W):
        sl = slice(j * SEG_W, (j + 1) * SEG_W)
        parts.append(jnp.dot(hi[:, sl], ones, preferred_element_type=F32)
                     + jnp.dot(mid[:, sl], ones, preferred_element_type=F32)
                     + jnp.dot(lo[:, sl], ones, preferred_element_type=F32))
    return jnp.concatenate(parts, axis=1)


def _rwkv_prep_kernel(z_ref, first_ref, mu_ref, w0_ref, ww2_ref, a0_ref, wa2_ref, wg2_ref, kk_ref_, ka_ref,
                      r_o, w_o, k_o, v_o, kk_o, nkka_o, g_o, carry_scr, *, tm, seq_len):
    z = z_ref[...]
    rolled = pltpu.roll(z, 1, 0)
    rows = lax.broadcasted_iota(jnp.int32, z.shape, 0)
    if seq_len >= tm:
        @pl.when(pl.program_id(1) == 0)
        def _():
            carry_scr[0:1, :] = first_ref[0]

        zprev = jnp.where(rows == 0, carry_scr[0:1, :], rolled)
        carry_scr[0:1, :] = z[tm - 1:tm, :]
    else:
        zprev = jnp.where(rows % seq_len == 0, first_ref[...], rolled)
    zs = z + (zprev - z) * mu_ref[...]
    r = zs[:, 0:A_WIDTH]
    k = zs[:, A_WIDTH:2 * A_WIDTH]
    v = zs[:, 2 * A_WIDTH:3 * A_WIDTH]
    wl = zs[:, COL_WL:COL_AL]
    al = zs[:, COL_AL:COL_GL]
    gl = zs[:, COL_GL:ZA_W]
    wpre = w0_ref[...] + jnp.dot(jnp.tanh(wl).astype(BF16), ww2_ref[...], preferred_element_type=F32)
    nw = -wpre
    softplus = jnp.maximum(nw, 0.0) + jnp.log1p(jnp.exp(-jnp.abs(nw)))
    wlog = -softplus - 0.5
    decay = jnp.exp(-jnp.exp(wlog))
    a = _sigmoid(a0_ref[...] + jnp.dot(al.astype(BF16), wa2_ref[...], preferred_element_type=F32))
    g = jnp.dot(_sigmoid(gl).astype(BF16), wg2_ref[...], preferred_element_type=F32)
    kk = k * kk_ref_[...]
    ss = _segsum64(kk * kk, _seg_ones())
    kk = kk / jnp.maximum(jnp.sqrt(ss), 1e-12)
    k2 = k * (1.0 + (a - 1.0) * ka_ref[...])
    r_o[...] = r
    w_o[...] = decay
    k_o[...] = k2
    v_o[...] = v
    kk_o[...] = kk
    nkka_o[...] = -(kk * a)
    g_o[...] = g


def _rwkv_prep(z, shift0_l, bsz, t, pw, tm=256):
    n = bsz * t
    nt = max(t // tm, 1)
    if t >= tm:
        first = shift0_l[:, None, :]
        first_spec = pl.BlockSpec((1, 1, ZA_W), lambda i, j: (i, 0, 0))
    else:
        first = jnp.pad(shift0_l[:, None, :], ((0, 0), (0, t - 1), (0, 0))).reshape(n, ZA_W)
        first_spec = pl.BlockSpec((tm, ZA_W), lambda i, j: (i * nt + j, 0))
    row = lambda w: pl.BlockSpec((1, w), lambda i, j: (0, 0))
    full = lambda a: pl.BlockSpec(a.shape, lambda i, j: (0, 0))
    out = jax.ShapeDtypeStruct((n, A_WIDTH), F32)
    return pl.pallas_call(
        functools.partial(_rwkv_prep_kernel, tm=tm, seq_len=t),
        out_shape=[out] * 7,
        grid=(n // (tm * nt), nt),
        in_specs=[
            pl.BlockSpec((tm, ZA_W), lambda i, j: (i * nt + j, 0)),
            first_spec,
            row(ZA_W), row(A_WIDTH), full(pw["ww2"]), row(A_WIDTH), full(pw["wa2"]), full(pw["wg2"]),
            row(A_WIDTH), row(A_WIDTH),
        ],
        out_specs=[pl.BlockSpec((tm, A_WIDTH), lambda i, j: (i * nt + j, 0))] * 7,
        scratch_shapes=[pltpu.VMEM((SUBLANE, ZA_W), F32)],
        compiler_params=_cp(("parallel", "arbitrary")),
        name="rwkv_prep",
    )(z, first, pw["mu"], pw["w0"], pw["ww2"], pw["a0"], pw["wa2"], pw["wg2"], pw["k_k"], pw["k_a"])


def _rwkv_scan_kernel(r_ref, w_ref, k_ref, v_ref, kk_ref, nkka_ref, s0_ref, m4_ref, y_ref, sf_ref, s_scr,
                      xa_scr, xb_scr, y8_scr, rows_scr, *, nb, tc):
    c = pl.program_id(1)

    @pl.when(c == 0)
    def _():
        s_scr[...] = s0_ref[...]

    lane = lax.broadcasted_iota(jnp.int32, (A_HEAD, LANE), 1)
    sub = lax.broadcasted_iota(jnp.int32, (A_HEAD, LANE), 0)
    lo = lane < A_HEAD
    diag = (lane & (A_HEAD - 1)) == sub
    diag_lo = jnp.logical_and(diag, lo)
    diag_hi = jnp.logical_and(diag, jnp.logical_not(lo))
    nq = A_WIDTH // SEG_W
    npair = A_WIDTH // LANE

    seq_refs = (r_ref, w_ref, k_ref, v_ref, kk_ref, nkka_ref)
    seq_ids = [id(ref) for ref in seq_refs]

    def step(t8, carry):
        base = pl.multiple_of(t8 * SUBLANE, SUBLANE)
        for a, ref in enumerate(seq_refs):
            for b in range(nb):
                rows_scr[a, b] = ref[b, pl.ds(base, SUBLANE), :]
        for i in range(SUBLANE):
            row = lambda ref, b, lanes: rows_scr[seq_ids.index(id(ref)), b, i:i + 1, lanes]
            for b in range(nb):
                for j in range(nq):
                    q = slice(j * SEG_W, (j + 1) * SEG_W)
                    r0 = (b * nq + j) * A_HEAD
                    xa_scr[r0:r0 + A_HEAD, :] = (s_scr[b, :, q] * row(kk_ref, b, q)).astype(BF16)
            sa_all = jnp.dot(xa_scr[...], m4_ref[...], preferred_element_type=F32)
            for b in range(nb):
                for j in range(npair):
                    sl = slice(j * LANE, (j + 1) * LANE)
                    r0 = (b * nq + j // 2) * A_HEAD
                    half = slice((j % 2) * LANE, (j % 2 + 1) * LANE)
                    vrow = row(v_ref, b, sl)
                    v_lo = jnp.sum(jnp.where(diag_lo, vrow, 0.0), axis=1, keepdims=True)
                    v_hi = jnp.sum(jnp.where(diag_hi, vrow, 0.0), axis=1, keepdims=True)
                    vb = jnp.where(lo, v_lo, v_hi)
                    s = (s_scr[b, :, sl] * row(w_ref, b, sl) + sa_all[r0:r0 + A_HEAD, half] * row(nkka_ref, b, sl)
                         + vb * row(k_ref, b, sl))
                    s_scr[b, :, sl] = s
                    xb_scr[r0:r0 + A_HEAD, half] = (s * row(r_ref, b, sl)).astype(BF16)
            y_all = jnp.dot(xb_scr[...], m4_ref[...], preferred_element_type=F32)
            for b in range(nb):
                for j in range(npair):
                    sl = slice(j * LANE, (j + 1) * LANE)
                    r0 = (b * nq + j // 2) * A_HEAD
                    half = slice((j % 2) * LANE, (j % 2 + 1) * LANE)
                    yb = y_all[r0:r0 + A_HEAD, half]
                    y8_scr[b, i:i + 1, sl] = jnp.sum(jnp.where(diag, yb, 0.0), axis=0, keepdims=True)
        for b in range(nb):
            y_ref[b, pl.ds(base, SUBLANE), :] = y8_scr[b]
        return carry

    lax.fori_loop(0, tc // SUBLANE, step, 0)

    @pl.when(c == pl.num_programs(1) - 1)
    def _():
        sf_ref[...] = s_scr[...]


def _rwkv_scan(seqs, s0, nb, tc):
    bsz, t, _ = seqs[0].shape
    seq = pl.BlockSpec((nb, tc, A_WIDTH), lambda i, c: (i, c, 0))
    st = pl.BlockSpec((nb, A_HEAD, A_WIDTH), lambda i, c: (i, 0, 0))
    seg = jnp.arange(SEG_W, dtype=jnp.int32) // A_HEAD
    m4 = (seg[:, None] == seg[None, :]).astype(BF16)
    xrows = nb * (A_WIDTH // SEG_W) * A_HEAD
    x_scr = pltpu.VMEM((xrows, SEG_W), BF16)
    return pl.pallas_call(
        functools.partial(_rwkv_scan_kernel, nb=nb, tc=tc),
        out_shape=[jax.ShapeDtypeStruct((bsz, t, A_WIDTH), F32), jax.ShapeDtypeStruct((bsz, A_HEAD, A_WIDTH), F32)],
        grid=(bsz // nb, t // tc),
        in_specs=[seq] * 6 + [st, pl.BlockSpec((SEG_W, SEG_W), lambda i, c: (0, 0))],
        out_specs=[seq, st],
        scratch_shapes=[pltpu.VMEM((nb, A_HEAD, A_WIDTH), F32), x_scr, x_scr,
                        pltpu.VMEM((nb, SUBLANE, A_WIDTH), F32), pltpu.VMEM((6, nb, SUBLANE, A_WIDTH), F32)],
        compiler_params=_cp(("parallel", "arbitrary")),
        name="rwkv_scan",
    )(*seqs, s0, m4)


def _retention_kernel(q_ref, k_ref, v_ref, g_ref, cos_ref, sin_ref, dm_ref, qd_ref, kd_ref, gc_ref, r0_ref,
                      o_ref, rn_ref, r_scr):
    c = pl.program_id(1)

    @pl.when(c == 0)
    def _():
        r_scr[...] = r0_ref[0]

    cos = cos_ref[...]
    sin = sin_ref[...]

    def rot(x):
        return x * cos + pltpu.roll(x, B_QK_HEAD // 2, 1) * sin

    for h in range(B_HEADS):
        qs = slice(h * B_QK_HEAD, (h + 1) * B_QK_HEAD)
        vs = slice(h * B_V_HEAD, (h + 1) * B_V_HEAD)
        q = rot(q_ref[:, qs])
        k = rot(k_ref[:, qs]) * (B_QK_HEAD ** -0.5)
        v = v_ref[:, vs].astype(BF16)
        rs = r_scr[h]
        qb = q.astype(BF16)
        s = lax.dot_general(qb, k.astype(BF16), (((1,), (1,)), ((), ())), preferred_element_type=F32) * dm_ref[h]
        inner = jnp.dot(s.astype(BF16), v, preferred_element_type=F32)
        cross = jnp.dot(qb, rs.astype(BF16), preferred_element_type=F32) * qd_ref[h]
        kd = (k * kd_ref[h]).astype(BF16)
        r_scr[h] = gc_ref[h] * rs + lax.dot_general(kd, v, (((0,), (0,)), ((), ())), preferred_element_type=F32)
        o = inner + cross
        o = o * lax.rsqrt(jnp.mean(o * o, axis=-1, keepdims=True) + NORM_EPS)
        gate = g_ref[:, vs]
        o_ref[:, vs] = gate * _sigmoid(gate) * o

    @pl.when(c == pl.num_programs(1) - 1)
    def _():
        rn_ref[0] = r_scr[...]


def _retention(z, r0, tabs, bsz, t, chunk):
    cos, sin, dm, qd, kd, gc = tabs
    nc = t // chunk
    seq = lambda w, col: pl.BlockSpec((chunk, w), lambda b, c: (b * nc + c, col // w))
    tab3 = lambda a: pl.BlockSpec(a.shape, lambda b, c: (0, 0, 0))
    st = pl.BlockSpec((1, B_HEADS, B_QK_HEAD, B_V_HEAD), lambda b, c: (b, 0, 0, 0))
    return pl.pallas_call(
        _retention_kernel,
        out_shape=[jax.ShapeDtypeStruct((bsz * t, B_V_WIDTH), F32),
                   jax.ShapeDtypeStruct((bsz, B_HEADS, B_QK_HEAD, B_V_HEAD), F32)],
        grid=(bsz, nc),
        in_specs=[
            seq(B_QK_WIDTH, COL_QB), seq(B_QK_WIDTH, COL_KB), seq(B_V_WIDTH, COL_VB), seq(B_V_WIDTH, COL_GB),
            pl.BlockSpec((chunk, B_QK_HEAD), lambda b, c: (c, 0)),
            pl.BlockSpec((chunk, B_QK_HEAD), lambda b, c: (c, 0)),
            tab3(dm), tab3(qd), tab3(kd), tab3(gc), st,
        ],
        out_specs=[pl.BlockSpec((chunk, B_V_WIDTH), lambda b, c: (b * nc + c, 0)), st],
        scratch_shapes=[pltpu.VMEM((B_HEADS, B_QK_HEAD, B_V_HEAD), F32)],
        compiler_params=_cp(("parallel", "arbitrary")),
        name="retention",
    )(z, z, z, z, cos, sin, dm, qd, kd, gc, r0)


def _retention_tables(pos, chunk):
    half = B_QK_HEAD // 2
    inv = jnp.power(ROT_BASE, -jnp.linspace(0.0, 1.0, half, dtype=F32))
    ang = pos[:, None] * inv[None, :]
    cos = jnp.cos(ang)
    sin = jnp.sin(ang)
    cos2 = jnp.concatenate([cos, cos], axis=-1)
    sin2 = jnp.concatenate([-sin, sin], axis=-1)
    log_g = jnp.log(1.0 - jnp.power(2.0, -5.0 - jnp.arange(B_HEADS, dtype=F32)))
    idx = jnp.arange(chunk, dtype=F32)
    diff = idx[:, None] - idx[None, :]
    dmask = jnp.where(diff >= 0, jnp.exp(log_g[:, None, None] * jnp.maximum(diff, 0.0)), 0.0)
    q_dec = jnp.exp(log_g[:, None] * (idx[None, :] + 1.0))
    k_dec = jnp.exp(log_g[:, None] * (chunk - 1.0 - idx[None, :]))
    g_c = jnp.exp(log_g * chunk)
    qd = jnp.broadcast_to(q_dec[:, :, None], (B_HEADS, chunk, B_V_HEAD))
    kd = jnp.broadcast_to(k_dec[:, :, None], (B_HEADS, chunk, B_QK_HEAD))
    gc = jnp.broadcast_to(g_c[:, None, None], (B_HEADS, B_QK_HEAD, B_V_HEAD))
    return cos2, sin2, dmask, qd, kd, gc


def _merge1_kernel(y_ref, r_ref, k_ref, v_ref, g_ref, ob_ref, gma_ref, gmb_ref, lw_ref, lb_ref, rk_ref, wpa_ref,
                   wpb_ref, m_ref, oa_scr, ob_scr):
    @pl.when(pl.program_id(1) == 0)
    def _():
        ones = _seg_ones()
        y = y_ref[...]
        mean = _segsum64(y, ones) * (1.0 / A_HEAD)
        yc = y - mean
        var = _segsum64(yc * yc, ones) * (1.0 / A_HEAD)
        yn = yc * lax.rsqrt(var + A_GN_EPS) * lw_ref[...] + lb_ref[...]
        bonus = _segsum64(r_ref[...] * k_ref[...] * rk_ref[...], ones) * v_ref[...]
        oa_scr[...] = ((yn + bonus) * g_ref[...]).astype(BF16)
        ob_scr[...] = ob_ref[...].astype(BF16)

    pa = jnp.dot(oa_scr[...], wpa_ref[...], preferred_element_type=F32)
    pb = jnp.dot(ob_scr[...], wpb_ref[...], preferred_element_type=F32)
    m_ref[...] = (_sigmoid(gma_ref[...]) * pa + _sigmoid(gmb_ref[...]) * pb).astype(m_ref.dtype)


def _merge1(y, r, k2, v, g, ob, z, pw, tm=256, tn=512):
    n = y.shape[0]
    a_blk = pl.BlockSpec((tm, A_WIDTH), lambda i, j: (i, 0))
    row = pl.BlockSpec((1, A_WIDTH), lambda i, j: (0, 0))
    return pl.pallas_call(
        _merge1_kernel,
        out_shape=jax.ShapeDtypeStruct((n, D_MODEL), BF16),
        grid=(n // tm, D_MODEL // tn),
        in_specs=[
            a_blk, a_blk, a_blk, a_blk, a_blk,
            pl.BlockSpec((tm, B_V_WIDTH), lambda i, j: (i, 0)),
            pl.BlockSpec((tm, tn), lambda i, j: (i, COL_GMA // tn + j)),
            pl.BlockSpec((tm, tn), lambda i, j: (i, COL_GMB // tn + j)),
            row, row, row,
            pl.BlockSpec((A_WIDTH, tn), lambda i, j: (0, j)),
            pl.BlockSpec((B_V_WIDTH, tn), lambda i, j: (0, j)),
        ],
        out_specs=pl.BlockSpec((tm, tn), lambda i, j: (i, j)),
        scratch_shapes=[pltpu.VMEM((tm, A_WIDTH), BF16), pltpu.VMEM((tm, B_V_WIDTH), BF16)],
        compiler_params=_cp(("parallel", "arbitrary")),
        name="merge1",
    )(y, r, k2, v, g, ob, z, z, pw["lnx_w"], pw["lnx_b"], pw["r_k"], pw["wpa"], pw["wpb"])


def _merge2_kernel(m_ref, x_ref, w_ref, g_ref, x1_ref, xn_ref):
    x1 = x_ref[...] + jnp.dot(m_ref[...], w_ref[...], preferred_element_type=F32)
    x1_ref[...] = x1
    ms = jnp.mean(x1 * x1, axis=-1, keepdims=True)
    xn_ref[...] = (x1 * lax.rsqrt(ms + NORM_EPS) * g_ref[...]).astype(BF16)


def _merge2(m, x, w_out, norm2, tm=256):
    n = x.shape[0]
    blk = pl.BlockSpec((tm, D_MODEL), lambda i: (i, 0))
    return pl.pallas_call(
        _merge2_kernel,
        out_shape=[jax.ShapeDtypeStruct((n, D_MODEL), F32), jax.ShapeDtypeStruct((n, D_MODEL), BF16)],
        grid=(n // tm,),
        in_specs=[blk, blk, pl.BlockSpec((D_MODEL, D_MODEL), lambda i: (0, 0)),
                  pl.BlockSpec((1, D_MODEL), lambda i: (0, 0))],
        out_specs=[blk, blk],
        compiler_params=_cp(("parallel",)),
        name="merge2",
    )(m, x, w_out, norm2)


_PAIRS = [(i, j) for i in range(PEER_TOPK) for j in range(PEER_TOPK) if (i + 1) * (j + 1) <= PEER_TOPK]


def _top16_rows(s):
    rows = []
    cur = s
    for i in range(PEER_TOPK):
        m = jnp.max(cur, axis=0, keepdims=True)
        rows.append(m)
        if i + 1 < PEER_TOPK:
            cur = jnp.where(cur == m, -jnp.inf, cur)
    return rows


def _peer_topk_kernel(xn_ref, wq_ref, keys_ref, s1_ref, s2_ref, st_ref, top_scr, cand_scr):
    q = jnp.dot(xn_ref[...], wq_ref[...], preferred_element_type=F32)
    for h in range(PEER_HEADS):
        for p in range(2):
            c0 = (h * 2 + p) * N_KEYS
            qhp = q[:, c0:c0 + N_KEYS].astype(BF16)
            st = lax.dot_general(keys_ref[h, p], qhp, (((1,), (1,)), ((), ())), preferred_element_type=F32)
            (s1_ref if p == 0 else s2_ref)[h] = st
            for i, row in enumerate(_top16_rows(st)):
                top_scr[p, i, h:h + 1, :] = row
    for n, (i, j) in enumerate(_PAIRS):
        cand_scr[n] = top_scr[0, i] + top_scr[1, j]
    m = None
    for it in range(PEER_TOPK):
        m = cand_scr[0]
        for n in range(1, len(_PAIRS)):
            m = jnp.maximum(m, cand_scr[n])
        if it + 1 < PEER_TOPK:
            for n in range(len(_PAIRS)):
                cnd = cand_scr[n]
                cand_scr[n] = jnp.where(cnd == m, -jnp.inf, cnd)
    tau = m
    top = top_scr[0, 0] + top_scr[1, 0]
    zsum = jnp.zeros_like(tau)
    for (i, j) in _PAIRS:
        cnd = top_scr[0, i] + top_scr[1, j]
        zsum = zsum + jnp.where(cnd >= tau, jnp.exp(cnd - top), 0.0)
    st_ref[0] = tau
    st_ref[1] = top_scr[0, 0]
    st_ref[2] = top_scr[1, 0]
    st_ref[3] = 1.0 / zsum


def _peer_topk(xn, wq, keys, tm=256):
    n = xn.shape[0]
    s_shape = jax.ShapeDtypeStruct((PEER_HEADS, N_KEYS, n), F32)
    s_spec = pl.BlockSpec((PEER_HEADS, N_KEYS, tm), lambda i: (0, 0, i))
    return pl.pallas_call(
        _peer_topk_kernel,
        out_shape=[s_shape, s_shape, jax.ShapeDtypeStruct((4, PEER_HEADS, n), F32)],
        grid=(n // tm,),
        in_specs=[pl.BlockSpec((tm, D_MODEL), lambda i: (i, 0)),
                  pl.BlockSpec((D_MODEL, D_MODEL), lambda i: (0, 0)),
                  pl.BlockSpec(keys.shape, lambda i: (0, 0, 0, 0))],
        out_specs=[s_spec, s_spec, pl.BlockSpec((4, PEER_HEADS, tm), lambda i: (0, 0, i))],
        scratch_shapes=[pltpu.VMEM((2, PEER_TOPK, PEER_HEADS, tm), F32),
                        pltpu.VMEM((len(_PAIRS), PEER_HEADS, tm), F32)],
        compiler_params=_cp(("parallel",)),
        name="peer_topk",
    )(xn, wq, keys)


def _peer_dense_kernel(xn_ref, s1_ref, s2_ref, st_ref, u_ref, vt_ref, o_ref, acc_scr, b_scr, sc_scr, w_scr, *, ec, tm):
    c = pl.program_id(1)
    nc = pl.num_programs(1) - 2
    n1 = ec // N_KEYS

    @pl.when(c == 0)
    def _():
        acc_scr[...] = jnp.zeros_like(acc_scr)
        sc_scr[...] = jnp.zeros_like(sc_scr)
        w_scr[...] = jnp.zeros_like(w_scr)
        for h in range(PEER_HEADS):
            b_scr[h] = jnp.exp(s2_ref[h] - st_ref[2, pl.ds(h, 1), :]) * st_ref[3, pl.ds(h, 1), :]

    slot = c % 2
    sc_scr[slot] = lax.dot_general(u_ref[...], xn_ref[...], (((1,), (1,)), ((), ())), preferred_element_type=F32)

    i1_base = pl.multiple_of(jnp.clip(c - 1, 0, nc - 1) * n1, SUBLANE)
    for ii in range(n1):
        for tcol in range(tm // LANE):
            cs = slice(tcol * LANE, (tcol + 1) * LANE)
            g = jnp.zeros((N_KEYS, LANE), F32)
            for h in range(PEER_HEADS):
                s1row = s1_ref[h, pl.ds(i1_base, n1), cs][ii:ii + 1, :]
                theta = st_ref[0, pl.ds(h, 1), cs] - s1row
                arow = jnp.exp(s1row - st_ref[1, pl.ds(h, 1), cs])
                g = g + jnp.where(s2_ref[h, :, cs] >= theta, arow * b_scr[h, :, cs], 0.0)
            x = sc_scr[1 - slot, ii * N_KEYS:(ii + 1) * N_KEYS, cs]
            gelu = 0.5 * x * (1.0 + lax.erf(x * np.float32(np.sqrt(0.5))))
            w_scr[1 - slot, ii * N_KEYS:(ii + 1) * N_KEYS, cs] = (g * gelu).astype(BF16)

    acc_scr[...] += jnp.dot(vt_ref[...], w_scr[slot], preferred_element_type=F32)

    @pl.when(c == pl.num_programs(1) - 1)
    def _():
        o_ref[...] = acc_scr[...].T


def _peer_dense(xn_bf, s1, s2, stats, u_bf, vt_bf, tm=512, ec=1024):
    assert ec // N_KEYS == SUBLANE
    n = xn_bf.shape[0]
    nc = u_bf.shape[0] // ec
    s_spec = pl.BlockSpec((PEER_HEADS, N_KEYS, tm), lambda i, c: (0, 0, i))
    return pl.pallas_call(
        functools.partial(_peer_dense_kernel, ec=ec, tm=tm),
        out_shape=jax.ShapeDtypeStruct((n, D_MODEL), F32),
        grid=(n // tm, nc + 2),
        in_specs=[
            pl.BlockSpec((tm, D_MODEL), lambda i, c: (i, 0)),
            s_spec, s_spec,
            pl.BlockSpec((4, PEER_HEADS, tm), lambda i, c: (0, 0, i)),
            pl.BlockSpec((ec, D_MODEL), lambda i, c: (jnp.minimum(c, nc - 1), 0)),
            pl.BlockSpec((D_MODEL, ec), lambda i, c: (0, jnp.clip(c - 2, 0, nc - 1))),
        ],
        out_specs=pl.BlockSpec((tm, D_MODEL), lambda i, c: (i, 0)),
        scratch_shapes=[pltpu.VMEM((D_MODEL, tm), F32), pltpu.VMEM((PEER_HEADS, N_KEYS, tm), F32),
                        pltpu.VMEM((2, ec, tm), F32), pltpu.VMEM((2, ec, tm), BF16)],
        compiler_params=_cp(("parallel", "arbitrary")),
        name="peer_dense",
    )(xn_bf, s1, s2, stats, u_bf, vt_bf)


def _final_kernel(x1_ref, p_ref, g_ref, y_ref):
    x = x1_ref[...] + p_ref[...]
    ms = jnp.mean(x * x, axis=-1, keepdims=True)
    y_ref[...] = x * lax.rsqrt(ms + NORM_EPS) * g_ref[...]


def _final(x1, p, g, tm=512):
    n = x1.shape[0]
    blk = pl.BlockSpec((tm, D_MODEL), lambda i: (i, 0))
    return pl.pallas_call(
        _final_kernel,
        out_shape=jax.ShapeDtypeStruct((n, D_MODEL), F32),
        grid=(n // tm,),
        in_specs=[blk, blk, pl.BlockSpec((1, D_MODEL), lambda i: (0, 0))],
        out_specs=blk,
        compiler_params=_cp(("parallel",)),
        name="final",
    )(x1, p, g)


def _pad_cols(a, w):
    return jnp.pad(a, ((0, 0), (0, w - a.shape[1])))


def _za_layout(a):
    r3 = a[:, :3 * A_WIDTH]
    wl = _pad_cols(a[:, 3 * A_WIDTH:3 * A_WIDTH + A_DECAY_LORA], COL_AL - COL_WL)
    al = _pad_cols(a[:, 3 * A_WIDTH + A_DECAY_LORA:3 * A_WIDTH + A_DECAY_LORA + A_ICL_LORA], COL_GL - COL_AL)
    gl = _pad_cols(a[:, 3 * A_WIDTH + A_DECAY_LORA + A_ICL_LORA:SHIFT_W], ZA_W - COL_GL)
    return jnp.concatenate([r3, wl, al, gl], axis=1)


def _za_unlayout(a):
    return jnp.concatenate([a[:, :3 * A_WIDTH], a[:, COL_WL:COL_WL + A_DECAY_LORA], a[:, COL_AL:COL_AL + A_ICL_LORA],
                            a[:, COL_GL:COL_GL + A_GATE_LORA]], axis=1)


def _pad_rows(a, n):
    return jnp.pad(a, ((0, n - a.shape[0]), (0, 0)))


def _prepare_weights(lw):
    (norm1, w_in, mu_shift, w0, w_w2, a0, w_a2, w_g2, k_k, k_a, r_k, lnx_w, lnx_b, w_pa, w_pb, w_out, norm2,
     peer_wq, peer_keys, peer_u, peer_v) = lw
    row = lambda a: a[None, :]
    wcat = jnp.concatenate([
        _za_layout(w_in[:, :SHIFT_W]), jnp.zeros((D_MODEL, COL_QB - ZA_W), F32), w_in[:, SHIFT_W:]], axis=1)
    return dict(
        norm1=row(norm1), wcat=wcat.astype(BF16), mu=_za_layout(row(mu_shift)), w0=row(w0),
        ww2=_pad_rows(w_w2, COL_AL - COL_WL).astype(BF16), a0=row(a0),
        wa2=_pad_rows(w_a2, COL_GL - COL_AL).astype(BF16), wg2=_pad_rows(w_g2, ZA_W - COL_GL).astype(BF16),
        k_k=row(k_k), k_a=row(k_a), r_k=row(r_k), lnx_w=row(lnx_w), lnx_b=row(lnx_b),
        wpa=w_pa.astype(BF16), wpb=w_pb.astype(BF16), wout=w_out.astype(BF16), norm2=row(norm2),
        wq=peer_wq.astype(BF16), keys=peer_keys.astype(BF16), u=peer_u.astype(BF16), vt=peer_v.T.astype(BF16))


def _group_layer(x, shift0, wkv0, ret0, pos, pw, scan_tc):
    bsz, t, d = x.shape
    n = bsz * t
    z = _inproj(x.reshape(n, d), pw["norm1"], pw["wcat"])
    shift_new = _za_unlayout(z.reshape(bsz, t, W_TOT)[:, -1, :ZA_W])

    r, w, k2, v, kk, nkka, g = _rwkv_prep(z, _za_layout(shift0), bsz, t, pw)
    s0 = jnp.swapaxes(wkv0, 1, 2).reshape(bsz, A_HEAD, A_WIDTH)
    y, sf = _rwkv_scan([a.reshape(bsz, t, A_WIDTH) for a in (r, w, k2, v, kk, nkka)], s0, nb=4, tc=scan_tc)
    wkv_new = jnp.swapaxes(sf.reshape(bsz, A_HEAD, A_HEADS, A_HEAD), 1, 2)

    chunk = RET_CHUNK if t % RET_CHUNK == 0 else t
    ob, ret_new = _retention(z, ret0, _retention_tables(pos, chunk), bsz, t, chunk)

    m = _merge1(y.reshape(n, A_WIDTH), r, k2, v, g, ob, z, pw)
    x1, xn = _merge2(m, x.reshape(n, d), pw["wout"], pw["norm2"])
    s1, s2, stats = _peer_topk(xn, pw["wq"], pw["keys"])
    peer = _peer_dense(xn, s1, s2, stats, pw["u"], pw["vt"])
    return x1, peer, shift_new, wkv_new, ret_new


def _trunk(x, shift0, wkv0, ret0, pos, pws, norm_f, scan_tc):
    bsz, t, d = x.shape
    shifts, wkvs, rets = [], [], []
    x1 = peer = None
    for layer, pw in enumerate(pws):
        if layer > 0:
            x = (x1 + peer).reshape(bsz, t, d)
        x1, peer, s_new, wkv_new, ret_new = _group_layer(x, shift0[layer], wkv0[layer], ret0[layer], pos, pw, scan_tc)
        shifts.append(s_new)
        wkvs.append(wkv_new)
        rets.append(ret_new)
    y = _final(x1, peer, norm_f[None, :]).reshape(bsz, t, d)
    return y, jnp.stack(shifts), jnp.stack(wkvs), jnp.stack(rets)


def kernel(x_prompt, x_sample, state_shift, state_wkv, state_ret, norm1, w_in, mu_shift, w0, w_w2, a0, w_a2, w_g2, k_k, k_a, r_k, lnx_w, lnx_b, w_pa, w_pb, w_out, norm2, peer_wq, peer_keys, peer_u, peer_v, norm_f):
    weights = (norm1, w_in, mu_shift, w0, w_w2, a0, w_a2, w_g2, k_k, k_a, r_k, lnx_w, lnx_b, w_pa, w_pb, w_out,
               norm2, peer_wq, peer_keys, peer_u, peer_v)
    depth = norm1.shape[0]
    pws = [_prepare_weights([w[layer] for w in weights]) for layer in range(depth)]
    bp, tp, _ = x_prompt.shape
    ts = x_sample.shape[1]
    shift0 = jnp.zeros((depth, bp, SHIFT_W), F32)
    wkv0 = jnp.zeros((depth, bp, A_HEADS, A_HEAD, A_HEAD), F32)
    ret0 = jnp.zeros((depth, bp, B_HEADS, B_QK_HEAD, B_V_HEAD), F32)
    y_p, shift_p, wkv_p, ret_p = _trunk(x_prompt, shift0, wkv0, ret0, jnp.arange(tp, dtype=F32), pws, norm_f,
                                        scan_tc=128)
    y_s, shift_s, wkv_s, ret_s = _trunk(x_sample, state_shift, state_wkv, state_ret,
                                        PAST_LEN + jnp.arange(ts, dtype=F32), pws, norm_f, scan_tc=ts)
    return (y_p, y_s, shift_p, wkv_p, ret_p, shift_s, wkv_s, ret_s)
```

```python
import functools

import jax
import jax.numpy as jnp
import numpy as np
from jax import lax
from jax.experimental import pallas as pl
from jax.experimental.pallas import tpu as pltpu

F32 = jnp.float32
BF16 = jnp.bfloat16

D_MODEL = 2048
A_HEAD = 64
A_WIDTH = 1024
A_HEADS = 16
A_DECAY_LORA = 64
A_ICL_LORA = 64
A_GATE_LORA = 160
A_GN_EPS = 64e-5
B_HEADS = 8
B_QK_HEAD = 128
B_V_HEAD = 256
B_QK_WIDTH = 1024
B_V_WIDTH = 2048
RET_CHUNK = 128
ROT_BASE = 10000.0
N_KEYS = 128
N_EXPERTS = N_KEYS * N_KEYS
PEER_HEADS = 8
PEER_TOPK = 16
NORM_EPS = 1e-6
PAST_LEN = 16384
SHIFT_W = 3 * A_WIDTH + A_DECAY_LORA + A_ICL_LORA + A_GATE_LORA

LANE = 128
SUBLANE = 8
SEG_W = 256
ZA_W = 3584
COL_WL = 3072
COL_AL = 3200
COL_GL = 3328
COL_QB = 4096
COL_KB = 5120
COL_VB = 6144
COL_GB = 8192
COL_GMA = 10240
COL_GMB = 12288
W_TOT = 14336
VMEM_LIMIT = 56 * 1024 * 1024


def _cp(sem):
    return pltpu.CompilerParams(dimension_semantics=sem, vmem_limit_bytes=VMEM_LIMIT)


def _sigmoid(x):
    return 1.0 / (1.0 + jnp.exp(-x))


def _inproj_kernel(x_ref, g_ref, w_ref, o_ref, hn_ref):
    @pl.when(pl.program_id(1) == 0)
    def _():
        x = x_ref[...]
        ms = jnp.mean(x * x, axis=-1, keepdims=True)
        hn_ref[...] = (x * lax.rsqrt(ms + NORM_EPS) * g_ref[...]).astype(BF16)

    o_ref[...] = jnp.dot(hn_ref[...], w_ref[...], preferred_element_type=F32)


def _inproj(x, g, w, tm=1024, tn=512):
    n, d = x.shape
    wt = w.shape[1]
    return pl.pallas_call(
        _inproj_kernel,
        out_shape=jax.ShapeDtypeStruct((n, wt), F32),
        grid=(n // tm, wt // tn),
        in_specs=[
            pl.BlockSpec((tm, d), lambda i, j: (i, 0)),
            pl.BlockSpec((1, d), lambda i, j: (0, 0)),
            pl.BlockSpec((d, tn), lambda i, j: (0, j)),
        ],
        out_specs=pl.BlockSpec((tm, tn), lambda i, j: (i, j)),
        scratch_shapes=[pltpu.VMEM((tm, d), BF16)],
        compiler_params=_cp(("parallel", "arbitrary")),
        name="inproj",
    )(x, g, w)


def _seg_ones():
    r = lax.broadcasted_iota(jnp.int32, (SEG_W, SEG_W), 0) // A_HEAD
    c = lax.broadcasted_iota(jnp.int32, (SEG_W, SEG_W), 1) // A_HEAD
    return (r == c).astype(BF16)


def _segsum64(x, ones):
    def top8(v):
        bits = lax.bitcast_convert_type(v, jnp.uint32) & jnp.uint32(0xFFFF0000)
        return lax.bitcast_convert_type(bits, F32)

    hi = top8(x)
    r1 = x - hi
    mid = top8(r1)
    lo = r1 - mid
    hi, mid, lo = hi.astype(BF16), mid.astype(BF16), lo.astype(BF16)
    parts = []
    for j in range(x.shape[1] // SEG_W):
        sl = slice(j * SEG_W, (j + 1) * SEG_W)
        parts.append(jnp.dot(hi[:, sl], ones, preferred_element_type=F32)
                     + jnp.dot(mid[:, sl], ones, preferred_element_type=F32)
                     + jnp.dot(lo[:, sl], ones, preferred_element_type=F32))
    return jnp.concatenate(parts, axis=1)


def _rwkv_prep_kernel(z_ref, first_ref, mu_ref, w0_ref, ww2_ref, a0_ref, wa2_ref, wg2_ref, kk_ref_, ka_ref,
                      r_o, w_o, k_o, v_o, kk_o, nkka_o, g_o, carry_scr, *, tm, seq_len):
    z = z_ref[...]
    rolled = pltpu.roll(z, 1, 0)
    rows = lax.broadcasted_iota(jnp.int32, z.shape, 0)
    if seq_len >= tm:
        @pl.when(pl.program_id(1) == 0)
        def _():
            carry_scr[0:1, :] = first_ref[0]

        zprev = jnp.where(rows == 0, carry_scr[0:1, :], rolled)
        carry_scr[0:1, :] = z[tm - 1:tm, :]
    else:
        zprev = jnp.where(rows % seq_len == 0, first_ref[...], rolled)
    zs = z + (zprev - z) * mu_ref[...]
    r = zs[:, 0:A_WIDTH]
    k = zs[:, A_WIDTH:2 * A_WIDTH]
    v = zs[:, 2 * A_WIDTH:3 * A_WIDTH]
    wl = zs[:, COL_WL:COL_AL]
    al = zs[:, COL_AL:COL_GL]
    gl = zs[:, COL_GL:ZA_W]
    wpre = w0_ref[...] + jnp.dot(jnp.tanh(wl).astype(BF16), ww2_ref[...], preferred_element_type=F32)
    nw = -wpre
    softplus = jnp.maximum(nw, 0.0) + jnp.log1p(jnp.exp(-jnp.abs(nw)))
    wlog = -softplus - 0.5
    decay = jnp.exp(-jnp.exp(wlog))
    a = _sigmoid(a0_ref[...] + jnp.dot(al.astype(BF16), wa2_ref[...], preferred_element_type=F32))
    g = jnp.dot(_sigmoid(gl).astype(BF16), wg2_ref[...], preferred_element_type=F32)
    kk = k * kk_ref_[...]
    ss = _segsum64(kk * kk, _seg_ones())
    kk = kk / jnp.maximum(jnp.sqrt(ss), 1e-12)
    k2 = k * (1.0 + (a - 1.0) * ka_ref[...])
    r_o[...] = r
    w_o[...] = decay
    k_o[...] = k2
    v_o[...] = v
    kk_o[...] = kk
    nkka_o[...] = -(kk * a)
    g_o[...] = g


def _rwkv_prep(z, shift0_l, bsz, t, pw, tm=256):
    n = bsz * t
    nt = max(t // tm, 1)
    if t >= tm:
        first = shift0_l[:, None, :]
        first_spec = pl.BlockSpec((1, 1, ZA_W), lambda i, j: (i, 0, 0))
    else:
        first = jnp.pad(shift0_l[:, None, :], ((0, 0), (0, t - 1), (0, 0))).reshape(n, ZA_W)
        first_spec = pl.BlockSpec((tm, ZA_W), lambda i, j: (i * nt + j, 0))
    row = lambda w: pl.BlockSpec((1, w), lambda i, j: (0, 0))
    full = lambda a: pl.BlockSpec(a.shape, lambda i, j: (0, 0))
    out = jax.ShapeDtypeStruct((n, A_WIDTH), F32)
    return pl.pallas_call(
        functools.partial(_rwkv_prep_kernel, tm=tm, seq_len=t),
        out_shape=[out] * 7,
        grid=(n // (tm * nt), nt),
        in_specs=[
            pl.BlockSpec((tm, ZA_W), lambda i, j: (i * nt + j, 0)),
            first_spec,
            row(ZA_W), row(A_WIDTH), full(pw["ww2"]), row(A_WIDTH), full(pw["wa2"]), full(pw["wg2"]),
            row(A_WIDTH), row(A_WIDTH),
        ],
        out_specs=[pl.BlockSpec((tm, A_WIDTH), lambda i, j: (i * nt + j, 0))] * 7,
        scratch_shapes=[pltpu.VMEM((SUBLANE, ZA_W), F32)],
        compiler_params=_cp(("parallel", "arbitrary")),
        name="rwkv_prep",
    )(z, first, pw["mu"], pw["w0"], pw["ww2"], pw["a0"], pw["wa2"], pw["wg2"], pw["k_k"], pw["k_a"])


def _rwkv_scan_kernel(r_ref, w_ref, k_ref, v_ref, kk_ref, nkka_ref, s0_ref, m4_ref, y_ref, sf_ref, s_scr,
                      xa_scr, xb_scr, y8_scr, rows_scr, *, nb, tc):
    c = pl.program_id(1)

    @pl.when(c == 0)
    def _():
        s_scr[...] = s0_ref[...]

    lane = lax.broadcasted_iota(jnp.int32, (A_HEAD, LANE), 1)
    sub = lax.broadcasted_iota(jnp.int32, (A_HEAD, LANE), 0)
    lo = lane < A_HEAD
    diag = (lane & (A_HEAD - 1)) == sub
    diag_lo = jnp.logical_and(diag, lo)
    diag_hi = jnp.logical_and(diag, jnp.logical_not(lo))
    nq = A_WIDTH // SEG_W
    npair = A_WIDTH // LANE
    ngroup = xa_scr.shape[0]
    gb = nb // ngroup

    seq_refs = (r_ref, w_ref, k_ref, v_ref, kk_ref, nkka_ref)
    seq_ids = [id(ref) for ref in seq_refs]

    def step(t8, carry):
        base = pl.multiple_of(t8 * SUBLANE, SUBLANE)
        for a, ref in enumerate(seq_refs):
            for b in range(nb):
                rows_scr[a, b] = ref[b, pl.ds(base, SUBLANE), :]
        for i in range(SUBLANE):
            row = lambda ref, b, lanes: rows_scr[seq_ids.index(id(ref)), b, i:i + 1, lanes]

            def phase_a(g):
                for bl in range(gb):
                    b = g * gb + bl
                    for j in range(nq):
                        q = slice(j * SEG_W, (j + 1) * SEG_W)
                        r0 = (bl * nq + j) * A_HEAD
                        xa_scr[g, r0:r0 + A_HEAD, :] = (s_scr[b, :, q] * row(kk_ref, b, q)).astype(BF16)
                return jnp.dot(xa_scr[g], m4_ref[...], preferred_element_type=F32)

            def phase_b(g, sa_all):
                for bl in range(gb):
                    b = g * gb + bl
                    for j in range(npair):
                        sl = slice(j * LANE, (j + 1) * LANE)
                        r0 = (bl * nq + j // 2) * A_HEAD
                        half = slice((j % 2) * LANE, (j % 2 + 1) * LANE)
                        vrow = row(v_ref, b, sl)
                        v_lo = jnp.sum(jnp.where(diag_lo, vrow, 0.0), axis=1, keepdims=True)
                        v_hi = jnp.sum(jnp.where(diag_hi, vrow, 0.0), axis=1, keepdims=True)
                        vb = jnp.where(lo, v_lo, v_hi)
                        s = (s_scr[b, :, sl] * row(w_ref, b, sl)
                             + sa_all[r0:r0 + A_HEAD, half] * row(nkka_ref, b, sl) + vb * row(k_ref, b, sl))
                        s_scr[b, :, sl] = s
                        xb_scr[g, r0:r0 + A_HEAD, half] = (s * row(r_ref, b, sl)).astype(BF16)
                return jnp.dot(xb_scr[g], m4_ref[...], preferred_element_type=F32)

            def phase_c(g, y_all):
                for bl in range(gb):
                    b = g * gb + bl
                    for j in range(npair):
                        sl = slice(j * LANE, (j + 1) * LANE)
                        r0 = (bl * nq + j // 2) * A_HEAD
                        half = slice((j % 2) * LANE, (j % 2 + 1) * LANE)
                        yb = y_all[r0:r0 + A_HEAD, half]
                        y8_scr[b, i:i + 1, sl] = jnp.sum(jnp.where(diag, yb, 0.0), axis=0, keepdims=True)

            sa = [phase_a(g) for g in range(ngroup)]
            ya = [phase_b(g, sa[g]) for g in range(ngroup)]
            for g in range(ngroup):
                phase_c(g, ya[g])
        for b in range(nb):
            y_ref[b, pl.ds(base, SUBLANE), :] = y8_scr[b]
        return carry

    lax.fori_loop(0, tc // SUBLANE, step, 0)

    @pl.when(c == pl.num_programs(1) - 1)
    def _():
        sf_ref[...] = s_scr[...]


def _rwkv_scan(seqs, s0, nb, tc):
    bsz, t, _ = seqs[0].shape
    seq = pl.BlockSpec((nb, tc, A_WIDTH), lambda i, c: (i, c, 0))
    st = pl.BlockSpec((nb, A_HEAD, A_WIDTH), lambda i, c: (i, 0, 0))
    seg = jnp.arange(SEG_W, dtype=jnp.int32) // A_HEAD
    m4 = (seg[:, None] == seg[None, :]).astype(BF16)
    ngroup = 2
    xrows = (nb // ngroup) * (A_WIDTH // SEG_W) * A_HEAD
    x_scr = pltpu.VMEM((ngroup, xrows, SEG_W), BF16)
    return pl.pallas_call(
        functools.partial(_rwkv_scan_kernel, nb=nb, tc=tc),
        out_shape=[jax.ShapeDtypeStruct((bsz, t, A_WIDTH), F32), jax.ShapeDtypeStruct((bsz, A_HEAD, A_WIDTH), F32)],
        grid=(bsz // nb, t // tc),
        in_specs=[seq] * 6 + [st, pl.BlockSpec((SEG_W, SEG_W), lambda i, c: (0, 0))],
        out_specs=[seq, st],
        scratch_shapes=[pltpu.VMEM((nb, A_HEAD, A_WIDTH), F32), x_scr, x_scr,
                        pltpu.VMEM((nb, SUBLANE, A_WIDTH), F32), pltpu.VMEM((6, nb, SUBLANE, A_WIDTH), F32)],
        compiler_params=_cp(("parallel", "arbitrary")),
        name="rwkv_scan",
    )(*seqs, s0, m4)


def _retention_kernel(q_ref, k_ref, v_ref, g_ref, cos_ref, sin_ref, dm_ref, qd_ref, kd_ref, gc_ref, r0_ref,
                      o_ref, rn_ref, r_scr):
    c = pl.program_id(1)
    nbr = r_scr.shape[0]
    chunk = cos_ref.shape[0]

    @pl.when(c == 0)
    def _():
        r_scr[...] = r0_ref[...]

    cos = cos_ref[...]
    sin = sin_ref[...]

    def rot(x):
        return x * cos + pltpu.roll(x, B_QK_HEAD // 2, 1) * sin

    for bi in range(nbr):
        rows = slice(bi * chunk, (bi + 1) * chunk)
        for h in range(B_HEADS):
            qs = slice(h * B_QK_HEAD, (h + 1) * B_QK_HEAD)
            vs = slice(h * B_V_HEAD, (h + 1) * B_V_HEAD)
            q = rot(q_ref[rows, qs])
            k = rot(k_ref[rows, qs]) * (B_QK_HEAD ** -0.5)
            v = v_ref[rows, vs].astype(BF16)
            rs = r_scr[bi, h]
            qb = q.astype(BF16)
            s = lax.dot_general(qb, k.astype(BF16), (((1,), (1,)), ((), ())),
                                preferred_element_type=F32) * dm_ref[h]
            inner = jnp.dot(s.astype(BF16), v, preferred_element_type=F32)
            cross = jnp.dot(qb, rs.astype(BF16), preferred_element_type=F32) * qd_ref[h]
            kd = (k * kd_ref[h]).astype(BF16)
            r_scr[bi, h] = gc_ref[h] * rs + lax.dot_general(kd, v, (((0,), (0,)), ((), ())),
                                                            preferred_element_type=F32)
            o = inner + cross
            o = o * lax.rsqrt(jnp.mean(o * o, axis=-1, keepdims=True) + NORM_EPS)
            gate = g_ref[rows, vs]
            o_ref[rows, vs] = gate * _sigmoid(gate) * o

    @pl.when(c == pl.num_programs(1) - 1)
    def _():
        rn_ref[...] = r_scr[...]


def _retention(z, r0, tabs, bsz, t, chunk):
    cos, sin, dm, qd, kd, gc = tabs
    nc = t // chunk
    nbr = 4 if nc == 1 else 1
    seq = lambda w, col: pl.BlockSpec((nbr * chunk, w), lambda b, c: (b * nc + c, col // w))
    tab3 = lambda a: pl.BlockSpec(a.shape, lambda b, c: (0, 0, 0))
    st = pl.BlockSpec((nbr, B_HEADS, B_QK_HEAD, B_V_HEAD), lambda b, c: (b, 0, 0, 0))
    return pl.pallas_call(
        _retention_kernel,
        out_shape=[jax.ShapeDtypeStruct((bsz * t, B_V_WIDTH), F32),
                   jax.ShapeDtypeStruct((bsz, B_HEADS, B_QK_HEAD, B_V_HEAD), F32)],
        grid=(bsz // nbr, nc),
        in_specs=[
            seq(B_QK_WIDTH, COL_QB), seq(B_QK_WIDTH, COL_KB), seq(B_V_WIDTH, COL_VB), seq(B_V_WIDTH, COL_GB),
            pl.BlockSpec((chunk, B_QK_HEAD), lambda b, c: (c, 0)),
            pl.BlockSpec((chunk, B_QK_HEAD), lambda b, c: (c, 0)),
            tab3(dm), tab3(qd), tab3(kd), tab3(gc), st,
        ],
        out_specs=[pl.BlockSpec((nbr * chunk, B_V_WIDTH), lambda b, c: (b * nc + c, 0)), st],
        scratch_shapes=[pltpu.VMEM((nbr, B_HEADS, B_QK_HEAD, B_V_HEAD), F32)],
        compiler_params=_cp(("parallel", "arbitrary")),
        name="retention",
    )(z, z, z, z, cos, sin, dm, qd, kd, gc, r0)


def _retention_tables(pos, chunk):
    half = B_QK_HEAD // 2
    inv = jnp.power(ROT_BASE, -jnp.linspace(0.0, 1.0, half, dtype=F32))
    ang = pos[:, None] * inv[None, :]
    cos = jnp.cos(ang)
    sin = jnp.sin(ang)
    cos2 = jnp.concatenate([cos, cos], axis=-1)
    sin2 = jnp.concatenate([-sin, sin], axis=-1)
    log_g = jnp.log(1.0 - jnp.power(2.0, -5.0 - jnp.arange(B_HEADS, dtype=F32)))
    idx = jnp.arange(chunk, dtype=F32)
    diff = idx[:, None] - idx[None, :]
    dmask = jnp.where(diff >= 0, jnp.exp(log_g[:, None, None] * jnp.maximum(diff, 0.0)), 0.0)
    q_dec = jnp.exp(log_g[:, None] * (idx[None, :] + 1.0))
    k_dec = jnp.exp(log_g[:, None] * (chunk - 1.0 - idx[None, :]))
    g_c = jnp.exp(log_g * chunk)
    qd = jnp.broadcast_to(q_dec[:, :, None], (B_HEADS, chunk, B_V_HEAD))
    kd = jnp.broadcast_to(k_dec[:, :, None], (B_HEADS, chunk, B_QK_HEAD))
    gc = jnp.broadcast_to(g_c[:, None, None], (B_HEADS, B_QK_HEAD, B_V_HEAD))
    return cos2, sin2, dmask, qd, kd, gc


def _merge1_kernel(y_ref, r_ref, k_ref, v_ref, g_ref, ob_ref, gma_ref, gmb_ref, lw_ref, lb_ref, rk_ref, wpa_ref,
                   wpb_ref, m_ref, oa_scr, ob_scr):
    @pl.when(pl.program_id(1) == 0)
    def _():
        ones = _seg_ones()
        y = y_ref[...]
        mean = _segsum64(y, ones) * (1.0 / A_HEAD)
        yc = y - mean
        var = _segsum64(yc * yc, ones) * (1.0 / A_HEAD)
        yn = yc * lax.rsqrt(var + A_GN_EPS) * lw_ref[...] + lb_ref[...]
        bonus = _segsum64(r_ref[...] * k_ref[...] * rk_ref[...], ones) * v_ref[...]
        oa_scr[...] = ((yn + bonus) * g_ref[...]).astype(BF16)
        ob_scr[...] = ob_ref[...].astype(BF16)

    pa = jnp.dot(oa_scr[...], wpa_ref[...], preferred_element_type=F32)
    pb = jnp.dot(ob_scr[...], wpb_ref[...], preferred_element_type=F32)
    m_ref[...] = (_sigmoid(gma_ref[...]) * pa + _sigmoid(gmb_ref[...]) * pb).astype(m_ref.dtype)


def _merge1(y, r, k2, v, g, ob, z, pw, tm=512, tn=512):
    n = y.shape[0]
    a_blk = pl.BlockSpec((tm, A_WIDTH), lambda i, j: (i, 0))
    row = pl.BlockSpec((1, A_WIDTH), lambda i, j: (0, 0))
    return pl.pallas_call(
        _merge1_kernel,
        out_shape=jax.ShapeDtypeStruct((n, D_MODEL), BF16),
        grid=(n // tm, D_MODEL // tn),
        in_specs=[
            a_blk, a_blk, a_blk, a_blk, a_blk,
            pl.BlockSpec((tm, B_V_WIDTH), lambda i, j: (i, 0)),
            pl.BlockSpec((tm, tn), lambda i, j: (i, COL_GMA // tn + j)),
            pl.BlockSpec((tm, tn), lambda i, j: (i, COL_GMB // tn + j)),
            row, row, row,
            pl.BlockSpec((A_WIDTH, tn), lambda i, j: (0, j)),
            pl.BlockSpec((B_V_WIDTH, tn), lambda i, j: (0, j)),
        ],
        out_specs=pl.BlockSpec((tm, tn), lambda i, j: (i, j)),
        scratch_shapes=[pltpu.VMEM((tm, A_WIDTH), BF16), pltpu.VMEM((tm, B_V_WIDTH), BF16)],
        compiler_params=_cp(("parallel", "arbitrary")),
        name="merge1",
    )(y, r, k2, v, g, ob, z, z, pw["lnx_w"], pw["lnx_b"], pw["r_k"], pw["wpa"], pw["wpb"])


def _merge2_kernel(m_ref, x_ref, w_ref, g_ref, x1_ref, xn_ref):
    x1 = x_ref[...] + jnp.dot(m_ref[...], w_ref[...], preferred_element_type=F32)
    x1_ref[...] = x1
    ms = jnp.mean(x1 * x1, axis=-1, keepdims=True)
    xn_ref[...] = (x1 * lax.rsqrt(ms + NORM_EPS) * g_ref[...]).astype(BF16)


def _merge2(m, x, w_out, norm2, tm=512):
    n = x.shape[0]
    blk = pl.BlockSpec((tm, D_MODEL), lambda i: (i, 0))
    return pl.pallas_call(
        _merge2_kernel,
        out_shape=[jax.ShapeDtypeStruct((n, D_MODEL), F32), jax.ShapeDtypeStruct((n, D_MODEL), BF16)],
        grid=(n // tm,),
        in_specs=[blk, blk, pl.BlockSpec((D_MODEL, D_MODEL), lambda i: (0, 0)),
                  pl.BlockSpec((1, D_MODEL), lambda i: (0, 0))],
        out_specs=[blk, blk],
        compiler_params=_cp(("parallel",)),
        name="merge2",
    )(m, x, w_out, norm2)


_PAIRS = [(i, j) for i in range(PEER_TOPK) for j in range(PEER_TOPK) if (i + 1) * (j + 1) <= PEER_TOPK]


def _top16_rows(s):
    rows = []
    cur = s
    for i in range(PEER_TOPK):
        m = jnp.max(cur, axis=0, keepdims=True)
        rows.append(m)
        if i + 1 < PEER_TOPK:
            cur = jnp.where(cur == m, -jnp.inf, cur)
    return rows


def _peer_topk_kernel(xn_ref, wq_ref, keys_ref, s1_ref, s2_ref, st_ref, top_scr, cand_scr):
    q = jnp.dot(xn_ref[...], wq_ref[...], preferred_element_type=F32)
    for h in range(PEER_HEADS):
        for p in range(2):
            c0 = (h * 2 + p) * N_KEYS
            qhp = q[:, c0:c0 + N_KEYS].astype(BF16)
            st = lax.dot_general(keys_ref[h, p], qhp, (((1,), (1,)), ((), ())), preferred_element_type=F32)
            (s1_ref if p == 0 else s2_ref)[h] = st
            for i, row in enumerate(_top16_rows(st)):
                top_scr[p, i, h:h + 1, :] = row
    for n, (i, j) in enumerate(_PAIRS):
        cand_scr[n] = top_scr[0, i] + top_scr[1, j]
    m = None
    for it in range(PEER_TOPK):
        m = cand_scr[0]
        for n in range(1, len(_PAIRS)):
            m = jnp.maximum(m, cand_scr[n])
        if it + 1 < PEER_TOPK:
            for n in range(len(_PAIRS)):
                cnd = cand_scr[n]
                cand_scr[n] = jnp.where(cnd == m, -jnp.inf, cnd)
    tau = m
    top = top_scr[0, 0] + top_scr[1, 0]
    zsum = jnp.zeros_like(tau)
    for (i, j) in _PAIRS:
        cnd = top_scr[0, i] + top_scr[1, j]
        zsum = zsum + jnp.where(cnd >= tau, jnp.exp(cnd - top), 0.0)
    st_ref[0] = tau
    st_ref[1] = top_scr[0, 0]
    st_ref[2] = top_scr[1, 0]
    st_ref[3] = 1.0 / zsum


def _peer_topk(xn, wq, keys, tm=512):
    n = xn.shape[0]
    s_shape = jax.ShapeDtypeStruct((PEER_HEADS, N_KEYS, n), F32)
    s_spec = pl.BlockSpec((PEER_HEADS, N_KEYS, tm), lambda i: (0, 0, i))
    return pl.pallas_call(
        _peer_topk_kernel,
        out_shape=[s_shape, s_shape, jax.ShapeDtypeStruct((4, PEER_HEADS, n), F32)],
        grid=(n // tm,),
        in_specs=[pl.BlockSpec((tm, D_MODEL), lambda i: (i, 0)),
                  pl.BlockSpec((D_MODEL, D_MODEL), lambda i: (0, 0)),
                  pl.BlockSpec(keys.shape, lambda i: (0, 0, 0, 0))],
        out_specs=[s_spec, s_spec, pl.BlockSpec((4, PEER_HEADS, tm), lambda i: (0, 0, i))],
        scratch_shapes=[pltpu.VMEM((2, PEER_TOPK, PEER_HEADS, tm), F32),
                        pltpu.VMEM((len(_PAIRS), PEER_HEADS, tm), F32)],
        compiler_params=_cp(("parallel",)),
        name="peer_topk",
    )(xn, wq, keys)


def _peer_dense_kernel(xn_ref, s1_ref, s2_ref, st_ref, u_ref, vt_ref, o_ref, acc_scr, b_scr, sc_scr, w_scr, *, ec, tm):
    c = pl.program_id(1)
    nc = pl.num_programs(1) - 2
    n1 = ec // N_KEYS

    @pl.when(c == 0)
    def _():
        acc_scr[...] = jnp.zeros_like(acc_scr)
        sc_scr[...] = jnp.zeros_like(sc_scr)
        w_scr[...] = jnp.zeros_like(w_scr)
        for h in range(PEER_HEADS):
            b_scr[h] = jnp.exp(s2_ref[h] - st_ref[2, pl.ds(h, 1), :]) * st_ref[3, pl.ds(h, 1), :]

    slot = c % 2
    sc_scr[slot] = lax.dot_general(u_ref[...], xn_ref[...], (((1,), (1,)), ((), ())), preferred_element_type=F32)

    i1_base = pl.multiple_of(jnp.clip(c - 1, 0, nc - 1) * n1, SUBLANE)
    for ii in range(n1):
        for tcol in range(tm // LANE):
            cs = slice(tcol * LANE, (tcol + 1) * LANE)
            g = jnp.zeros((N_KEYS, LANE), F32)
            for h in range(PEER_HEADS):
                s1row = s1_ref[h, pl.ds(i1_base, n1), cs][ii:ii + 1, :]
                theta = st_ref[0, pl.ds(h, 1), cs] - s1row
                arow = jnp.exp(s1row - st_ref[1, pl.ds(h, 1), cs])
                g = g + jnp.where(s2_ref[h, :, cs] >= theta, arow * b_scr[h, :, cs], 0.0)
            x = sc_scr[1 - slot, ii * N_KEYS:(ii + 1) * N_KEYS, cs]
            gelu = 0.5 * x * (1.0 + lax.erf(x * np.float32(np.sqrt(0.5))))
            w_scr[1 - slot, ii * N_KEYS:(ii + 1) * N_KEYS, cs] = (g * gelu).astype(BF16)

    acc_scr[...] += jnp.dot(vt_ref[...], w_scr[slot], preferred_element_type=F32)

    @pl.when(c == pl.num_programs(1) - 1)
    def _():
        o_ref[...] = acc_scr[...].T


def _peer_dense(xn_bf, s1, s2, stats, u_bf, vt_bf, tm=512, ec=1024):
    assert ec // N_KEYS == SUBLANE
    n = xn_bf.shape[0]
    nc = u_bf.shape[0] // ec
    s_spec = pl.BlockSpec((PEER_HEADS, N_KEYS, tm), lambda i, c: (0, 0, i))
    return pl.pallas_call(
        functools.partial(_peer_dense_kernel, ec=ec, tm=tm),
        out_shape=jax.ShapeDtypeStruct((n, D_MODEL), F32),
        grid=(n // tm, nc + 2),
        in_specs=[
            pl.BlockSpec((tm, D_MODEL), lambda i, c: (i, 0)),
            s_spec, s_spec,
            pl.BlockSpec((4, PEER_HEADS, tm), lambda i, c: (0, 0, i)),
            pl.BlockSpec((ec, D_MODEL), lambda i, c: (jnp.minimum(c, nc - 1), 0)),
            pl.BlockSpec((D_MODEL, ec), lambda i, c: (0, jnp.clip(c - 2, 0, nc - 1))),
        ],
        out_specs=pl.BlockSpec((tm, D_MODEL), lambda i, c: (i, 0)),
        scratch_shapes=[pltpu.VMEM((D_MODEL, tm), F32), pltpu.VMEM((PEER_HEADS, N_KEYS, tm), F32),
                        pltpu.VMEM((2, ec, tm), F32), pltpu.VMEM((2, ec, tm), BF16)],
        compiler_params=_cp(("parallel", "arbitrary")),
        name="peer_dense",
    )(xn_bf, s1, s2, stats, u_bf, vt_bf)


def _final_kernel(x1_ref, p_ref, g_ref, y_ref):
    x = x1_ref[...] + p_ref[...]
    ms = jnp.mean(x * x, axis=-1, keepdims=True)
    y_ref[...] = x * lax.rsqrt(ms + NORM_EPS) * g_ref[...]


def _final(x1, p, g, tm=512):
    n = x1.shape[0]
    blk = pl.BlockSpec((tm, D_MODEL), lambda i: (i, 0))
    return pl.pallas_call(
        _final_kernel,
        out_shape=jax.ShapeDtypeStruct((n, D_MODEL), F32),
        grid=(n // tm,),
        in_specs=[blk, blk, pl.BlockSpec((1, D_MODEL), lambda i: (0, 0))],
        out_specs=blk,
        compiler_params=_cp(("parallel",)),
        name="final",
    )(x1, p, g)


def _pad_cols(a, w):
    return jnp.pad(a, ((0, 0), (0, w - a.shape[1])))


def _za_layout(a):
    r3 = a[:, :3 * A_WIDTH]
    wl = _pad_cols(a[:, 3 * A_WIDTH:3 * A_WIDTH + A_DECAY_LORA], COL_AL - COL_WL)
    al = _pad_cols(a[:, 3 * A_WIDTH + A_DECAY_LORA:3 * A_WIDTH + A_DECAY_LORA + A_ICL_LORA], COL_GL - COL_AL)
    gl = _pad_cols(a[:, 3 * A_WIDTH + A_DECAY_LORA + A_ICL_LORA:SHIFT_W], ZA_W - COL_GL)
    return jnp.concatenate([r3, wl, al, gl], axis=1)


def _za_unlayout(a):
    return jnp.concatenate([a[:, :3 * A_WIDTH], a[:, COL_WL:COL_WL + A_DECAY_LORA], a[:, COL_AL:COL_AL + A_ICL_LORA],
                            a[:, COL_GL:COL_GL + A_GATE_LORA]], axis=1)


def _pad_rows(a, n):
    return jnp.pad(a, ((0, n - a.shape[0]), (0, 0)))


def _prepare_weights(lw):
    (norm1, w_in, mu_shift, w0, w_w2, a0, w_a2, w_g2, k_k, k_a, r_k, lnx_w, lnx_b, w_pa, w_pb, w_out, norm2,
     peer_wq, peer_keys, peer_u, peer_v) = lw
    row = lambda a: a[None, :]
    wcat = jnp.concatenate([
        _za_layout(w_in[:, :SHIFT_W]), jnp.zeros((D_MODEL, COL_QB - ZA_W), F32), w_in[:, SHIFT_W:]], axis=1)
    return dict(
        norm1=row(norm1), wcat=wcat.astype(BF16), mu=_za_layout(row(mu_shift)), w0=row(w0),
        ww2=_pad_rows(w_w2, COL_AL - COL_WL).astype(BF16), a0=row(a0),
        wa2=_pad_rows(w_a2, COL_GL - COL_AL).astype(BF16), wg2=_pad_rows(w_g2, ZA_W - COL_GL).astype(BF16),
        k_k=row(k_k), k_a=row(k_a), r_k=row(r_k), lnx_w=row(lnx_w), lnx_b=row(lnx_b),
        wpa=w_pa.astype(BF16), wpb=w_pb.astype(BF16), wout=w_out.astype(BF16), norm2=row(norm2),
        wq=peer_wq.astype(BF16), keys=peer_keys.astype(BF16), u=peer_u.astype(BF16), vt=peer_v.T.astype(BF16))


def _group_layer(x, shift0, wkv0, ret0, pos, pw, scan_tc):
    bsz, t, d = x.shape
    n = bsz * t
    z = _inproj(x.reshape(n, d), pw["norm1"], pw["wcat"])
    shift_new = _za_unlayout(z.reshape(bsz, t, W_TOT)[:, -1, :ZA_W])

    r, w, k2, v, kk, nkka, g = _rwkv_prep(z, _za_layout(shift0), bsz, t, pw)
    s0 = jnp.swapaxes(wkv0, 1, 2).reshape(bsz, A_HEAD, A_WIDTH)
    y, sf = _rwkv_scan([a.reshape(bsz, t, A_WIDTH) for a in (r, w, k2, v, kk, nkka)], s0, nb=4, tc=scan_tc)
    wkv_new = jnp.swapaxes(sf.reshape(bsz, A_HEAD, A_HEADS, A_HEAD), 1, 2)

    chunk = RET_CHUNK if t % RET_CHUNK == 0 else t
    ob, ret_new = _retention(z, ret0, _retention_tables(pos, chunk), bsz, t, chunk)

    m = _merge1(y.reshape(n, A_WIDTH), r, k2, v, g, ob, z, pw)
    x1, xn = _merge2(m, x.reshape(n, d), pw["wout"], pw["norm2"])
    s1, s2, stats = _peer_topk(xn, pw["wq"], pw["keys"])
    peer = _peer_dense(xn, s1, s2, stats, pw["u"], pw["vt"])
    return x1, peer, shift_new, wkv_new, ret_new


def _trunk(x, shift0, wkv0, ret0, pos, pws, norm_f, scan_tc):
    bsz, t, d = x.shape
    shifts, wkvs, rets = [], [], []
    x1 = peer = None
    for layer, pw in enumerate(pws):
        if layer > 0:
            x = (x1 + peer).reshape(bsz, t, d)
        x1, peer, s_new, wkv_new, ret_new = _group_layer(x, shift0[layer], wkv0[layer], ret0[layer], pos, pw, scan_tc)
        shifts.append(s_new)
        wkvs.append(wkv_new)
        rets.append(ret_new)
    y = _final(x1, peer, norm_f[None, :]).reshape(bsz, t, d)
    return y, jnp.stack(shifts), jnp.stack(wkvs), jnp.stack(rets)


def kernel(x_prompt, x_sample, state_shift, state_wkv, state_ret, norm1, w_in, mu_shift, w0, w_w2, a0, w_a2, w_g2, k_k, k_a, r_k, lnx_w, lnx_b, w_pa, w_pb, w_out, norm2, peer_wq, peer_keys, peer_u, peer_v, norm_f):
    weights = (norm1, w_in, mu_shift, w0, w_w2, a0, w_a2, w_g2, k_k, k_a, r_k, lnx_w, lnx_b, w_pa, w_pb, w_out,
               norm2, peer_wq, peer_keys, peer_u, peer_v)
    depth = norm1.shape[0]
    pws = [_prepare_weights([w[layer] for w in weights]) for layer in range(depth)]
    bp, tp, _ = x_prompt.shape
    ts = x_sample.shape[1]
    shift0 = jnp.zeros((depth, bp, SHIFT_W), F32)
    wkv0 = jnp.zeros((depth, bp, A_HEADS, A_HEAD, A_HEAD), F32)
    ret0 = jnp.zeros((depth, bp, B_HEADS, B_QK_HEAD, B_V_HEAD), F32)
    y_p, shift_p, wkv_p, ret_p = _trunk(x_prompt, shift0, wkv0, ret0, jnp.arange(tp, dtype=F32), pws, norm_f,
                                        scan_tc=128)
    y_s, shift_s, wkv_s, ret_s = _trunk(x_sample, state_shift, state_wkv, state_ret,
                                        PAST_LEN + jnp.arange(ts, dtype=F32), pws, norm_f, scan_tc=ts)
    return (y_p, y_s, shift_p, wkv_p, ret_p, shift_s, wkv_s, ret_s)
```

```python
import functools

import jax
import jax.numpy as jnp
import numpy as np
from jax import lax
from jax.experimental import pallas as pl
from jax.experimental.pallas import tpu as pltpu

F32 = jnp.float32
BF16 = jnp.bfloat16

D_MODEL = 2048
A_HEAD = 64
A_WIDTH = 1024
A_HEADS = 16
A_DECAY_LORA = 64
A_ICL_LORA = 64
A_GATE_LORA = 160
A_GN_EPS = 64e-5
B_HEADS = 8
B_QK_HEAD = 128
B_V_HEAD = 256
B_QK_WIDTH = 1024
B_V_WIDTH = 2048
RET_CHUNK = 128
ROT_BASE = 10000.0
N_KEYS = 128
N_EXPERTS = N_KEYS * N_KEYS
PEER_HEADS = 8
PEER_TOPK = 16
NORM_EPS = 1e-6
PAST_LEN = 16384
SHIFT_W = 3 * A_WIDTH + A_DECAY_LORA + A_ICL_LORA + A_GATE_LORA

LANE = 128
SUBLANE = 8
SEG_W = 256
ZA_W = 3584
COL_WL = 3072
COL_AL = 3200
COL_GL = 3328
COL_QB = 4096
COL_KB = 5120
COL_VB = 6144
COL_GB = 8192
COL_GMA = 10240
COL_GMB = 12288
W_TOT = 14336
VMEM_LIMIT = 56 * 1024 * 1024


def _cp(sem):
    return pltpu.CompilerParams(dimension_semantics=sem, vmem_limit_bytes=VMEM_LIMIT)


def _sigmoid(x):
    return 1.0 / (1.0 + jnp.exp(-x))


def _inproj_kernel(x_ref, g_ref, w_ref, o_ref, hn_ref):
    @pl.when(pl.program_id(1) == 0)
    def _():
        x = x_ref[...]
        ms = jnp.mean(x * x, axis=-1, keepdims=True)
        hn_ref[...] = (x * lax.rsqrt(ms + NORM_EPS) * g_ref[...]).astype(BF16)

    o_ref[...] = jnp.dot(hn_ref[...], w_ref[...], preferred_element_type=F32)


def _inproj(x, g, w, tm=1024, tn=512):
    n, d = x.shape
    wt = w.shape[1]
    return pl.pallas_call(
        _inproj_kernel,
        out_shape=jax.ShapeDtypeStruct((n, wt), F32),
        grid=(n // tm, wt // tn),
        in_specs=[
            pl.BlockSpec((tm, d), lambda i, j: (i, 0)),
            pl.BlockSpec((1, d), lambda i, j: (0, 0)),
            pl.BlockSpec((d, tn), lambda i, j: (0, j)),
        ],
        out_specs=pl.BlockSpec((tm, tn), lambda i, j: (i, j)),
        scratch_shapes=[pltpu.VMEM((tm, d), BF16)],
        compiler_params=_cp(("parallel", "arbitrary")),
        name="inproj",
    )(x, g, w)


def _seg_ones():
    r = lax.broadcasted_iota(jnp.int32, (SEG_W, SEG_W), 0) // A_HEAD
    c = lax.broadcasted_iota(jnp.int32, (SEG_W, SEG_W), 1) // A_HEAD
    return (r == c).astype(BF16)


def _segsum64(x, ones):
    def top8(v):
        bits = lax.bitcast_convert_type(v, jnp.uint32) & jnp.uint32(0xFFFF0000)
        return lax.bitcast_convert_type(bits, F32)

    hi = top8(x)
    r1 = x - hi
    mid = top8(r1)
    lo = r1 - mid
    hi, mid, lo = hi.astype(BF16), mid.astype(BF16), lo.astype(BF16)
    parts = []
    for j in range(x.shape[1] // SEG_W):
        sl = slice(j * SEG_W, (j + 1) * SEG_W)
        parts.append(jnp.dot(hi[:, sl], ones, preferred_element_type=F32)
                     + jnp.dot(mid[:, sl], ones, preferred_element_type=F32)
                     + jnp.dot(lo[:, sl], ones, preferred_element_type=F32))
    return jnp.concatenate(parts, axis=1)


def _rwkv_prep_kernel(z_ref, first_ref, mu_ref, w0_ref, ww2_ref, a0_ref, wa2_ref, wg2_ref, kk_ref_, ka_ref,
                      r_o, w_o, k_o, v_o, kk_o, nkka_o, g_o, carry_scr, *, tm, seq_len):
    z = z_ref[...]
    rolled = pltpu.roll(z, 1, 0)
    rows = lax.broadcasted_iota(jnp.int32, z.shape, 0)
    if seq_len >= tm:
        @pl.when(pl.program_id(1) == 0)
        def _():
            carry_scr[0:1, :] = first_ref[0]

        zprev = jnp.where(rows == 0, carry_scr[0:1, :], rolled)
        carry_scr[0:1, :] = z[tm - 1:tm, :]
    else:
        zprev = jnp.where(rows % seq_len == 0, first_ref[...], rolled)
    zs = z + (zprev - z) * mu_ref[...]
    r = zs[:, 0:A_WIDTH]
    k = zs[:, A_WIDTH:2 * A_WIDTH]
    v = zs[:, 2 * A_WIDTH:3 * A_WIDTH]
    wl = zs[:, COL_WL:COL_AL]
    al = zs[:, COL_AL:COL_GL]
    gl = zs[:, COL_GL:ZA_W]
    wpre = w0_ref[...] + jnp.dot(jnp.tanh(wl).astype(BF16), ww2_ref[...], preferred_element_type=F32)
    nw = -wpre
    softplus = jnp.maximum(nw, 0.0) + jnp.log1p(jnp.exp(-jnp.abs(nw)))
    wlog = -softplus - 0.5
    decay = jnp.exp(-jnp.exp(wlog))
    a = _sigmoid(a0_ref[...] + jnp.dot(al.astype(BF16), wa2_ref[...], preferred_element_type=F32))
    g = jnp.dot(_sigmoid(gl).astype(BF16), wg2_ref[...], preferred_element_type=F32)
    kk = k * kk_ref_[...]
    ss = _segsum64(kk * kk, _seg_ones())
    kk = kk / jnp.maximum(jnp.sqrt(ss), 1e-12)
    k2 = k * (1.0 + (a - 1.0) * ka_ref[...])
    r_o[...] = r
    w_o[...] = decay
    k_o[...] = k2
    v_o[...] = v
    kk_o[...] = kk
    nkka_o[...] = -(kk * a)
    g_o[...] = g


def _rwkv_prep(z, shift0_l, bsz, t, pw, tm=256):
    n = bsz * t
    nt = max(t // tm, 1)
    if t >= tm:
        first = shift0_l[:, None, :]
        first_spec = pl.BlockSpec((1, 1, ZA_W), lambda i, j: (i, 0, 0))
    else:
        first = jnp.pad(shift0_l[:, None, :], ((0, 0), (0, t - 1), (0, 0))).reshape(n, ZA_W)
        first_spec = pl.BlockSpec((tm, ZA_W), lambda i, j: (i * nt + j, 0))
    row = lambda w: pl.BlockSpec((1, w), lambda i, j: (0, 0))
    full = lambda a: pl.BlockSpec(a.shape, lambda i, j: (0, 0))
    out = jax.ShapeDtypeStruct((n, A_WIDTH), F32)
    return pl.pallas_call(
        functools.partial(_rwkv_prep_kernel, tm=tm, seq_len=t),
        out_shape=[out] * 7,
        grid=(n // (tm * nt), nt),
        in_specs=[
            pl.BlockSpec((tm, ZA_W), lambda i, j: (i * nt + j, 0)),
            first_spec,
            row(ZA_W), row(A_WIDTH), full(pw["ww2"]), row(A_WIDTH), full(pw["wa2"]), full(pw["wg2"]),
            row(A_WIDTH), row(A_WIDTH),
        ],
        out_specs=[pl.BlockSpec((tm, A_WIDTH), lambda i, j: (i * nt + j, 0))] * 7,
        scratch_shapes=[pltpu.VMEM((SUBLANE, ZA_W), F32)],
        compiler_params=_cp(("parallel", "arbitrary")),
        name="rwkv_prep",
    )(z, first, pw["mu"], pw["w0"], pw["ww2"], pw["a0"], pw["wa2"], pw["wg2"], pw["k_k"], pw["k_a"])


def _rwkv_scan_kernel(r_ref, w_ref, k_ref, v_ref, kk_ref, nkka_ref, s0_ref, m4_ref, y_ref, sf_ref, s_scr,
                      xa_scr, xb_scr, y8_scr, rows_scr, *, nb, tc):
    c = pl.program_id(1)

    @pl.when(c == 0)
    def _():
        for b in range(nb):
            for h in range(A_HEADS):
                s_scr[b, :, h * A_HEAD:(h + 1) * A_HEAD] = s0_ref[b, h]

    lane = lax.broadcasted_iota(jnp.int32, (A_HEAD, LANE), 1)
    sub = lax.broadcasted_iota(jnp.int32, (A_HEAD, LANE), 0)
    lo = lane < A_HEAD
    diag = (lane & (A_HEAD - 1)) == sub
    diag_lo = jnp.logical_and(diag, lo)
    diag_hi = jnp.logical_and(diag, jnp.logical_not(lo))
    nq = A_WIDTH // SEG_W
    npair = A_WIDTH // LANE
    ngroup = xa_scr.shape[0]
    gb = nb // ngroup

    seq_refs = (r_ref, w_ref, k_ref, v_ref, kk_ref, nkka_ref)
    seq_ids = [id(ref) for ref in seq_refs]

    def step(t8, carry):
        base = pl.multiple_of(t8 * SUBLANE, SUBLANE)
        for a, ref in enumerate(seq_refs):
            for b in range(nb):
                rows_scr[a, b] = ref[b, pl.ds(base, SUBLANE), :]
        for i in range(SUBLANE):
            row = lambda ref, b, lanes: rows_scr[seq_ids.index(id(ref)), b, i:i + 1, lanes]

            def phase_a(g):
                for bl in range(gb):
                    b = g * gb + bl
                    for j in range(nq):
                        q = slice(j * SEG_W, (j + 1) * SEG_W)
                        r0 = (bl * nq + j) * A_HEAD
                        xa_scr[g, r0:r0 + A_HEAD, :] = (s_scr[b, :, q] * row(kk_ref, b, q)).astype(BF16)
                return jnp.dot(xa_scr[g], m4_ref[...], preferred_element_type=F32)

            def phase_b(g, sa_all):
                for bl in range(gb):
                    b = g * gb + bl
                    for j in range(npair):
                        sl = slice(j * LANE, (j + 1) * LANE)
                        r0 = (bl * nq + j // 2) * A_HEAD
                        half = slice((j % 2) * LANE, (j % 2 + 1) * LANE)
                        vrow = row(v_ref, b, sl)
                        v_lo = jnp.sum(jnp.where(diag_lo, vrow, 0.0), axis=1, keepdims=True)
                        v_hi = jnp.sum(jnp.where(diag_hi, vrow, 0.0), axis=1, keepdims=True)
                        vb = jnp.where(lo, v_lo, v_hi)
                        s = (s_scr[b, :, sl] * row(w_ref, b, sl)
                             + sa_all[r0:r0 + A_HEAD, half] * row(nkka_ref, b, sl) + vb * row(k_ref, b, sl))
                        s_scr[b, :, sl] = s
                        xb_scr[g, r0:r0 + A_HEAD, half] = (s * row(r_ref, b, sl)).astype(BF16)
                return jnp.dot(xb_scr[g], m4_ref[...], preferred_element_type=F32)

            def phase_c(g, y_all):
                for bl in range(gb):
                    b = g * gb + bl
                    for j in range(npair):
                        sl = slice(j * LANE, (j + 1) * LANE)
                        r0 = (bl * nq + j // 2) * A_HEAD
                        half = slice((j % 2) * LANE, (j % 2 + 1) * LANE)
                        yb = y_all[r0:r0 + A_HEAD, half]
                        y8_scr[b, i:i + 1, sl] = jnp.sum(jnp.where(diag, yb, 0.0), axis=0, keepdims=True)

            sa = [phase_a(g) for g in range(ngroup)]
            ya = [phase_b(g, sa[g]) for g in range(ngroup)]
            for g in range(ngroup):
                phase_c(g, ya[g])
        for b in range(nb):
            y_ref[b, pl.ds(base, SUBLANE), :] = y8_scr[b]
        return carry

    lax.fori_loop(0, tc // SUBLANE, step, 0)

    @pl.when(c == pl.num_programs(1) - 1)
    def _():
        for b in range(nb):
            for h in range(A_HEADS):
                sf_ref[b, h] = s_scr[b, :, h * A_HEAD:(h + 1) * A_HEAD]


def _rwkv_scan(seqs, s0, nb, tc):
    bsz, t, _ = seqs[0].shape
    seq = pl.BlockSpec((nb, tc, A_WIDTH), lambda i, c: (i, c, 0))
    st = pl.BlockSpec((nb, A_HEADS, A_HEAD, A_HEAD), lambda i, c: (i, 0, 0, 0))
    seg = jnp.arange(SEG_W, dtype=jnp.int32) // A_HEAD
    m4 = (seg[:, None] == seg[None, :]).astype(BF16)
    ngroup = 4
    xrows = (nb // ngroup) * (A_WIDTH // SEG_W) * A_HEAD
    x_scr = pltpu.VMEM((ngroup, xrows, SEG_W), BF16)
    return pl.pallas_call(
        functools.partial(_rwkv_scan_kernel, nb=nb, tc=tc),
        out_shape=[jax.ShapeDtypeStruct((bsz, t, A_WIDTH), F32),
                   jax.ShapeDtypeStruct((bsz, A_HEADS, A_HEAD, A_HEAD), F32)],
        grid=(bsz // nb, t // tc),
        in_specs=[seq] * 6 + [st, pl.BlockSpec((SEG_W, SEG_W), lambda i, c: (0, 0))],
        out_specs=[seq, st],
        scratch_shapes=[pltpu.VMEM((nb, A_HEAD, A_WIDTH), F32), x_scr, x_scr,
                        pltpu.VMEM((nb, SUBLANE, A_WIDTH), F32), pltpu.VMEM((6, nb, SUBLANE, A_WIDTH), F32)],
        compiler_params=_cp(("parallel", "arbitrary")),
        name="rwkv_scan",
    )(*seqs, s0, m4)


def _retention_kernel(q_ref, k_ref, v_ref, g_ref, cos_ref, sin_ref, dm_ref, qd_ref, kd_ref, gc_ref, r0_ref,
                      o_ref, rn_ref, r_scr):
    c = pl.program_id(1)
    nbr = r_scr.shape[0]
    chunk = cos_ref.shape[0]

    @pl.when(c == 0)
    def _():
        r_scr[...] = r0_ref[...]

    cos = cos_ref[...]
    sin = sin_ref[...]

    def rot(x):
        return x * cos + pltpu.roll(x, B_QK_HEAD // 2, 1) * sin

    for bi in range(nbr):
        rows = slice(bi * chunk, (bi + 1) * chunk)
        for h in range(B_HEADS):
            qs = slice(h * B_QK_HEAD, (h + 1) * B_QK_HEAD)
            vs = slice(h * B_V_HEAD, (h + 1) * B_V_HEAD)
            q = rot(q_ref[rows, qs])
            k = rot(k_ref[rows, qs]) * (B_QK_HEAD ** -0.5)
            v = v_ref[rows, vs].astype(BF16)
            rs = r_scr[bi, h]
            qb = q.astype(BF16)
            s = lax.dot_general(qb, k.astype(BF16), (((1,), (1,)), ((), ())),
                                preferred_element_type=F32) * dm_ref[h]
            inner = jnp.dot(s.astype(BF16), v, preferred_element_type=F32)
            cross = jnp.dot(qb, rs.astype(BF16), preferred_element_type=F32) * qd_ref[h]
            kd = (k * kd_ref[h]).astype(BF16)
            r_scr[bi, h] = gc_ref[h] * rs + lax.dot_general(kd, v, (((0,), (0,)), ((), ())),
                                                            preferred_element_type=F32)
            o = inner + cross
            o = o * lax.rsqrt(jnp.mean(o * o, axis=-1, keepdims=True) + NORM_EPS)
            gate = g_ref[rows, vs]
            o_ref[rows, vs] = gate * _sigmoid(gate) * o

    @pl.when(c == pl.num_programs(1) - 1)
    def _():
        rn_ref[...] = r_scr[...]


def _retention(z, r0, tabs, bsz, t, chunk):
    cos, sin, dm, qd, kd, gc = tabs
    nc = t // chunk
    nbr = 4 if nc == 1 else 1
    seq = lambda w, col: pl.BlockSpec((nbr * chunk, w), lambda b, c: (b * nc + c, col // w))
    tab3 = lambda a: pl.BlockSpec(a.shape, lambda b, c: (0, 0, 0))
    st = pl.BlockSpec((nbr, B_HEADS, B_QK_HEAD, B_V_HEAD), lambda b, c: (b, 0, 0, 0))
    return pl.pallas_call(
        _retention_kernel,
        out_shape=[jax.ShapeDtypeStruct((bsz * t, B_V_WIDTH), F32),
                   jax.ShapeDtypeStruct((bsz, B_HEADS, B_QK_HEAD, B_V_HEAD), F32)],
        grid=(bsz // nbr, nc),
        in_specs=[
            seq(B_QK_WIDTH, COL_QB), seq(B_QK_WIDTH, COL_KB), seq(B_V_WIDTH, COL_VB), seq(B_V_WIDTH, COL_GB),
            pl.BlockSpec((chunk, B_QK_HEAD), lambda b, c: (c, 0)),
            pl.BlockSpec((chunk, B_QK_HEAD), lambda b, c: (c, 0)),
            tab3(dm), tab3(qd), tab3(kd), tab3(gc), st,
        ],
        out_specs=[pl.BlockSpec((nbr * chunk, B_V_WIDTH), lambda b, c: (b * nc + c, 0)), st],
        scratch_shapes=[pltpu.VMEM((nbr, B_HEADS, B_QK_HEAD, B_V_HEAD), F32)],
        compiler_params=_cp(("parallel", "arbitrary")),
        name="retention",
    )(z, z, z, z, cos, sin, dm, qd, kd, gc, r0)


def _retention_tables(pos, chunk):
    half = B_QK_HEAD // 2
    inv = jnp.power(ROT_BASE, -jnp.linspace(0.0, 1.0, half, dtype=F32))
    ang = pos[:, None] * inv[None, :]
    cos = jnp.cos(ang)
    sin = jnp.sin(ang)
    cos2 = jnp.concatenate([cos, cos], axis=-1)
    sin2 = jnp.concatenate([-sin, sin], axis=-1)
    log_g = jnp.log(1.0 - jnp.power(2.0, -5.0 - jnp.arange(B_HEADS, dtype=F32)))
    idx = jnp.arange(chunk, dtype=F32)
    diff = idx[:, None] - idx[None, :]
    dmask = jnp.where(diff >= 0, jnp.exp(log_g[:, None, None] * jnp.maximum(diff, 0.0)), 0.0)
    q_dec = jnp.exp(log_g[:, None] * (idx[None, :] + 1.0))
    k_dec = jnp.exp(log_g[:, None] * (chunk - 1.0 - idx[None, :]))
    g_c = jnp.exp(log_g * chunk)
    qd = jnp.broadcast_to(q_dec[:, :, None], (B_HEADS, chunk, B_V_HEAD))
    kd = jnp.broadcast_to(k_dec[:, :, None], (B_HEADS, chunk, B_QK_HEAD))
    gc = jnp.broadcast_to(g_c[:, None, None], (B_HEADS, B_QK_HEAD, B_V_HEAD))
    return cos2, sin2, dmask, qd, kd, gc


def _merge1_kernel(y_ref, r_ref, k_ref, v_ref, g_ref, ob_ref, gma_ref, gmb_ref, lw_ref, lb_ref, rk_ref, wpa_ref,
                   wpb_ref, m_ref, oa_scr, ob_scr):
    @pl.when(pl.program_id(1) == 0)
    def _():
        ones = _seg_ones()
        y = y_ref[...]
        mean = _segsum64(y, ones) * (1.0 / A_HEAD)
        yc = y - mean
        var = _segsum64(yc * yc, ones) * (1.0 / A_HEAD)
        yn = yc * lax.rsqrt(var + A_GN_EPS) * lw_ref[...] + lb_ref[...]
        bonus = _segsum64(r_ref[...] * k_ref[...] * rk_ref[...], ones) * v_ref[...]
        oa_scr[...] = ((yn + bonus) * g_ref[...]).astype(BF16)
        ob_scr[...] = ob_ref[...].astype(BF16)

    pa = jnp.dot(oa_scr[...], wpa_ref[...], preferred_element_type=F32)
    pb = jnp.dot(ob_scr[...], wpb_ref[...], preferred_element_type=F32)
    m_ref[...] = (_sigmoid(gma_ref[...]) * pa + _sigmoid(gmb_ref[...]) * pb).astype(m_ref.dtype)


def _merge1(y, r, k2, v, g, ob, z, pw, tm=512, tn=512):
    n = y.shape[0]
    a_blk = pl.BlockSpec((tm, A_WIDTH), lambda i, j: (i, 0))
    row = pl.BlockSpec((1, A_WIDTH), lambda i, j: (0, 0))
    return pl.pallas_call(
        _merge1_kernel,
        out_shape=jax.ShapeDtypeStruct((n, D_MODEL), BF16),
        grid=(n // tm, D_MODEL // tn),
        in_specs=[
            a_blk, a_blk, a_blk, a_blk, a_blk,
            pl.BlockSpec((tm, B_V_WIDTH), lambda i, j: (i, 0)),
            pl.BlockSpec((tm, tn), lambda i, j: (i, COL_GMA // tn + j)),
            pl.BlockSpec((tm, tn), lambda i, j: (i, COL_GMB // tn + j)),
            row, row, row,
            pl.BlockSpec((A_WIDTH, tn), lambda i, j: (0, j)),
            pl.BlockSpec((B_V_WIDTH, tn), lambda i, j: (0, j)),
        ],
        out_specs=pl.BlockSpec((tm, tn), lambda i, j: (i, j)),
        scratch_shapes=[pltpu.VMEM((tm, A_WIDTH), BF16), pltpu.VMEM((tm, B_V_WIDTH), BF16)],
        compiler_params=_cp(("parallel", "arbitrary")),
        name="merge1",
    )(y, r, k2, v, g, ob, z, z, pw["lnx_w"], pw["lnx_b"], pw["r_k"], pw["wpa"], pw["wpb"])


def _merge2_kernel(m_ref, x_ref, w_ref, g_ref, x1_ref, xn_ref):
    x1 = x_ref[...] + jnp.dot(m_ref[...], w_ref[...], preferred_element_type=F32)
    x1_ref[...] = x1
    ms = jnp.mean(x1 * x1, axis=-1, keepdims=True)
    xn_ref[...] = (x1 * lax.rsqrt(ms + NORM_EPS) * g_ref[...]).astype(BF16)


def _merge2(m, x, w_out, norm2, tm=512):
    n = x.shape[0]
    blk = pl.BlockSpec((tm, D_MODEL), lambda i: (i, 0))
    return pl.pallas_call(
        _merge2_kernel,
        out_shape=[jax.ShapeDtypeStruct((n, D_MODEL), F32), jax.ShapeDtypeStruct((n, D_MODEL), BF16)],
        grid=(n // tm,),
        in_specs=[blk, blk, pl.BlockSpec((D_MODEL, D_MODEL), lambda i: (0, 0)),
                  pl.BlockSpec((1, D_MODEL), lambda i: (0, 0))],
        out_specs=[blk, blk],
        compiler_params=_cp(("parallel",)),
        name="merge2",
    )(m, x, w_out, norm2)


_PAIRS = [(i, j) for i in range(PEER_TOPK) for j in range(PEER_TOPK) if (i + 1) * (j + 1) <= PEER_TOPK]


def _top16_rows(s):
    rows = []
    cur = s
    for i in range(PEER_TOPK):
        m = jnp.max(cur, axis=0, keepdims=True)
        rows.append(m)
        if i + 1 < PEER_TOPK:
            cur = jnp.where(cur == m, -jnp.inf, cur)
    return rows


def _peer_topk_kernel(xn_ref, wq_ref, keys_ref, s1_ref, s2_ref, st_ref, top_scr, cand_scr):
    q = jnp.dot(xn_ref[...], wq_ref[...], preferred_element_type=F32)
    for h in range(PEER_HEADS):
        for p in range(2):
            c0 = (h * 2 + p) * N_KEYS
            qhp = q[:, c0:c0 + N_KEYS].astype(BF16)
            st = lax.dot_general(keys_ref[h, p], qhp, (((1,), (1,)), ((), ())), preferred_element_type=F32)
            (s1_ref if p == 0 else s2_ref)[h] = st
            for i, row in enumerate(_top16_rows(st)):
                top_scr[p, i, h:h + 1, :] = row
    for n, (i, j) in enumerate(_PAIRS):
        cand_scr[n] = top_scr[0, i] + top_scr[1, j]
    m = None
    for it in range(PEER_TOPK):
        m = cand_scr[0]
        for n in range(1, len(_PAIRS)):
            m = jnp.maximum(m, cand_scr[n])
        if it + 1 < PEER_TOPK:
            for n in range(len(_PAIRS)):
                cnd = cand_scr[n]
                cand_scr[n] = jnp.where(cnd == m, -jnp.inf, cnd)
    tau = m
    top = top_scr[0, 0] + top_scr[1, 0]
    zsum = jnp.zeros_like(tau)
    for (i, j) in _PAIRS:
        cnd = top_scr[0, i] + top_scr[1, j]
        zsum = zsum + jnp.where(cnd >= tau, jnp.exp(cnd - top), 0.0)
    st_ref[0] = tau
    st_ref[1] = top_scr[0, 0]
    st_ref[2] = top_scr[1, 0]
    st_ref[3] = 1.0 / zsum


def _peer_topk(xn, wq, keys, tm=512):
    n = xn.shape[0]
    s_shape = jax.ShapeDtypeStruct((PEER_HEADS, N_KEYS, n), F32)
    s_spec = pl.BlockSpec((PEER_HEADS, N_KEYS, tm), lambda i: (0, 0, i))
    return pl.pallas_call(
        _peer_topk_kernel,
        out_shape=[s_shape, s_shape, jax.ShapeDtypeStruct((4, PEER_HEADS, n), F32)],
        grid=(n // tm,),
        in_specs=[pl.BlockSpec((tm, D_MODEL), lambda i: (i, 0)),
                  pl.BlockSpec((D_MODEL, D_MODEL), lambda i: (0, 0)),
                  pl.BlockSpec(keys.shape, lambda i: (0, 0, 0, 0))],
        out_specs=[s_spec, s_spec, pl.BlockSpec((4, PEER_HEADS, tm), lambda i: (0, 0, i))],
        scratch_shapes=[pltpu.VMEM((2, PEER_TOPK, PEER_HEADS, tm), F32),
                        pltpu.VMEM((len(_PAIRS), PEER_HEADS, tm), F32)],
        compiler_params=_cp(("parallel",)),
        name="peer_topk",
    )(xn, wq, keys)


def _peer_dense_kernel(xn_ref, s1_ref, s2_ref, st_ref, u_ref, v_ref, o_ref, acc_scr, b_scr, sc_scr, w_scr, *, ec, tm):
    c = pl.program_id(1)
    nc = pl.num_programs(1) - 2
    n1 = ec // N_KEYS

    @pl.when(c == 0)
    def _():
        acc_scr[...] = jnp.zeros_like(acc_scr)
        sc_scr[...] = jnp.zeros_like(sc_scr)
        w_scr[...] = jnp.zeros_like(w_scr)
        for h in range(PEER_HEADS):
            b_scr[h] = jnp.exp(s2_ref[h] - st_ref[2, pl.ds(h, 1), :]) * st_ref[3, pl.ds(h, 1), :]

    slot = c % 2
    sc_scr[slot] = lax.dot_general(u_ref[...], xn_ref[...], (((1,), (1,)), ((), ())), preferred_element_type=F32)

    i1_base = pl.multiple_of(jnp.clip(c - 1, 0, nc - 1) * n1, SUBLANE)
    for ii in range(n1):
        for tcol in range(tm // LANE):
            cs = slice(tcol * LANE, (tcol + 1) * LANE)
            g = jnp.zeros((N_KEYS, LANE), F32)
            for h in range(PEER_HEADS):
                s1row = s1_ref[h, pl.ds(i1_base, n1), cs][ii:ii + 1, :]
                theta = st_ref[0, pl.ds(h, 1), cs] - s1row
                arow = jnp.exp(s1row - st_ref[1, pl.ds(h, 1), cs])
                g = g + jnp.where(s2_ref[h, :, cs] >= theta, arow * b_scr[h, :, cs], 0.0)
            x = sc_scr[1 - slot, ii * N_KEYS:(ii + 1) * N_KEYS, cs]
            gelu = 0.5 * x * (1.0 + lax.erf(x * np.float32(np.sqrt(0.5))))
            w_scr[1 - slot, ii * N_KEYS:(ii + 1) * N_KEYS, cs] = (g * gelu).astype(BF16)

    acc_scr[...] += lax.dot_general(v_ref[...], w_scr[slot], (((0,), (0,)), ((), ())),
                                    preferred_element_type=F32)

    @pl.when(c == pl.num_programs(1) - 1)
    def _():
        o_ref[...] = acc_scr[...].T


def _peer_dense(xn_bf, s1, s2, stats, u_bf, v_bf, tm=512, ec=1024):
    assert ec // N_KEYS == SUBLANE
    n = xn_bf.shape[0]
    nc = u_bf.shape[0] // ec
    s_spec = pl.BlockSpec((PEER_HEADS, N_KEYS, tm), lambda i, c: (0, 0, i))
    return pl.pallas_call(
        functools.partial(_peer_dense_kernel, ec=ec, tm=tm),
        out_shape=jax.ShapeDtypeStruct((n, D_MODEL), F32),
        grid=(n // tm, nc + 2),
        in_specs=[
            pl.BlockSpec((tm, D_MODEL), lambda i, c: (i, 0)),
            s_spec, s_spec,
            pl.BlockSpec((4, PEER_HEADS, tm), lambda i, c: (0, 0, i)),
            pl.BlockSpec((ec, D_MODEL), lambda i, c: (jnp.minimum(c, nc - 1), 0)),
            pl.BlockSpec((ec, D_MODEL), lambda i, c: (jnp.clip(c - 2, 0, nc - 1), 0)),
        ],
        out_specs=pl.BlockSpec((tm, D_MODEL), lambda i, c: (i, 0), pipeline_mode=pl.Buffered(1)),
        scratch_shapes=[pltpu.VMEM((D_MODEL, tm), F32), pltpu.VMEM((PEER_HEADS, N_KEYS, tm), F32),
                        pltpu.VMEM((2, ec, tm), F32), pltpu.VMEM((2, ec, tm), BF16)],
        compiler_params=_cp(("parallel", "arbitrary")),
        name="peer_dense",
    )(xn_bf, s1, s2, stats, u_bf, v_bf)


def _final_kernel(x1_ref, p_ref, g_ref, y_ref):
    x = x1_ref[...] + p_ref[...]
    ms = jnp.mean(x * x, axis=-1, keepdims=True)
    y_ref[...] = x * lax.rsqrt(ms + NORM_EPS) * g_ref[...]


def _final(x1, p, g, tm=512):
    n = x1.shape[0]
    blk = pl.BlockSpec((tm, D_MODEL), lambda i: (i, 0))
    return pl.pallas_call(
        _final_kernel,
        out_shape=jax.ShapeDtypeStruct((n, D_MODEL), F32),
        grid=(n // tm,),
        in_specs=[blk, blk, pl.BlockSpec((1, D_MODEL), lambda i: (0, 0))],
        out_specs=blk,
        compiler_params=_cp(("parallel",)),
        name="final",
    )(x1, p, g)


def _pad_cols(a, w):
    return jnp.pad(a, ((0, 0), (0, w - a.shape[1])))


def _za_layout(a):
    r3 = a[:, :3 * A_WIDTH]
    wl = _pad_cols(a[:, 3 * A_WIDTH:3 * A_WIDTH + A_DECAY_LORA], COL_AL - COL_WL)
    al = _pad_cols(a[:, 3 * A_WIDTH + A_DECAY_LORA:3 * A_WIDTH + A_DECAY_LORA + A_ICL_LORA], COL_GL - COL_AL)
    gl = _pad_cols(a[:, 3 * A_WIDTH + A_DECAY_LORA + A_ICL_LORA:SHIFT_W], ZA_W - COL_GL)
    return jnp.concatenate([r3, wl, al, gl], axis=1)


def _za_unlayout(a):
    return jnp.concatenate([a[:, :3 * A_WIDTH], a[:, COL_WL:COL_WL + A_DECAY_LORA], a[:, COL_AL:COL_AL + A_ICL_LORA],
                            a[:, COL_GL:COL_GL + A_GATE_LORA]], axis=1)


def _pad_rows(a, n):
    return jnp.pad(a, ((0, n - a.shape[0]), (0, 0)))


def _prepare_weights(lw):
    (norm1, w_in, mu_shift, w0, w_w2, a0, w_a2, w_g2, k_k, k_a, r_k, lnx_w, lnx_b, w_pa, w_pb, w_out, norm2,
     peer_wq, peer_keys, peer_u, peer_v) = lw
    row = lambda a: a[None, :]
    w_in_bf = w_in.astype(BF16)
    wcat = jnp.concatenate([
        _za_layout(w_in_bf[:, :SHIFT_W]), jnp.zeros((D_MODEL, COL_QB - ZA_W), BF16), w_in_bf[:, SHIFT_W:]], axis=1)
    return dict(
        norm1=row(norm1), wcat=wcat, mu=_za_layout(row(mu_shift)), w0=row(w0),
        ww2=_pad_rows(w_w2, COL_AL - COL_WL).astype(BF16), a0=row(a0),
        wa2=_pad_rows(w_a2, COL_GL - COL_AL).astype(BF16), wg2=_pad_rows(w_g2, ZA_W - COL_GL).astype(BF16),
        k_k=row(k_k), k_a=row(k_a), r_k=row(r_k), lnx_w=row(lnx_w), lnx_b=row(lnx_b),
        wpa=w_pa.astype(BF16), wpb=w_pb.astype(BF16), wout=w_out.astype(BF16), norm2=row(norm2),
        wq=peer_wq.astype(BF16), keys=peer_keys.astype(BF16), u=peer_u.astype(BF16), v=peer_v.astype(BF16))


def _group_layer(x, shift0, wkv0, ret0, pos, pw, scan_tc):
    bsz, t, d = x.shape
    n = bsz * t
    z = _inproj(x.reshape(n, d), pw["norm1"], pw["wcat"])
    shift_new = _za_unlayout(z.reshape(bsz, t, W_TOT)[:, -1, :ZA_W])

    r, w, k2, v, kk, nkka, g = _rwkv_prep(z, _za_layout(shift0), bsz, t, pw)
    y, wkv_new = _rwkv_scan([a.reshape(bsz, t, A_WIDTH) for a in (r, w, k2, v, kk, nkka)], wkv0, nb=4, tc=scan_tc)

    chunk = RET_CHUNK if t % RET_CHUNK == 0 else t
    ob, ret_new = _retention(z, ret0, _retention_tables(pos, chunk), bsz, t, chunk)

    m = _merge1(y.reshape(n, A_WIDTH), r, k2, v, g, ob, z, pw)
    x1, xn = _merge2(m, x.reshape(n, d), pw["wout"], pw["norm2"])
    s1, s2, stats = _peer_topk(xn, pw["wq"], pw["keys"])
    peer = _peer_dense(xn, s1, s2, stats, pw["u"], pw["v"])
    return x1, peer, shift_new, wkv_new, ret_new


def _trunk(x, shift0, wkv0, ret0, pos, pws, norm_f, scan_tc):
    bsz, t, d = x.shape
    shifts, wkvs, rets = [], [], []
    x1 = peer = None
    for layer, pw in enumerate(pws):
        if layer > 0:
            x = (x1 + peer).reshape(bsz, t, d)
        x1, peer, s_new, wkv_new, ret_new = _group_layer(x, shift0[layer], wkv0[layer], ret0[layer], pos, pw, scan_tc)
        shifts.append(s_new)
        wkvs.append(wkv_new)
        rets.append(ret_new)
    y = _final(x1, peer, norm_f[None, :]).reshape(bsz, t, d)
    return y, jnp.stack(shifts), jnp.stack(wkvs), jnp.stack(rets)


def kernel(x_prompt, x_sample, state_shift, state_wkv, state_ret, norm1, w_in, mu_shift, w0, w_w2, a0, w_a2, w_g2, k_k, k_a, r_k, lnx_w, lnx_b, w_pa, w_pb, w_out, norm2, peer_wq, peer_keys, peer_u, peer_v, norm_f):
    weights = (norm1, w_in, mu_shift, w0, w_w2, a0, w_a2, w_g2, k_k, k_a, r_k, lnx_w, lnx_b, w_pa, w_pb, w_out,
               norm2, peer_wq, peer_keys, peer_u, peer_v)
    depth = norm1.shape[0]
    pws = [_prepare_weights([w[layer] for w in weights]) for layer in range(depth)]
    bp, tp, _ = x_prompt.shape
    ts = x_sample.shape[1]
    shift0 = jnp.zeros((depth, bp, SHIFT_W), F32)
    wkv0 = jnp.zeros((depth, bp, A_HEADS, A_HEAD, A_HEAD), F32)
    ret0 = jnp.zeros((depth, bp, B_HEADS, B_QK_HEAD, B_V_HEAD), F32)
    y_p, shift_p, wkv_p, ret_p = _trunk(x_prompt, shift0, wkv0, ret0, jnp.arange(tp, dtype=F32), pws, norm_f,
                                        scan_tc=128)
    y_s, shift_s, wkv_s, ret_s = _trunk(x_sample, state_shift, state_wkv, state_ret,
                                        PAST_LEN + jnp.arange(ts, dtype=F32), pws, norm_f, scan_tc=ts)
    return (y_p, y_s, shift_p, wkv_p, ret_p, shift_s, wkv_s, ret_s)
```

```python
import functools

import jax
import jax.numpy as jnp
import numpy as np
from jax import lax
from jax.experimental import pallas as pl
from jax.experimental.pallas import tpu as pltpu

F32 = jnp.float32
BF16 = jnp.bfloat16

D_MODEL = 2048
A_HEAD = 64
A_WIDTH = 1024
A_HEADS = 16
A_DECAY_LORA = 64
A_ICL_LORA = 64
A_GATE_LORA = 160
A_GN_EPS = 64e-5
B_HEADS = 8
B_QK_HEAD = 128
B_V_HEAD = 256
B_QK_WIDTH = 1024
B_V_WIDTH = 2048
RET_CHUNK = 128
ROT_BASE = 10000.0
N_KEYS = 128
N_EXPERTS = N_KEYS * N_KEYS
PEER_HEADS = 8
PEER_TOPK = 16
NORM_EPS = 1e-6
PAST_LEN = 16384
SHIFT_W = 3 * A_WIDTH + A_DECAY_LORA + A_ICL_LORA + A_GATE_LORA

LANE = 128
SUBLANE = 8
SEG_W = 256
ZA_W = 3584
COL_WL = 3072
COL_AL = 3200
COL_GL = 3328
COL_QB = 4096
COL_KB = 5120
COL_VB = 6144
COL_GB = 8192
COL_GMA = 10240
COL_GMB = 12288
W_TOT = 14336
VMEM_LIMIT = 56 * 1024 * 1024


def _cp(sem):
    return pltpu.CompilerParams(dimension_semantics=sem, vmem_limit_bytes=VMEM_LIMIT)


def _sigmoid(x):
    return 1.0 / (1.0 + jnp.exp(-x))


def _inproj_kernel(x_ref, g_ref, w_ref, o_ref, hn_ref):
    @pl.when(pl.program_id(1) == 0)
    def _():
        x = x_ref[...]
        ms = jnp.mean(x * x, axis=-1, keepdims=True)
        hn_ref[...] = (x * lax.rsqrt(ms + NORM_EPS) * g_ref[...]).astype(BF16)

    o_ref[...] = jnp.dot(hn_ref[...], w_ref[...], preferred_element_type=F32)


def _inproj(x, g, w, tm=1024, tn=1024):
    n, d = x.shape
    wt = w.shape[1]
    return pl.pallas_call(
        _inproj_kernel,
        out_shape=jax.ShapeDtypeStruct((n, wt), F32),
        grid=(n // tm, wt // tn),
        in_specs=[
            pl.BlockSpec((tm, d), lambda i, j: (i, 0)),
            pl.BlockSpec((1, d), lambda i, j: (0, 0)),
            pl.BlockSpec((d, tn), lambda i, j: (0, j)),
        ],
        out_specs=pl.BlockSpec((tm, tn), lambda i, j: (i, j)),
        scratch_shapes=[pltpu.VMEM((tm, d), BF16)],
        compiler_params=_cp(("parallel", "arbitrary")),
        name="inproj",
    )(x, g, w)


def _seg_ones():
    r = lax.broadcasted_iota(jnp.int32, (SEG_W, SEG_W), 0) // A_HEAD
    c = lax.broadcasted_iota(jnp.int32, (SEG_W, SEG_W), 1) // A_HEAD
    return (r == c).astype(BF16)


def _segsum64(x, ones):
    def top8(v):
        bits = lax.bitcast_convert_type(v, jnp.uint32) & jnp.uint32(0xFFFF0000)
        return lax.bitcast_convert_type(bits, F32)

    hi = top8(x)
    r1 = x - hi
    mid = top8(r1)
    lo = r1 - mid
    hi, mid, lo = hi.astype(BF16), mid.astype(BF16), lo.astype(BF16)
    parts = []
    for j in range(x.shape[1] // SEG_W):
        sl = slice(j * SEG_W, (j + 1) * SEG_W)
        parts.append(jnp.dot(hi[:, sl], ones, preferred_element_type=F32)
                     + jnp.dot(mid[:, sl], ones, preferred_element_type=F32)
                     + jnp.dot(lo[:, sl], ones, preferred_element_type=F32))
    return jnp.concatenate(parts, axis=1)


def _rwkv_prep_kernel(z_ref, first_ref, mu_ref, w0_ref, ww2_ref, a0_ref, wa2_ref, wg2_ref, kk_ref_, ka_ref,
                      r_o, w_o, k_o, v_o, kk_o, nkka_o, g_o, carry_scr, *, tm, seq_len):
    z = z_ref[...]
    rolled = pltpu.roll(z, 1, 0)
    rows = lax.broadcasted_iota(jnp.int32, z.shape, 0)
    if seq_len >= tm:
        @pl.when(pl.program_id(1) == 0)
        def _():
            carry_scr[0:1, :] = first_ref[0]

        zprev = jnp.where(rows == 0, carry_scr[0:1, :], rolled)
        carry_scr[0:1, :] = z[tm - 1:tm, :]
    else:
        zprev = jnp.where(rows % seq_len == 0, first_ref[...], rolled)
    zs = z + (zprev - z) * mu_ref[...]
    r = zs[:, 0:A_WIDTH]
    k = zs[:, A_WIDTH:2 * A_WIDTH]
    v = zs[:, 2 * A_WIDTH:3 * A_WIDTH]
    wl = zs[:, COL_WL:COL_AL]
    al = zs[:, COL_AL:COL_GL]
    gl = zs[:, COL_GL:ZA_W]
    wpre = w0_ref[...] + jnp.dot(jnp.tanh(wl).astype(BF16), ww2_ref[...], preferred_element_type=F32)
    nw = -wpre
    softplus = jnp.maximum(nw, 0.0) + jnp.log1p(jnp.exp(-jnp.abs(nw)))
    wlog = -softplus - 0.5
    decay = jnp.exp(-jnp.exp(wlog))
    a = _sigmoid(a0_ref[...] + jnp.dot(al.astype(BF16), wa2_ref[...], preferred_element_type=F32))
    g = jnp.dot(_sigmoid(gl).astype(BF16), wg2_ref[...], preferred_element_type=F32)
    kk = k * kk_ref_[...]
    ss = _segsum64(kk * kk, _seg_ones())
    kk = kk / jnp.maximum(jnp.sqrt(ss), 1e-12)
    k2 = k * (1.0 + (a - 1.0) * ka_ref[...])
    r_o[...] = r
    w_o[...] = decay
    k_o[...] = k2
    v_o[...] = v
    kk_o[...] = kk
    nkka_o[...] = -(kk * a)
    g_o[...] = g


def _rwkv_prep(z, shift0_l, bsz, t, pw, tm=256):
    n = bsz * t
    nt = max(t // tm, 1)
    if t >= tm:
        first = shift0_l[:, None, :]
        first_spec = pl.BlockSpec((1, 1, ZA_W), lambda i, j: (i, 0, 0))
    else:
        first = jnp.pad(shift0_l[:, None, :], ((0, 0), (0, t - 1), (0, 0))).reshape(n, ZA_W)
        first_spec = pl.BlockSpec((tm, ZA_W), lambda i, j: (i * nt + j, 0))
    row = lambda w: pl.BlockSpec((1, w), lambda i, j: (0, 0))
    full = lambda a: pl.BlockSpec(a.shape, lambda i, j: (0, 0))
    out = jax.ShapeDtypeStruct((n, A_WIDTH), F32)
    return pl.pallas_call(
        functools.partial(_rwkv_prep_kernel, tm=tm, seq_len=t),
        out_shape=[out] * 7,
        grid=(n // (tm * nt), nt),
        in_specs=[
            pl.BlockSpec((tm, ZA_W), lambda i, j: (i * nt + j, 0)),
            first_spec,
            row(ZA_W), row(A_WIDTH), full(pw["ww2"]), row(A_WIDTH), full(pw["wa2"]), full(pw["wg2"]),
            row(A_WIDTH), row(A_WIDTH),
        ],
        out_specs=[pl.BlockSpec((tm, A_WIDTH), lambda i, j: (i * nt + j, 0))] * 7,
        scratch_shapes=[pltpu.VMEM((SUBLANE, ZA_W), F32)],
        compiler_params=_cp(("parallel", "arbitrary")),
        name="rwkv_prep",
    )(z, first, pw["mu"], pw["w0"], pw["ww2"], pw["a0"], pw["wa2"], pw["wg2"], pw["k_k"], pw["k_a"])


def _rwkv_scan_kernel(r_ref, w_ref, k_ref, v_ref, kk_ref, nkka_ref, s0_ref, m4_ref, y_ref, sf_ref, s_scr,
                      xa_scr, xb_scr, y8_scr, rows_scr, *, nb, tc):
    c = pl.program_id(1)

    @pl.when(c == 0)
    def _():
        s_scr[...] = s0_ref[...]

    lane = lax.broadcasted_iota(jnp.int32, (A_HEAD, LANE), 1)
    sub = lax.broadcasted_iota(jnp.int32, (A_HEAD, LANE), 0)
    lo = lane < A_HEAD
    diag = (lane & (A_HEAD - 1)) == sub
    diag_lo = jnp.logical_and(diag, lo)
    diag_hi = jnp.logical_and(diag, jnp.logical_not(lo))
    nq = A_WIDTH // SEG_W
    npair = A_WIDTH // LANE
    ngroup = xa_scr.shape[0]
    gb = nb // ngroup

    seq_refs = (r_ref, w_ref, k_ref, v_ref, kk_ref, nkka_ref)
    seq_ids = [id(ref) for ref in seq_refs]

    def step(t8, carry):
        base = pl.multiple_of(t8 * SUBLANE, SUBLANE)
        for a, ref in enumerate(seq_refs):
            for b in range(nb):
                rows_scr[a, b] = ref[b, pl.ds(base, SUBLANE), :]
        for i in range(SUBLANE):
            row = lambda ref, b, lanes: rows_scr[seq_ids.index(id(ref)), b, i:i + 1, lanes]

            def phase_a(g):
                for bl in range(gb):
                    b = g * gb + bl
                    for j in range(nq):
                        q = slice(j * SEG_W, (j + 1) * SEG_W)
                        r0 = (bl * nq + j) * A_HEAD
                        xa_scr[g, r0:r0 + A_HEAD, :] = (s_scr[b, :, q] * row(kk_ref, b, q)).astype(BF16)
                return jnp.dot(xa_scr[g], m4_ref[...], preferred_element_type=F32)

            def phase_b(g, sa_all):
                for bl in range(gb):
                    b = g * gb + bl
                    for j in range(npair):
                        sl = slice(j * LANE, (j + 1) * LANE)
                        r0 = (bl * nq + j // 2) * A_HEAD
                        half = slice((j % 2) * LANE, (j % 2 + 1) * LANE)
                        vrow = row(v_ref, b, sl)
                        v_lo = jnp.sum(jnp.where(diag_lo, vrow, 0.0), axis=1, keepdims=True)
                        v_hi = jnp.sum(jnp.where(diag_hi, vrow, 0.0), axis=1, keepdims=True)
                        vb = jnp.where(lo, v_lo, v_hi)
                        s = (s_scr[b, :, sl] * row(w_ref, b, sl)
                             + sa_all[r0:r0 + A_HEAD, half] * row(nkka_ref, b, sl) + vb * row(k_ref, b, sl))
                        s_scr[b, :, sl] = s
                        xb_scr[g, r0:r0 + A_HEAD, half] = (s * row(r_ref, b, sl)).astype(BF16)
                return jnp.dot(xb_scr[g], m4_ref[...], preferred_element_type=F32)

            def phase_c(g, y_all):
                for bl in range(gb):
                    b = g * gb + bl
                    for j in range(npair):
                        sl = slice(j * LANE, (j + 1) * LANE)
                        r0 = (bl * nq + j // 2) * A_HEAD
                        half = slice((j % 2) * LANE, (j % 2 + 1) * LANE)
                        yb = y_all[r0:r0 + A_HEAD, half]
                        y8_scr[b, i:i + 1, sl] = jnp.sum(jnp.where(diag, yb, 0.0), axis=0, keepdims=True)

            sa = [phase_a(g) for g in range(ngroup)]
            ya = [phase_b(g, sa[g]) for g in range(ngroup)]
            for g in range(ngroup):
                phase_c(g, ya[g])
        for b in range(nb):
            y_ref[b, pl.ds(base, SUBLANE), :] = y8_scr[b]
        return carry

    lax.fori_loop(0, tc // SUBLANE, step, 0)

    @pl.when(c == pl.num_programs(1) - 1)
    def _():
        sf_ref[...] = s_scr[...]


def _rwkv_scan(seqs, s0, nb, tc):
    bsz, t, _ = seqs[0].shape
    seq = pl.BlockSpec((nb, tc, A_WIDTH), lambda i, c: (i, c, 0))
    st = pl.BlockSpec((nb, A_HEAD, A_WIDTH), lambda i, c: (i, 0, 0))
    seg = jnp.arange(SEG_W, dtype=jnp.int32) // A_HEAD
    m4 = (seg[:, None] == seg[None, :]).astype(BF16)
    ngroup = 2
    xrows = (nb // ngroup) * (A_WIDTH // SEG_W) * A_HEAD
    x_scr = pltpu.VMEM((ngroup, xrows, SEG_W), BF16)
    return pl.pallas_call(
        functools.partial(_rwkv_scan_kernel, nb=nb, tc=tc),
        out_shape=[jax.ShapeDtypeStruct((bsz, t, A_WIDTH), F32), jax.ShapeDtypeStruct((bsz, A_HEAD, A_WIDTH), F32)],
        grid=(bsz // nb, t // tc),
        in_specs=[seq] * 6 + [st, pl.BlockSpec((SEG_W, SEG_W), lambda i, c: (0, 0))],
        out_specs=[seq, st],
        scratch_shapes=[pltpu.VMEM((nb, A_HEAD, A_WIDTH), F32), x_scr, x_scr,
                        pltpu.VMEM((nb, SUBLANE, A_WIDTH), F32), pltpu.VMEM((6, nb, SUBLANE, A_WIDTH), F32)],
        compiler_params=_cp(("parallel", "arbitrary")),
        name="rwkv_scan",
    )(*seqs, s0, m4)


def _retention_kernel(q_ref, k_ref, v_ref, g_ref, cos_ref, sin_ref, dm_ref, qd_ref, kd_ref, gc_ref, r0_ref,
                      o_ref, rn_ref, r_scr):
    c = pl.program_id(1)
    nbr = r_scr.shape[0]
    chunk = cos_ref.shape[0]

    @pl.when(c == 0)
    def _():
        r_scr[...] = r0_ref[...]

    cos = cos_ref[...]
    sin = sin_ref[...]

    def rot(x):
        return x * cos + pltpu.roll(x, B_QK_HEAD // 2, 1) * sin

    for bi in range(nbr):
        rows = slice(bi * chunk, (bi + 1) * chunk)
        for h in range(B_HEADS):
            qs = slice(h * B_QK_HEAD, (h + 1) * B_QK_HEAD)
            vs = slice(h * B_V_HEAD, (h + 1) * B_V_HEAD)
            q = rot(q_ref[rows, qs])
            k = rot(k_ref[rows, qs]) * (B_QK_HEAD ** -0.5)
            v = v_ref[rows, vs].astype(BF16)
            rs = r_scr[bi, h]
            qb = q.astype(BF16)
            s = lax.dot_general(qb, k.astype(BF16), (((1,), (1,)), ((), ())),
                                preferred_element_type=F32) * dm_ref[h]
            inner = jnp.dot(s.astype(BF16), v, preferred_element_type=F32)
            cross = jnp.dot(qb, rs.astype(BF16), preferred_element_type=F32) * qd_ref[h]
            kd = (k * kd_ref[h]).astype(BF16)
            r_scr[bi, h] = gc_ref[h] * rs + lax.dot_general(kd, v, (((0,), (0,)), ((), ())),
                                                            preferred_element_type=F32)
            o = inner + cross
            o = o * lax.rsqrt(jnp.mean(o * o, axis=-1, keepdims=True) + NORM_EPS)
            gate = g_ref[rows, vs]
            o_ref[rows, vs] = gate * _sigmoid(gate) * o

    @pl.when(c == pl.num_programs(1) - 1)
    def _():
        rn_ref[...] = r_scr[...]


def _retention(z, r0, tabs, bsz, t, chunk):
    cos, sin, dm, qd, kd, gc = tabs
    nc = t // chunk
    nbr = 4 if nc == 1 else 1
    seq = lambda w, col: pl.BlockSpec((nbr * chunk, w), lambda b, c: (b * nc + c, col // w))
    tab3 = lambda a: pl.BlockSpec(a.shape, lambda b, c: (0, 0, 0))
    st = pl.BlockSpec((nbr, B_HEADS, B_QK_HEAD, B_V_HEAD), lambda b, c: (b, 0, 0, 0))
    return pl.pallas_call(
        _retention_kernel,
        out_shape=[jax.ShapeDtypeStruct((bsz * t, B_V_WIDTH), F32),
                   jax.ShapeDtypeStruct((bsz, B_HEADS, B_QK_HEAD, B_V_HEAD), F32)],
        grid=(bsz // nbr, nc),
        in_specs=[
            seq(B_QK_WIDTH, COL_QB), seq(B_QK_WIDTH, COL_KB), seq(B_V_WIDTH, COL_VB), seq(B_V_WIDTH, COL_GB),
            pl.BlockSpec((chunk, B_QK_HEAD), lambda b, c: (c, 0)),
            pl.BlockSpec((chunk, B_QK_HEAD), lambda b, c: (c, 0)),
            tab3(dm), tab3(qd), tab3(kd), tab3(gc), st,
        ],
        out_specs=[pl.BlockSpec((nbr * chunk, B_V_WIDTH), lambda b, c: (b * nc + c, 0)), st],
        scratch_shapes=[pltpu.VMEM((nbr, B_HEADS, B_QK_HEAD, B_V_HEAD), F32)],
        compiler_params=_cp(("parallel", "arbitrary")),
        name="retention",
    )(z, z, z, z, cos, sin, dm, qd, kd, gc, r0)


def _retention_tables(pos, chunk):
    half = B_QK_HEAD // 2
    inv = jnp.power(ROT_BASE, -jnp.linspace(0.0, 1.0, half, dtype=F32))
    ang = pos[:, None] * inv[None, :]
    cos = jnp.cos(ang)
    sin = jnp.sin(ang)
    cos2 = jnp.concatenate([cos, cos], axis=-1)
    sin2 = jnp.concatenate([-sin, sin], axis=-1)
    log_g = jnp.log(1.0 - jnp.power(2.0, -5.0 - jnp.arange(B_HEADS, dtype=F32)))
    idx = jnp.arange(chunk, dtype=F32)
    diff = idx[:, None] - idx[None, :]
    dmask = jnp.where(diff >= 0, jnp.exp(log_g[:, None, None] * jnp.maximum(diff, 0.0)), 0.0)
    q_dec = jnp.exp(log_g[:, None] * (idx[None, :] + 1.0))
    k_dec = jnp.exp(log_g[:, None] * (chunk - 1.0 - idx[None, :]))
    g_c = jnp.exp(log_g * chunk)
    qd = jnp.broadcast_to(q_dec[:, :, None], (B_HEADS, chunk, B_V_HEAD))
    kd = jnp.broadcast_to(k_dec[:, :, None], (B_HEADS, chunk, B_QK_HEAD))
    gc = jnp.broadcast_to(g_c[:, None, None], (B_HEADS, B_QK_HEAD, B_V_HEAD))
    return cos2, sin2, dmask, qd, kd, gc


def _merge1_kernel(y_ref, r_ref, k_ref, v_ref, g_ref, ob_ref, gma_ref, gmb_ref, lw_ref, lb_ref, rk_ref, wpa_ref,
                   wpb_ref, m_ref, oa_scr, ob_scr):
    @pl.when(pl.program_id(1) == 0)
    def _():
        ones = _seg_ones()
        y = y_ref[...]
        mean = _segsum64(y, ones) * (1.0 / A_HEAD)
        yc = y - mean
        var = _segsum64(yc * yc, ones) * (1.0 / A_HEAD)
        yn = yc * lax.rsqrt(var + A_GN_EPS) * lw_ref[...] + lb_ref[...]
        bonus = _segsum64(r_ref[...] * k_ref[...] * rk_ref[...], ones) * v_ref[...]
        oa_scr[...] = ((yn + bonus) * g_ref[...]).astype(BF16)
        ob_scr[...] = ob_ref[...].astype(BF16)

    pa = jnp.dot(oa_scr[...], wpa_ref[...], preferred_element_type=F32)
    pb = jnp.dot(ob_scr[...], wpb_ref[...], preferred_element_type=F32)
    m_ref[...] = (_sigmoid(gma_ref[...]) * pa + _sigmoid(gmb_ref[...]) * pb).astype(m_ref.dtype)


def _merge1(y, r, k2, v, g, ob, z, pw, tm=512, tn=512):
    n = y.shape[0]
    a_blk = pl.BlockSpec((tm, A_WIDTH), lambda i, j: (i, 0))
    row = pl.BlockSpec((1, A_WIDTH), lambda i, j: (0, 0))
    return pl.pallas_call(
        _merge1_kernel,
        out_shape=jax.ShapeDtypeStruct((n, D_MODEL), BF16),
        grid=(n // tm, D_MODEL // tn),
        in_specs=[
            a_blk, a_blk, a_blk, a_blk, a_blk,
            pl.BlockSpec((tm, B_V_WIDTH), lambda i, j: (i, 0)),
            pl.BlockSpec((tm, tn), lambda i, j: (i, COL_GMA // tn + j)),
            pl.BlockSpec((tm, tn), lambda i, j: (i, COL_GMB // tn + j)),
            row, row, row,
            pl.BlockSpec((A_WIDTH, tn), lambda i, j: (0, j)),
            pl.BlockSpec((B_V_WIDTH, tn), lambda i, j: (0, j)),
        ],
        out_specs=pl.BlockSpec((tm, tn), lambda i, j: (i, j)),
        scratch_shapes=[pltpu.VMEM((tm, A_WIDTH), BF16), pltpu.VMEM((tm, B_V_WIDTH), BF16)],
        compiler_params=_cp(("parallel", "arbitrary")),
        name="merge1",
    )(y, r, k2, v, g, ob, z, z, pw["lnx_w"], pw["lnx_b"], pw["r_k"], pw["wpa"], pw["wpb"])


def _merge2_kernel(m_ref, x_ref, w_ref, g_ref, x1_ref, xn_ref):
    x1 = x_ref[...] + jnp.dot(m_ref[...], w_ref[...], preferred_element_type=F32)
    x1_ref[...] = x1
    ms = jnp.mean(x1 * x1, axis=-1, keepdims=True)
    xn_ref[...] = (x1 * lax.rsqrt(ms + NORM_EPS) * g_ref[...]).astype(BF16)


def _merge2(m, x, w_out, norm2, tm=512):
    n = x.shape[0]
    blk = pl.BlockSpec((tm, D_MODEL), lambda i: (i, 0))
    return pl.pallas_call(
        _merge2_kernel,
        out_shape=[jax.ShapeDtypeStruct((n, D_MODEL), F32), jax.ShapeDtypeStruct((n, D_MODEL), BF16)],
        grid=(n // tm,),
        in_specs=[blk, blk, pl.BlockSpec((D_MODEL, D_MODEL), lambda i: (0, 0)),
                  pl.BlockSpec((1, D_MODEL), lambda i: (0, 0))],
        out_specs=[blk, blk],
        compiler_params=_cp(("parallel",)),
        name="merge2",
    )(m, x, w_out, norm2)


_PAIRS = [(i, j) for i in range(PEER_TOPK) for j in range(PEER_TOPK) if (i + 1) * (j + 1) <= PEER_TOPK]


def _top16_rows(s):
    rows = []
    cur = s
    for i in range(PEER_TOPK):
        m = jnp.max(cur, axis=0, keepdims=True)
        rows.append(m)
        if i + 1 < PEER_TOPK:
            cur = jnp.where(cur == m, -jnp.inf, cur)
    return rows


def _peer_topk_kernel(xn_ref, wq_ref, keys_ref, s1_ref, s2_ref, st_ref, top_scr, cand_scr):
    q = jnp.dot(xn_ref[...], wq_ref[...], preferred_element_type=F32)
    for h in range(PEER_HEADS):
        for p in range(2):
            c0 = (h * 2 + p) * N_KEYS
            qhp = q[:, c0:c0 + N_KEYS].astype(BF16)
            st = lax.dot_general(keys_ref[h, p], qhp, (((1,), (1,)), ((), ())), preferred_element_type=F32)
            (s1_ref if p == 0 else s2_ref)[h] = st
            for i, row in enumerate(_top16_rows(st)):
                top_scr[p, i, h:h + 1, :] = row
    for n, (i, j) in enumerate(_PAIRS):
        cand_scr[n] = top_scr[0, i] + top_scr[1, j]
    m = None
    for it in range(PEER_TOPK):
        m = cand_scr[0]
        for n in range(1, len(_PAIRS)):
            m = jnp.maximum(m, cand_scr[n])
        if it + 1 < PEER_TOPK:
            for n in range(len(_PAIRS)):
                cnd = cand_scr[n]
                cand_scr[n] = jnp.where(cnd == m, -jnp.inf, cnd)
    tau = m
    top = top_scr[0, 0] + top_scr[1, 0]
    zsum = jnp.zeros_like(tau)
    for (i, j) in _PAIRS:
        cnd = top_scr[0, i] + top_scr[1, j]
        zsum = zsum + jnp.where(cnd >= tau, jnp.exp(cnd - top), 0.0)
    st_ref[0] = tau
    st_ref[1] = top_scr[0, 0]
    st_ref[2] = top_scr[1, 0]
    st_ref[3] = 1.0 / zsum


def _peer_topk(xn, wq, keys, tm=512):
    n = xn.shape[0]
    s_shape = jax.ShapeDtypeStruct((PEER_HEADS, N_KEYS, n), F32)
    s_spec = pl.BlockSpec((PEER_HEADS, N_KEYS, tm), lambda i: (0, 0, i))
    return pl.pallas_call(
        _peer_topk_kernel,
        out_shape=[s_shape, s_shape, jax.ShapeDtypeStruct((4, PEER_HEADS, n), F32)],
        grid=(n // tm,),
        in_specs=[pl.BlockSpec((tm, D_MODEL), lambda i: (i, 0)),
                  pl.BlockSpec((D_MODEL, D_MODEL), lambda i: (0, 0)),
                  pl.BlockSpec(keys.shape, lambda i: (0, 0, 0, 0))],
        out_specs=[s_spec, s_spec, pl.BlockSpec((4, PEER_HEADS, tm), lambda i: (0, 0, i))],
        scratch_shapes=[pltpu.VMEM((2, PEER_TOPK, PEER_HEADS, tm), F32),
                        pltpu.VMEM((len(_PAIRS), PEER_HEADS, tm), F32)],
        compiler_params=_cp(("parallel",)),
        name="peer_topk",
    )(xn, wq, keys)


def _peer_dense_kernel(xn_ref, s1_ref, s2_ref, st_ref, u_ref, v_ref, o_ref, acc_scr, b_scr, sc_scr, w_scr, *, ec, tm):
    c = pl.program_id(1)
    nc = pl.num_programs(1) - 2
    n1 = ec // N_KEYS

    @pl.when(c == 0)
    def _():
        acc_scr[...] = jnp.zeros_like(acc_scr)
        sc_scr[...] = jnp.zeros_like(sc_scr)
        w_scr[...] = jnp.zeros_like(w_scr)
        for h in range(PEER_HEADS):
            b_scr[h] = jnp.exp(s2_ref[h] - st_ref[2, pl.ds(h, 1), :]) * st_ref[3, pl.ds(h, 1), :]

    slot = c % 2
    sc_scr[slot] = lax.dot_general(u_ref[...], xn_ref[...], (((1,), (1,)), ((), ())), preferred_element_type=F32)

    i1_base = pl.multiple_of(jnp.clip(c - 1, 0, nc - 1) * n1, SUBLANE)
    for ii in range(n1):
        for tcol in range(tm // LANE):
            cs = slice(tcol * LANE, (tcol + 1) * LANE)
            g = jnp.zeros((N_KEYS, LANE), F32)
            for h in range(PEER_HEADS):
                s1row = s1_ref[h, pl.ds(i1_base, n1), cs][ii:ii + 1, :]
                theta = st_ref[0, pl.ds(h, 1), cs] - s1row
                arow = jnp.exp(s1row - st_ref[1, pl.ds(h, 1), cs])
                g = g + jnp.where(s2_ref[h, :, cs] >= theta, arow * b_scr[h, :, cs], 0.0)
            x = sc_scr[1 - slot, ii * N_KEYS:(ii + 1) * N_KEYS, cs]
            gelu = 0.5 * x * (1.0 + lax.erf(x * np.float32(np.sqrt(0.5))))
            w_scr[1 - slot, ii * N_KEYS:(ii + 1) * N_KEYS, cs] = (g * gelu).astype(BF16)

    acc_scr[...] += lax.dot_general(v_ref[...], w_scr[slot], (((0,), (0,)), ((), ())),
                                    preferred_element_type=F32)

    @pl.when(c == pl.num_programs(1) - 1)
    def _():
        o_ref[...] = acc_scr[...].T


def _peer_dense(xn_bf, s1, s2, stats, u_bf, v_bf, tm=512, ec=1024):
    assert ec // N_KEYS == SUBLANE
    n = xn_bf.shape[0]
    nc = u_bf.shape[0] // ec
    s_spec = pl.BlockSpec((PEER_HEADS, N_KEYS, tm), lambda i, c: (0, 0, i))
    return pl.pallas_call(
        functools.partial(_peer_dense_kernel, ec=ec, tm=tm),
        out_shape=jax.ShapeDtypeStruct((n, D_MODEL), F32),
        grid=(n // tm, nc + 2),
        in_specs=[
            pl.BlockSpec((tm, D_MODEL), lambda i, c: (i, 0)),
            s_spec, s_spec,
            pl.BlockSpec((4, PEER_HEADS, tm), lambda i, c: (0, 0, i)),
            pl.BlockSpec((ec, D_MODEL), lambda i, c: (jnp.minimum(c, nc - 1), 0)),
            pl.BlockSpec((ec, D_MODEL), lambda i, c: (jnp.clip(c - 2, 0, nc - 1), 0)),
        ],
        out_specs=pl.BlockSpec((tm, D_MODEL), lambda i, c: (i, 0), pipeline_mode=pl.Buffered(1)),
        scratch_shapes=[pltpu.VMEM((D_MODEL, tm), F32), pltpu.VMEM((PEER_HEADS, N_KEYS, tm), F32),
                        pltpu.VMEM((2, ec, tm), F32), pltpu.VMEM((2, ec, tm), BF16)],
        compiler_params=_cp(("parallel", "arbitrary")),
        name="peer_dense",
    )(xn_bf, s1, s2, stats, u_bf, v_bf)


def _final_kernel(x1_ref, p_ref, g_ref, y_ref):
    x = x1_ref[...] + p_ref[...]
    ms = jnp.mean(x * x, axis=-1, keepdims=True)
    y_ref[...] = x * lax.rsqrt(ms + NORM_EPS) * g_ref[...]


def _final(x1, p, g, tm=512):
    n = x1.shape[0]
    blk = pl.BlockSpec((tm, D_MODEL), lambda i: (i, 0))
    return pl.pallas_call(
        _final_kernel,
        out_shape=jax.ShapeDtypeStruct((n, D_MODEL), F32),
        grid=(n // tm,),
        in_specs=[blk, blk, pl.BlockSpec((1, D_MODEL), lambda i: (0, 0))],
        out_specs=blk,
        compiler_params=_cp(("parallel",)),
        name="final",
    )(x1, p, g)


def _pad_cols(a, w):
    return jnp.pad(a, ((0, 0), (0, w - a.shape[1])))


def _za_layout(a):
    r3 = a[:, :3 * A_WIDTH]
    wl = _pad_cols(a[:, 3 * A_WIDTH:3 * A_WIDTH + A_DECAY_LORA], COL_AL - COL_WL)
    al = _pad_cols(a[:, 3 * A_WIDTH + A_DECAY_LORA:3 * A_WIDTH + A_DECAY_LORA + A_ICL_LORA], COL_GL - COL_AL)
    gl = _pad_cols(a[:, 3 * A_WIDTH + A_DECAY_LORA + A_ICL_LORA:SHIFT_W], ZA_W - COL_GL)
    return jnp.concatenate([r3, wl, al, gl], axis=1)


def _za_unlayout(a):
    return jnp.concatenate([a[:, :3 * A_WIDTH], a[:, COL_WL:COL_WL + A_DECAY_LORA], a[:, COL_AL:COL_AL + A_ICL_LORA],
                            a[:, COL_GL:COL_GL + A_GATE_LORA]], axis=1)


def _pad_rows(a, n):
    return jnp.pad(a, ((0, n - a.shape[0]), (0, 0)))


def _prepare_weights(lw):
    (norm1, w_in, mu_shift, w0, w_w2, a0, w_a2, w_g2, k_k, k_a, r_k, lnx_w, lnx_b, w_pa, w_pb, w_out, norm2,
     peer_wq, peer_keys, peer_u, peer_v) = lw
    row = lambda a: a[None, :]
    w_in_bf = w_in.astype(BF16)
    wcat = jnp.concatenate([
        _za_layout(w_in_bf[:, :SHIFT_W]), jnp.zeros((D_MODEL, COL_QB - ZA_W), BF16), w_in_bf[:, SHIFT_W:]], axis=1)
    return dict(
        norm1=row(norm1), wcat=wcat, mu=_za_layout(row(mu_shift)), w0=row(w0),
        ww2=_pad_rows(w_w2, COL_AL - COL_WL).astype(BF16), a0=row(a0),
        wa2=_pad_rows(w_a2, COL_GL - COL_AL).astype(BF16), wg2=_pad_rows(w_g2, ZA_W - COL_GL).astype(BF16),
        k_k=row(k_k), k_a=row(k_a), r_k=row(r_k), lnx_w=row(lnx_w), lnx_b=row(lnx_b),
        wpa=w_pa.astype(BF16), wpb=w_pb.astype(BF16), wout=w_out.astype(BF16), norm2=row(norm2),
        wq=peer_wq.astype(BF16), keys=peer_keys.astype(BF16), u=peer_u.astype(BF16), v=peer_v.astype(BF16))


def _group_layer(x, shift0, wkv0, ret0, pos, pw, scan_tc):
    bsz, t, d = x.shape
    n = bsz * t
    z = _inproj(x.reshape(n, d), pw["norm1"], pw["wcat"])
    shift_new = _za_unlayout(z.reshape(bsz, t, W_TOT)[:, -1, :ZA_W])

    r, w, k2, v, kk, nkka, g = _rwkv_prep(z, _za_layout(shift0), bsz, t, pw)
    s0 = jnp.swapaxes(wkv0, 1, 2).reshape(bsz, A_HEAD, A_WIDTH)
    y, sf = _rwkv_scan([a.reshape(bsz, t, A_WIDTH) for a in (r, w, k2, v, kk, nkka)], s0, nb=4, tc=scan_tc)
    wkv_new = jnp.swapaxes(sf.reshape(bsz, A_HEAD, A_HEADS, A_HEAD), 1, 2)

    chunk = RET_CHUNK if t % RET_CHUNK == 0 else t
    ob, ret_new = _retention(z, ret0, _retention_tables(pos, chunk), bsz, t, chunk)

    m = _merge1(y.reshape(n, A_WIDTH), r, k2, v, g, ob, z, pw)
    x1, xn = _merge2(m, x.reshape(n, d), pw["wout"], pw["norm2"])
    s1, s2, stats = _peer_topk(xn, pw["wq"], pw["keys"])
    peer = _peer_dense(xn, s1, s2, stats, pw["u"], pw["v"])
    return x1, peer, shift_new, wkv_new, ret_new


def _trunk(x, shift0, wkv0, ret0, pos, pws, norm_f, scan_tc):
    bsz, t, d = x.shape
    shifts, wkvs, rets = [], [], []
    x1 = peer = None
    for layer, pw in enumerate(pws):
        if layer > 0:
            x = (x1 + peer).reshape(bsz, t, d)
        x1, peer, s_new, wkv_new, ret_new = _group_layer(x, shift0[layer], wkv0[layer], ret0[layer], pos, pw, scan_tc)
        shifts.append(s_new)
        wkvs.append(wkv_new)
        rets.append(ret_new)
    y = _final(x1, peer, norm_f[None, :]).reshape(bsz, t, d)
    return y, jnp.stack(shifts), jnp.stack(wkvs), jnp.stack(rets)


def kernel(x_prompt, x_sample, state_shift, state_wkv, state_ret, norm1, w_in, mu_shift, w0, w_w2, a0, w_a2, w_g2, k_k, k_a, r_k, lnx_w, lnx_b, w_pa, w_pb, w_out, norm2, peer_wq, peer_keys, peer_u, peer_v, norm_f):
    weights = (norm1, w_in, mu_shift, w0, w_w2, a0, w_a2, w_g2, k_k, k_a, r_k, lnx_w, lnx_b, w_pa, w_pb, w_out,
               norm2, peer_wq, peer_keys, peer_u, peer_v)
    depth = norm1.shape[0]
    pws = [_prepare_weights([w[layer] for w in weights]) for layer in range(depth)]
    bp, tp, _ = x_prompt.shape
    ts = x_sample.shape[1]
    shift0 = jnp.zeros((depth, bp, SHIFT_W), F32)
    wkv0 = jnp.zeros((depth, bp, A_HEADS, A_HEAD, A_HEAD), F32)
    ret0 = jnp.zeros((depth, bp, B_HEADS, B_QK_HEAD, B_V_HEAD), F32)
    y_p, shift_p, wkv_p, ret_p = _trunk(x_prompt, shift0, wkv0, ret0, jnp.arange(tp, dtype=F32), pws, norm_f,
                                        scan_tc=128)
    y_s, shift_s, wkv_s, ret_s = _trunk(x_sample, state_shift, state_wkv, state_ret,
                                        PAST_LEN + jnp.arange(ts, dtype=F32), pws, norm_f, scan_tc=ts)
    return (y_p, y_s, shift_p, wkv_p, ret_p, shift_s, wkv_s, ret_s)
```

```python
import functools

import jax
import jax.numpy as jnp
import numpy as np
from jax import lax
from jax.experimental import pallas as pl
from jax.experimental.pallas import tpu as pltpu

F32 = jnp.float32
BF16 = jnp.bfloat16

D_MODEL = 2048
A_HEAD = 64
A_WIDTH = 1024
A_HEADS = 16
A_DECAY_LORA = 64
A_ICL_LORA = 64
A_GATE_LORA = 160
A_GN_EPS = 64e-5
B_HEADS = 8
B_QK_HEAD = 128
B_V_HEAD = 256
B_QK_WIDTH = 1024
B_V_WIDTH = 2048
RET_CHUNK = 128
ROT_BASE = 10000.0
N_KEYS = 128
N_EXPERTS = N_KEYS * N_KEYS
PEER_HEADS = 8
PEER_TOPK = 16
NORM_EPS = 1e-6
PAST_LEN = 16384
SHIFT_W = 3 * A_WIDTH + A_DECAY_LORA + A_ICL_LORA + A_GATE_LORA

LANE = 128
SUBLANE = 8
SEG_W = 256
ZA_W = 3584
COL_WL = 3072
COL_AL = 3200
COL_GL = 3328
COL_QB = 4096
COL_KB = 5120
COL_VB = 6144
COL_GB = 8192
COL_GMA = 10240
COL_GMB = 12288
W_TOT = 14336
VMEM_LIMIT = 56 * 1024 * 1024


def _cp(sem):
    return pltpu.CompilerParams(dimension_semantics=sem, vmem_limit_bytes=VMEM_LIMIT)


def _sigmoid(x):
    return 1.0 / (1.0 + jnp.exp(-x))


def _inproj_kernel(x_ref, g_ref, w_ref, o_ref, hn_ref):
    @pl.when(pl.program_id(1) == 0)
    def _():
        x = x_ref[...]
        ms = jnp.mean(x * x, axis=-1, keepdims=True)
        hn_ref[...] = (x * lax.rsqrt(ms + NORM_EPS) * g_ref[...]).astype(BF16)

    o_ref[...] = jnp.dot(hn_ref[...], w_ref[...], preferred_element_type=F32)


def _inproj(x, g, w, tm=1024, tn=1024):
    n, d = x.shape
    wt = w.shape[1]
    return pl.pallas_call(
        _inproj_kernel,
        out_shape=jax.ShapeDtypeStruct((n, wt), F32),
        grid=(n // tm, wt // tn),
        in_specs=[
            pl.BlockSpec((tm, d), lambda i, j: (i, 0)),
            pl.BlockSpec((1, d), lambda i, j: (0, 0)),
            pl.BlockSpec((d, tn), lambda i, j: (0, j)),
        ],
        out_specs=pl.BlockSpec((tm, tn), lambda i, j: (i, j)),
        scratch_shapes=[pltpu.VMEM((tm, d), BF16)],
        compiler_params=_cp(("parallel", "arbitrary")),
        name="inproj",
    )(x, g, w)


def _seg_ones():
    r = lax.broadcasted_iota(jnp.int32, (SEG_W, SEG_W), 0) // A_HEAD
    c = lax.broadcasted_iota(jnp.int32, (SEG_W, SEG_W), 1) // A_HEAD
    return (r == c).astype(BF16)


def _segsum64(x, ones):
    def top8(v):
        bits = lax.bitcast_convert_type(v, jnp.uint32) & jnp.uint32(0xFFFF0000)
        return lax.bitcast_convert_type(bits, F32)

    hi = top8(x)
    r1 = x - hi
    mid = top8(r1)
    lo = r1 - mid
    hi, mid, lo = hi.astype(BF16), mid.astype(BF16), lo.astype(BF16)
    parts = []
    for j in range(x.shape[1] // SEG_W):
        sl = slice(j * SEG_W, (j + 1) * SEG_W)
        parts.append(jnp.dot(hi[:, sl], ones, preferred_element_type=F32)
                     + jnp.dot(mid[:, sl], ones, preferred_element_type=F32)
                     + jnp.dot(lo[:, sl], ones, preferred_element_type=F32))
    return jnp.concatenate(parts, axis=1)


def _rwkv_prep_kernel(z_ref, first_ref, mu_ref, w0_ref, ww2_ref, a0_ref, wa2_ref, wg2_ref, kk_ref_, ka_ref,
                      r_o, w_o, k_o, v_o, kk_o, nkka_o, g_o, carry_scr, *, tm, seq_len):
    z = z_ref[...]
    rolled = pltpu.roll(z, 1, 0)
    rows = lax.broadcasted_iota(jnp.int32, z.shape, 0)
    if seq_len >= tm:
        @pl.when(pl.program_id(1) == 0)
        def _():
            carry_scr[0:1, :] = first_ref[0]

        zprev = jnp.where(rows == 0, carry_scr[0:1, :], rolled)
        carry_scr[0:1, :] = z[tm - 1:tm, :]
    else:
        zprev = jnp.where(rows % seq_len == 0, first_ref[...], rolled)
    zs = z + (zprev - z) * mu_ref[...]
    r = zs[:, 0:A_WIDTH]
    k = zs[:, A_WIDTH:2 * A_WIDTH]
    v = zs[:, 2 * A_WIDTH:3 * A_WIDTH]
    wl = zs[:, COL_WL:COL_AL]
    al = zs[:, COL_AL:COL_GL]
    gl = zs[:, COL_GL:ZA_W]
    wpre = w0_ref[...] + jnp.dot(jnp.tanh(wl).astype(BF16), ww2_ref[...], preferred_element_type=F32)
    nw = -wpre
    softplus = jnp.maximum(nw, 0.0) + jnp.log1p(jnp.exp(-jnp.abs(nw)))
    wlog = -softplus - 0.5
    decay = jnp.exp(-jnp.exp(wlog))
    a = _sigmoid(a0_ref[...] + jnp.dot(al.astype(BF16), wa2_ref[...], preferred_element_type=F32))
    g = jnp.dot(_sigmoid(gl).astype(BF16), wg2_ref[...], preferred_element_type=F32)
    kk = k * kk_ref_[...]
    ss = _segsum64(kk * kk, _seg_ones())
    kk = kk / jnp.maximum(jnp.sqrt(ss), 1e-12)
    k2 = k * (1.0 + (a - 1.0) * ka_ref[...])
    r_o[...] = r
    w_o[...] = decay
    k_o[...] = k2
    v_o[...] = v
    kk_o[...] = kk
    nkka_o[...] = -(kk * a)
    g_o[...] = g


def _rwkv_prep(z, shift0_l, bsz, t, pw, tm=256):
    n = bsz * t
    nt = max(t // tm, 1)
    if t >= tm:
        first = shift0_l[:, None, :]
        first_spec = pl.BlockSpec((1, 1, ZA_W), lambda i, j: (i, 0, 0))
    else:
        first = jnp.pad(shift0_l[:, None, :], ((0, 0), (0, t - 1), (0, 0))).reshape(n, ZA_W)
        first_spec = pl.BlockSpec((tm, ZA_W), lambda i, j: (i * nt + j, 0))
    row = lambda w: pl.BlockSpec((1, w), lambda i, j: (0, 0))
    full = lambda a: pl.BlockSpec(a.shape, lambda i, j: (0, 0))
    out = jax.ShapeDtypeStruct((n, A_WIDTH), F32)
    return pl.pallas_call(
        functools.partial(_rwkv_prep_kernel, tm=tm, seq_len=t),
        out_shape=[out] * 7,
        grid=(n // (tm * nt), nt),
        in_specs=[
            pl.BlockSpec((tm, ZA_W), lambda i, j: (i * nt + j, 0)),
            first_spec,
            row(ZA_W), row(A_WIDTH), full(pw["ww2"]), row(A_WIDTH), full(pw["wa2"]), full(pw["wg2"]),
            row(A_WIDTH), row(A_WIDTH),
        ],
        out_specs=[pl.BlockSpec((tm, A_WIDTH), lambda i, j: (i * nt + j, 0))] * 7,
        scratch_shapes=[pltpu.VMEM((SUBLANE, ZA_W), F32)],
        compiler_params=_cp(("parallel", "arbitrary")),
        name="rwkv_prep",
    )(z, first, pw["mu"], pw["w0"], pw["ww2"], pw["a0"], pw["wa2"], pw["wg2"], pw["k_k"], pw["k_a"])


def _rwkv_scan_kernel(r_ref, w_ref, k_ref, v_ref, kk_ref, nkka_ref, s0_ref, m4_ref, y_ref, sf_ref, s_scr,
                      xa_scr, xb_scr, y8_scr, rows_scr, *, nb, tc):
    c = pl.program_id(1)

    @pl.when(c == 0)
    def _():
        s_scr[...] = s0_ref[...]

    lane = lax.broadcasted_iota(jnp.int32, (A_HEAD, LANE), 1)
    sub = lax.broadcasted_iota(jnp.int32, (A_HEAD, LANE), 0)
    lo = lane < A_HEAD
    diag = (lane & (A_HEAD - 1)) == sub
    diag_lo = jnp.logical_and(diag, lo)
    diag_hi = jnp.logical_and(diag, jnp.logical_not(lo))
    nq = A_WIDTH // SEG_W
    npair = A_WIDTH // LANE
    ngroup = xa_scr.shape[0]
    gb = nb // ngroup

    seq_refs = (r_ref, w_ref, k_ref, v_ref, kk_ref, nkka_ref)
    seq_ids = [id(ref) for ref in seq_refs]

    def step(t8, carry):
        base = pl.multiple_of(t8 * SUBLANE, SUBLANE)
        for a, ref in enumerate(seq_refs):
            for b in range(nb):
                rows_scr[a, b] = ref[b, pl.ds(base, SUBLANE), :]
        for i in range(SUBLANE):
            row = lambda ref, b, lanes: rows_scr[seq_ids.index(id(ref)), b, i:i + 1, lanes]

            def phase_a(g):
                for bl in range(gb):
                    b = g * gb + bl
                    for j in range(nq):
                        q = slice(j * SEG_W, (j + 1) * SEG_W)
                        r0 = (bl * nq + j) * A_HEAD
                        xa_scr[g, r0:r0 + A_HEAD, :] = (s_scr[b, :, q] * row(kk_ref, b, q)).astype(BF16)
                return jnp.dot(xa_scr[g], m4_ref[...], preferred_element_type=F32)

            def phase_b(g, sa_all):
                for bl in range(gb):
                    b = g * gb + bl
                    for j in range(npair):
                        sl = slice(j * LANE, (j + 1) * LANE)
                        r0 = (bl * nq + j // 2) * A_HEAD
                        half = slice((j % 2) * LANE, (j % 2 + 1) * LANE)
                        vrow = row(v_ref, b, sl)
                        v_lo = jnp.sum(jnp.where(diag_lo, vrow, 0.0), axis=1, keepdims=True)
                        v_hi = jnp.sum(jnp.where(diag_hi, vrow, 0.0), axis=1, keepdims=True)
                        vb = jnp.where(lo, v_lo, v_hi)
                        s = (s_scr[b, :, sl] * row(w_ref, b, sl)
                             + sa_all[r0:r0 + A_HEAD, half] * row(nkka_ref, b, sl) + vb * row(k_ref, b, sl))
                        s_scr[b, :, sl] = s
                        xb_scr[g, r0:r0 + A_HEAD, half] = (s * row(r_ref, b, sl)).astype(BF16)
                return jnp.dot(xb_scr[g], m4_ref[...], preferred_element_type=F32)

            def phase_c(g, y_all):
                for bl in range(gb):
                    b = g * gb + bl
                    for j in range(npair):
                        sl = slice(j * LANE, (j + 1) * LANE)
                        r0 = (bl * nq + j // 2) * A_HEAD
                        half = slice((j % 2) * LANE, (j % 2 + 1) * LANE)
                        yb = y_all[r0:r0 + A_HEAD, half]
                        y8_scr[b, i:i + 1, sl] = jnp.sum(jnp.where(diag, yb, 0.0), axis=0, keepdims=True)

            sa = [phase_a(g) for g in range(ngroup)]
            ya = [phase_b(g, sa[g]) for g in range(ngroup)]
            for g in range(ngroup):
                phase_c(g, ya[g])
        for b in range(nb):
            y_ref[b, pl.ds(base, SUBLANE), :] = y8_scr[b]
        return carry

    lax.fori_loop(0, tc // SUBLANE, step, 0)

    @pl.when(c == pl.num_programs(1) - 1)
    def _():
        sf_ref[...] = s_scr[...]


def _rwkv_scan(seqs, s0, nb, tc):
    bsz, t, _ = seqs[0].shape
    seq = pl.BlockSpec((nb, tc, A_WIDTH), lambda i, c: (i, c, 0))
    st = pl.BlockSpec((nb, A_HEAD, A_WIDTH), lambda i, c: (i, 0, 0))
    seg = jnp.arange(SEG_W, dtype=jnp.int32) // A_HEAD
    m4 = (seg[:, None] == seg[None, :]).astype(BF16)
    ngroup = 2
    xrows = (nb // ngroup) * (A_WIDTH // SEG_W) * A_HEAD
    x_scr = pltpu.VMEM((ngroup, xrows, SEG_W), BF16)
    return pl.pallas_call(
        functools.partial(_rwkv_scan_kernel, nb=nb, tc=tc),
        out_shape=[jax.ShapeDtypeStruct((bsz, t, A_WIDTH), F32), jax.ShapeDtypeStruct((bsz, A_HEAD, A_WIDTH), F32)],
        grid=(bsz // nb, t // tc),
        in_specs=[seq] * 6 + [st, pl.BlockSpec((SEG_W, SEG_W), lambda i, c: (0, 0))],
        out_specs=[seq, st],
        scratch_shapes=[pltpu.VMEM((nb, A_HEAD, A_WIDTH), F32), x_scr, x_scr,
                        pltpu.VMEM((nb, SUBLANE, A_WIDTH), F32), pltpu.VMEM((6, nb, SUBLANE, A_WIDTH), F32)],
        compiler_params=_cp(("parallel", "arbitrary")),
        name="rwkv_scan",
    )(*seqs, s0, m4)


def _retention_kernel(q_ref, k_ref, v_ref, g_ref, cos_ref, sin_ref, dm_ref, qd_ref, kd_ref, gc_ref, r0_ref,
                      o_ref, rn_ref, r_scr):
    c = pl.program_id(1)
    nbr = r_scr.shape[0]
    chunk = cos_ref.shape[0]

    @pl.when(c == 0)
    def _():
        r_scr[...] = r0_ref[...]

    cos = cos_ref[...]
    sin = sin_ref[...]

    def rot(x):
        return x * cos + pltpu.roll(x, B_QK_HEAD // 2, 1) * sin

    for bi in range(nbr):
        rows = slice(bi * chunk, (bi + 1) * chunk)
        for h in range(B_HEADS):
            qs = slice(h * B_QK_HEAD, (h + 1) * B_QK_HEAD)
            vs = slice(h * B_V_HEAD, (h + 1) * B_V_HEAD)
            q = rot(q_ref[rows, qs])
            k = rot(k_ref[rows, qs]) * (B_QK_HEAD ** -0.5)
            v = v_ref[rows, vs].astype(BF16)
            rs = r_scr[bi, h]
            qb = q.astype(BF16)
            s = lax.dot_general(qb, k.astype(BF16), (((1,), (1,)), ((), ())),
                                preferred_element_type=F32) * dm_ref[h]
            inner = jnp.dot(s.astype(BF16), v, preferred_element_type=F32)
            cross = jnp.dot(qb, rs.astype(BF16), preferred_element_type=F32) * qd_ref[h]
            kd = (k * kd_ref[h]).astype(BF16)
            r_scr[bi, h] = gc_ref[h] * rs + lax.dot_general(kd, v, (((0,), (0,)), ((), ())),
                                                            preferred_element_type=F32)
            o = inner + cross
            o = o * lax.rsqrt(jnp.mean(o * o, axis=-1, keepdims=True) + NORM_EPS)
            gate = g_ref[rows, vs]
            o_ref[rows, vs] = gate * _sigmoid(gate) * o

    @pl.when(c == pl.num_programs(1) - 1)
    def _():
        rn_ref[...] = r_scr[...]


def _retention(z, r0, tabs, bsz, t, chunk):
    cos, sin, dm, qd, kd, gc = tabs
    nc = t // chunk
    nbr = 4 if nc == 1 else 1
    seq = lambda w, col: pl.BlockSpec((nbr * chunk, w), lambda b, c: (b * nc + c, col // w))
    tab3 = lambda a: pl.BlockSpec(a.shape, lambda b, c: (0, 0, 0))
    st = pl.BlockSpec((nbr, B_HEADS, B_QK_HEAD, B_V_HEAD), lambda b, c: (b, 0, 0, 0))
    return pl.pallas_call(
        _retention_kernel,
        out_shape=[jax.ShapeDtypeStruct((bsz * t, B_V_WIDTH), F32),
                   jax.ShapeDtypeStruct((bsz, B_HEADS, B_QK_HEAD, B_V_HEAD), F32)],
        grid=(bsz // nbr, nc),
        in_specs=[
            seq(B_QK_WIDTH, COL_QB), seq(B_QK_WIDTH, COL_KB), seq(B_V_WIDTH, COL_VB), seq(B_V_WIDTH, COL_GB),
            pl.BlockSpec((chunk, B_QK_HEAD), lambda b, c: (c, 0)),
            pl.BlockSpec((chunk, B_QK_HEAD), lambda b, c: (c, 0)),
            tab3(dm), tab3(qd), tab3(kd), tab3(gc), st,
        ],
        out_specs=[pl.BlockSpec((nbr * chunk, B_V_WIDTH), lambda b, c: (b * nc + c, 0)), st],
        scratch_shapes=[pltpu.VMEM((nbr, B_HEADS, B_QK_HEAD, B_V_HEAD), F32)],
        compiler_params=_cp(("parallel", "arbitrary")),
        name="retention",
    )(z, z, z, z, cos, sin, dm, qd, kd, gc, r0)


def _retention_tables(pos, chunk):
    half = B_QK_HEAD // 2
    inv = jnp.power(ROT_BASE, -jnp.linspace(0.0, 1.0, half, dtype=F32))
    ang = pos[:, None] * inv[None, :]
    cos = jnp.cos(ang)
    sin = jnp.sin(ang)
    cos2 = jnp.concatenate([cos, cos], axis=-1)
    sin2 = jnp.concatenate([-sin, sin], axis=-1)
    log_g = jnp.log(1.0 - jnp.power(2.0, -5.0 - jnp.arange(B_HEADS, dtype=F32)))
    idx = jnp.arange(chunk, dtype=F32)
    diff = idx[:, None] - idx[None, :]
    dmask = jnp.where(diff >= 0, jnp.exp(log_g[:, None, None] * jnp.maximum(diff, 0.0)), 0.0)
    q_dec = jnp.exp(log_g[:, None] * (idx[None, :] + 1.0))
    k_dec = jnp.exp(log_g[:, None] * (chunk - 1.0 - idx[None, :]))
    g_c = jnp.exp(log_g * chunk)
    qd = jnp.broadcast_to(q_dec[:, :, None], (B_HEADS, chunk, B_V_HEAD))
    kd = jnp.broadcast_to(k_dec[:, :, None], (B_HEADS, chunk, B_QK_HEAD))
    gc = jnp.broadcast_to(g_c[:, None, None], (B_HEADS, B_QK_HEAD, B_V_HEAD))
    return cos2, sin2, dmask, qd, kd, gc


def _merge1_kernel(y_ref, r_ref, k_ref, v_ref, g_ref, ob_ref, gma_ref, gmb_ref, lw_ref, lb_ref, rk_ref, wpa_ref,
                   wpb_ref, m_ref, oa_scr, ob_scr):
    @pl.when(pl.program_id(1) == 0)
    def _():
        ones = _seg_ones()
        y = y_ref[...]
        mean = _segsum64(y, ones) * (1.0 / A_HEAD)
        yc = y - mean
        var = _segsum64(yc * yc, ones) * (1.0 / A_HEAD)
        yn = yc * lax.rsqrt(var + A_GN_EPS) * lw_ref[...] + lb_ref[...]
        bonus = _segsum64(r_ref[...] * k_ref[...] * rk_ref[...], ones) * v_ref[...]
        oa_scr[...] = ((yn + bonus) * g_ref[...]).astype(BF16)
        ob_scr[...] = ob_ref[...].astype(BF16)

    pa = jnp.dot(oa_scr[...], wpa_ref[...], preferred_element_type=F32)
    pb = jnp.dot(ob_scr[...], wpb_ref[...], preferred_element_type=F32)
    m_ref[...] = (_sigmoid(gma_ref[...]) * pa + _sigmoid(gmb_ref[...]) * pb).astype(m_ref.dtype)


def _merge1(y, r, k2, v, g, ob, z, pw, tm=512, tn=512):
    n = y.shape[0]
    a_blk = pl.BlockSpec((tm, A_WIDTH), lambda i, j: (i, 0))
    row = pl.BlockSpec((1, A_WIDTH), lambda i, j: (0, 0))
    return pl.pallas_call(
        _merge1_kernel,
        out_shape=jax.ShapeDtypeStruct((n, D_MODEL), BF16),
        grid=(n // tm, D_MODEL // tn),
        in_specs=[
            a_blk, a_blk, a_blk, a_blk, a_blk,
            pl.BlockSpec((tm, B_V_WIDTH), lambda i, j: (i, 0)),
            pl.BlockSpec((tm, tn), lambda i, j: (i, COL_GMA // tn + j)),
            pl.BlockSpec((tm, tn), lambda i, j: (i, COL_GMB // tn + j)),
            row, row, row,
            pl.BlockSpec((A_WIDTH, tn), lambda i, j: (0, j)),
            pl.BlockSpec((B_V_WIDTH, tn), lambda i, j: (0, j)),
        ],
        out_specs=pl.BlockSpec((tm, tn), lambda i, j: (i, j)),
        scratch_shapes=[pltpu.VMEM((tm, A_WIDTH), BF16), pltpu.VMEM((tm, B_V_WIDTH), BF16)],
        compiler_params=_cp(("parallel", "arbitrary")),
        name="merge1",
    )(y, r, k2, v, g, ob, z, z, pw["lnx_w"], pw["lnx_b"], pw["r_k"], pw["wpa"], pw["wpb"])


def _merge2_kernel(m_ref, x_ref, w_ref, g_ref, x1_ref, xn_ref):
    x1 = x_ref[...] + jnp.dot(m_ref[...], w_ref[...], preferred_element_type=F32)
    x1_ref[...] = x1
    ms = jnp.mean(x1 * x1, axis=-1, keepdims=True)
    xn_ref[...] = (x1 * lax.rsqrt(ms + NORM_EPS) * g_ref[...]).astype(BF16)


def _merge2(m, x, w_out, norm2, tm=512):
    n = x.shape[0]
    blk = pl.BlockSpec((tm, D_MODEL), lambda i: (i, 0))
    return pl.pallas_call(
        _merge2_kernel,
        out_shape=[jax.ShapeDtypeStruct((n, D_MODEL), F32), jax.ShapeDtypeStruct((n, D_MODEL), BF16)],
        grid=(n // tm,),
        in_specs=[blk, blk, pl.BlockSpec((D_MODEL, D_MODEL), lambda i: (0, 0)),
                  pl.BlockSpec((1, D_MODEL), lambda i: (0, 0))],
        out_specs=[blk, blk],
        compiler_params=_cp(("parallel",)),
        name="merge2",
    )(m, x, w_out, norm2)


_PAIRS = [(i, j) for i in range(PEER_TOPK) for j in range(PEER_TOPK) if (i + 1) * (j + 1) <= PEER_TOPK]


def _top16_rows(s):
    rows = []
    cur = s
    for i in range(PEER_TOPK):
        m = jnp.max(cur, axis=0, keepdims=True)
        rows.append(m)
        if i + 1 < PEER_TOPK:
            cur = jnp.where(cur == m, -jnp.inf, cur)
    return rows


def _peer_topk_kernel(xn_ref, wq_ref, keys_ref, s1_ref, s2_ref, st_ref, top_scr, cand_scr):
    q = jnp.dot(xn_ref[...], wq_ref[...], preferred_element_type=F32)
    for h in range(PEER_HEADS):
        for p in range(2):
            c0 = (h * 2 + p) * N_KEYS
            qhp = q[:, c0:c0 + N_KEYS].astype(BF16)
            st = lax.dot_general(keys_ref[h, p], qhp, (((1,), (1,)), ((), ())), preferred_element_type=F32)
            (s1_ref if p == 0 else s2_ref)[h] = st
            for i, row in enumerate(_top16_rows(st)):
                top_scr[p, i, h:h + 1, :] = row
    for n, (i, j) in enumerate(_PAIRS):
        cand_scr[n] = top_scr[0, i] + top_scr[1, j]
    m = None
    for it in range(PEER_TOPK):
        m = cand_scr[0]
        for n in range(1, len(_PAIRS)):
            m = jnp.maximum(m, cand_scr[n])
        if it + 1 < PEER_TOPK:
            for n in range(len(_PAIRS)):
                cnd = cand_scr[n]
                cand_scr[n] = jnp.where(cnd == m, -jnp.inf, cnd)
    tau = m
    top = top_scr[0, 0] + top_scr[1, 0]
    zsum = jnp.zeros_like(tau)
    for (i, j) in _PAIRS:
        cnd = top_scr[0, i] + top_scr[1, j]
        zsum = zsum + jnp.where(cnd >= tau, jnp.exp(cnd - top), 0.0)
    st_ref[0] = tau
    st_ref[1] = top_scr[0, 0]
    st_ref[2] = top_scr[1, 0]
    st_ref[3] = 1.0 / zsum


def _peer_topk(xn, wq, keys, tm=512):
    n = xn.shape[0]
    s_shape = jax.ShapeDtypeStruct((PEER_HEADS, N_KEYS, n), F32)
    s_spec = pl.BlockSpec((PEER_HEADS, N_KEYS, tm), lambda i: (0, 0, i))
    return pl.pallas_call(
        _peer_topk_kernel,
        out_shape=[s_shape, s_shape, jax.ShapeDtypeStruct((4, PEER_HEADS, n), F32)],
        grid=(n // tm,),
        in_specs=[pl.BlockSpec((tm, D_MODEL), lambda i: (i, 0)),
                  pl.BlockSpec((D_MODEL, D_MODEL), lambda i: (0, 0)),
                  pl.BlockSpec(keys.shape, lambda i: (0, 0, 0, 0))],
        out_specs=[s_spec, s_spec, pl.BlockSpec((4, PEER_HEADS, tm), lambda i: (0, 0, i))],
        scratch_shapes=[pltpu.VMEM((2, PEER_TOPK, PEER_HEADS, tm), F32),
                        pltpu.VMEM((len(_PAIRS), PEER_HEADS, tm), F32)],
        compiler_params=_cp(("parallel",)),
        name="peer_topk",
    )(xn, wq, keys)


def _peer_dense_kernel(xn_ref, s1_ref, s2_ref, st_ref, u_ref, v_ref, o_ref, acc_scr, b_scr, sc_scr, w_scr, *, ec, tm):
    c = pl.program_id(1)
    n1 = ec // N_KEYS

    @pl.when(c == 0)
    def _():
        acc_scr[...] = jnp.zeros_like(acc_scr)
        for h in range(PEER_HEADS):
            b_scr[h] = jnp.exp(s2_ref[h] - st_ref[2, pl.ds(h, 1), :]) * st_ref[3, pl.ds(h, 1), :]

    sc_scr[...] = lax.dot_general(u_ref[...], xn_ref[...], (((1,), (1,)), ((), ())), preferred_element_type=F32)

    i1_base = pl.multiple_of(c * n1, SUBLANE)
    for ii in range(n1):
        for tcol in range(tm // LANE):
            cs = slice(tcol * LANE, (tcol + 1) * LANE)
            g = jnp.zeros((N_KEYS, LANE), F32)
            for h in range(PEER_HEADS):
                s1row = s1_ref[h, pl.ds(i1_base, n1), cs][ii:ii + 1, :]
                theta = st_ref[0, pl.ds(h, 1), cs] - s1row
                arow = jnp.exp(s1row - st_ref[1, pl.ds(h, 1), cs])
                g = g + jnp.where(s2_ref[h, :, cs] >= theta, arow * b_scr[h, :, cs], 0.0)
            x = sc_scr[ii * N_KEYS:(ii + 1) * N_KEYS, cs]
            gelu = 0.5 * x * (1.0 + lax.erf(x * np.float32(np.sqrt(0.5))))
            w_scr[ii * N_KEYS:(ii + 1) * N_KEYS, cs] = (g * gelu).astype(BF16)

    acc_scr[...] += lax.dot_general(v_ref[...], w_scr[...], (((0,), (0,)), ((), ())), preferred_element_type=F32)

    @pl.when(c == pl.num_programs(1) - 1)
    def _():
        o_ref[...] = acc_scr[...].T


def _peer_dense(xn_bf, s1, s2, stats, u_bf, v_bf, tm=512, ec=1024):
    assert ec // N_KEYS == SUBLANE
    n = xn_bf.shape[0]
    s_spec = pl.BlockSpec((PEER_HEADS, N_KEYS, tm), lambda i, c: (0, 0, i))
    return pl.pallas_call(
        functools.partial(_peer_dense_kernel, ec=ec, tm=tm),
        out_shape=jax.ShapeDtypeStruct((n, D_MODEL), F32),
        grid=(n // tm, u_bf.shape[0] // ec),
        in_specs=[
            pl.BlockSpec((tm, D_MODEL), lambda i, c: (i, 0)),
            s_spec, s_spec,
            pl.BlockSpec((4, PEER_HEADS, tm), lambda i, c: (0, 0, i)),
            pl.BlockSpec((ec, D_MODEL), lambda i, c: (c, 0)),
            pl.BlockSpec((ec, D_MODEL), lambda i, c: (c, 0)),
        ],
        out_specs=pl.BlockSpec((tm, D_MODEL), lambda i, c: (i, 0), pipeline_mode=pl.Buffered(1)),
        scratch_shapes=[pltpu.VMEM((D_MODEL, tm), F32), pltpu.VMEM((PEER_HEADS, N_KEYS, tm), F32),
                        pltpu.VMEM((ec, tm), F32), pltpu.VMEM((ec, tm), BF16)],
        compiler_params=_cp(("parallel", "arbitrary")),
        name="peer_dense",
    )(xn_bf, s1, s2, stats, u_bf, v_bf)


def _final_kernel(x1_ref, p_ref, g_ref, y_ref):
    x = x1_ref[...] + p_ref[...]
    ms = jnp.mean(x * x, axis=-1, keepdims=True)
    y_ref[...] = x * lax.rsqrt(ms + NORM_EPS) * g_ref[...]


def _final(x1, p, g, tm=512):
    n = x1.shape[0]
    blk = pl.BlockSpec((tm, D_MODEL), lambda i: (i, 0))
    return pl.pallas_call(
        _final_kernel,
        out_shape=jax.ShapeDtypeStruct((n, D_MODEL), F32),
        grid=(n // tm,),
        in_specs=[blk, blk, pl.BlockSpec((1, D_MODEL), lambda i: (0, 0))],
        out_specs=blk,
        compiler_params=_cp(("parallel",)),
        name="final",
    )(x1, p, g)


def _pad_cols(a, w):
    return jnp.pad(a, ((0, 0), (0, w - a.shape[1])))


def _za_layout(a):
    r3 = a[:, :3 * A_WIDTH]
    wl = _pad_cols(a[:, 3 * A_WIDTH:3 * A_WIDTH + A_DECAY_LORA], COL_AL - COL_WL)
    al = _pad_cols(a[:, 3 * A_WIDTH + A_DECAY_LORA:3 * A_WIDTH + A_DECAY_LORA + A_ICL_LORA], COL_GL - COL_AL)
    gl = _pad_cols(a[:, 3 * A_WIDTH + A_DECAY_LORA + A_ICL_LORA:SHIFT_W], ZA_W - COL_GL)
    return jnp.concatenate([r3, wl, al, gl], axis=1)


def _za_unlayout(a):
    return jnp.concatenate([a[:, :3 * A_WIDTH], a[:, COL_WL:COL_WL + A_DECAY_LORA], a[:, COL_AL:COL_AL + A_ICL_LORA],
                            a[:, COL_GL:COL_GL + A_GATE_LORA]], axis=1)


def _pad_rows(a, n):
    return jnp.pad(a, ((0, n - a.shape[0]), (0, 0)))


def _prepare_weights(lw):
    (norm1, w_in, mu_shift, w0, w_w2, a0, w_a2, w_g2, k_k, k_a, r_k, lnx_w, lnx_b, w_pa, w_pb, w_out, norm2,
     peer_wq, peer_keys, peer_u, peer_v) = lw
    row = lambda a: a[None, :]
    w_in_bf = w_in.astype(BF16)
    wcat = jnp.concatenate([
        _za_layout(w_in_bf[:, :SHIFT_W]), jnp.zeros((D_MODEL, COL_QB - ZA_W), BF16), w_in_bf[:, SHIFT_W:]], axis=1)
    return dict(
        norm1=row(norm1), wcat=wcat, mu=_za_layout(row(mu_shift)), w0=row(w0),
        ww2=_pad_rows(w_w2, COL_AL - COL_WL).astype(BF16), a0=row(a0),
        wa2=_pad_rows(w_a2, COL_GL - COL_AL).astype(BF16), wg2=_pad_rows(w_g2, ZA_W - COL_GL).astype(BF16),
        k_k=row(k_k), k_a=row(k_a), r_k=row(r_k), lnx_w=row(lnx_w), lnx_b=row(lnx_b),
        wpa=w_pa.astype(BF16), wpb=w_pb.astype(BF16), wout=w_out.astype(BF16), norm2=row(norm2),
        wq=peer_wq.astype(BF16), keys=peer_keys.astype(BF16), u=peer_u.astype(BF16), v=peer_v.astype(BF16))


def _group_layer(x, shift0, wkv0, ret0, pos, pw, scan_tc):
    bsz, t, d = x.shape
    n = bsz * t
    z = _inproj(x.reshape(n, d), pw["norm1"], pw["wcat"])
    shift_new = _za_unlayout(z.reshape(bsz, t, W_TOT)[:, -1, :ZA_W])

    r, w, k2, v, kk, nkka, g = _rwkv_prep(z, _za_layout(shift0), bsz, t, pw)
    s0 = jnp.swapaxes(wkv0, 1, 2).reshape(bsz, A_HEAD, A_WIDTH)
    y, sf = _rwkv_scan([a.reshape(bsz, t, A_WIDTH) for a in (r, w, k2, v, kk, nkka)], s0, nb=4, tc=scan_tc)
    wkv_new = jnp.swapaxes(sf.reshape(bsz, A_HEAD, A_HEADS, A_HEAD), 1, 2)

    chunk = RET_CHUNK if t % RET_CHUNK == 0 else t
    ob, ret_new = _retention(z, ret0, _retention_tables(pos, chunk), bsz, t, chunk)

    m = _merge1(y.reshape(n, A_WIDTH), r, k2, v, g, ob, z, pw)
    x1, xn = _merge2(m, x.reshape(n, d), pw["wout"], pw["norm2"])
    s1, s2, stats = _peer_topk(xn, pw["wq"], pw["keys"])
    peer = _peer_dense(xn, s1, s2, stats, pw["u"], pw["v"])
    return x1, peer, shift_new, wkv_new, ret_new


def _trunk(x, shift0, wkv0, ret0, pos, pws, norm_f, scan_tc):
    bsz, t, d = x.shape
    shifts, wkvs, rets = [], [], []
    x1 = peer = None
    for layer, pw in enumerate(pws):
        if layer > 0:
            x = (x1 + peer).reshape(bsz, t, d)
        x1, peer, s_new, wkv_new, ret_new = _group_layer(x, shift0[layer], wkv0[layer], ret0[layer], pos, pw, scan_tc)
        shifts.append(s_new)
        wkvs.append(wkv_new)
        rets.append(ret_new)
    y = _final(x1, peer, norm_f[None, :]).reshape(bsz, t, d)
    return y, jnp.stack(shifts), jnp.stack(wkvs), jnp.stack(rets)


def kernel(x_prompt, x_sample, state_shift, state_wkv, state_ret, norm1, w_in, mu_shift, w0, w_w2, a0, w_a2, w_g2, k_k, k_a, r_k, lnx_w, lnx_b, w_pa, w_pb, w_out, norm2, peer_wq, peer_keys, peer_u, peer_v, norm_f):
    weights = (norm1, w_in, mu_shift, w0, w_w2, a0, w_a2, w_g2, k_k, k_a, r_k, lnx_w, lnx_b, w_pa, w_pb, w_out,
               norm2, peer_wq, peer_keys, peer_u, peer_v)
    depth = norm1.shape[0]
    pws = [_prepare_weights([w[layer] for w in weights]) for layer in range(depth)]
    bp, tp, _ = x_prompt.shape
    ts = x_sample.shape[1]
    shift0 = jnp.zeros((depth, bp, SHIFT_W), F32)
    wkv0 = jnp.zeros((depth, bp, A_HEADS, A_HEAD, A_HEAD), F32)
    ret0 = jnp.zeros((depth, bp, B_HEADS, B_QK_HEAD, B_V_HEAD), F32)
    y_p, shift_p, wkv_p, ret_p = _trunk(x_prompt, shift0, wkv0, ret0, jnp.arange(tp, dtype=F32), pws, norm_f,
                                        scan_tc=128)
    y_s, shift_s, wkv_s, ret_s = _trunk(x_sample, state_shift, state_wkv, state_ret,
                                        PAST_LEN + jnp.arange(ts, dtype=F32), pws, norm_f, scan_tc=ts)
    return (y_p, y_s, shift_p, wkv_p, ret_p, shift_s, wkv_s, ret_s)
```

```python
import functools

import jax
import jax.numpy as jnp
import numpy as np
from jax import lax
from jax.experimental import pallas as pl
from jax.experimental.pallas import tpu as pltpu

F32 = jnp.float32
BF16 = jnp.bfloat16

D_MODEL = 2048
A_HEAD = 64
A_WIDTH = 1024
A_HEADS = 16
A_DECAY_LORA = 64
A_ICL_LORA = 64
A_GATE_LORA = 160
A_GN_EPS = 64e-5
B_HEADS = 8
B_QK_HEAD = 128
B_V_HEAD = 256
B_QK_WIDTH = 1024
B_V_WIDTH = 2048
RET_CHUNK = 128
ROT_BASE = 10000.0
N_KEYS = 128
N_EXPERTS = N_KEYS * N_KEYS
PEER_HEADS = 8
PEER_TOPK = 16
NORM_EPS = 1e-6
PAST_LEN = 16384
SHIFT_W = 3 * A_WIDTH + A_DECAY_LORA + A_ICL_LORA + A_GATE_LORA

LANE = 128
SUBLANE = 8
SEG_W = 256
ZA_W = 3584
COL_WL = 3072
COL_AL = 3200
COL_GL = 3328
COL_QB = 4096
COL_KB = 5120
COL_VB = 6144
COL_GB = 8192
COL_GMA = 10240
COL_GMB = 12288
W_TOT = 14336
VMEM_LIMIT = 56 * 1024 * 1024


def _cp(sem):
    return pltpu.CompilerParams(dimension_semantics=sem, vmem_limit_bytes=VMEM_LIMIT)


def _sigmoid(x):
    return 1.0 / (1.0 + jnp.exp(-x))


def _inproj_kernel(x_ref, g_ref, w_ref, o_ref, hn_ref):
    @pl.when(pl.program_id(1) == 0)
    def _():
        x = x_ref[...]
        ms = jnp.mean(x * x, axis=-1, keepdims=True)
        hn_ref[...] = (x * lax.rsqrt(ms + NORM_EPS) * g_ref[...]).astype(BF16)

    o_ref[...] = jnp.dot(hn_ref[...], w_ref[...], preferred_element_type=F32)


def _inproj(x, g, w, tm=1024, tn=1024):
    n, d = x.shape
    wt = w.shape[1]
    return pl.pallas_call(
        _inproj_kernel,
        out_shape=jax.ShapeDtypeStruct((n, wt), F32),
        grid=(n // tm, wt // tn),
        in_specs=[
            pl.BlockSpec((tm, d), lambda i, j: (i, 0)),
            pl.BlockSpec((1, d), lambda i, j: (0, 0)),
            pl.BlockSpec((d, tn), lambda i, j: (0, j)),
        ],
        out_specs=pl.BlockSpec((tm, tn), lambda i, j: (i, j)),
        scratch_shapes=[pltpu.VMEM((tm, d), BF16)],
        compiler_params=_cp(("parallel", "arbitrary")),
        name="inproj",
    )(x, g, w)


def _seg_ones():
    r = lax.broadcasted_iota(jnp.int32, (SEG_W, SEG_W), 0) // A_HEAD
    c = lax.broadcasted_iota(jnp.int32, (SEG_W, SEG_W), 1) // A_HEAD
    return (r == c).astype(BF16)


def _segsum64(x, ones):
    def top8(v):
        bits = lax.bitcast_convert_type(v, jnp.uint32) & jnp.uint32(0xFFFF0000)
        return lax.bitcast_convert_type(bits, F32)

    hi = top8(x)
    r1 = x - hi
    mid = top8(r1)
    lo = r1 - mid
    hi, mid, lo = hi.astype(BF16), mid.astype(BF16), lo.astype(BF16)
    parts = []
    for j in range(x.shape[1] // SEG_W):
        sl = slice(j * SEG_W, (j + 1) * SEG_W)
        parts.append(jnp.dot(hi[:, sl], ones, preferred_element_type=F32)
                     + jnp.dot(mid[:, sl], ones, preferred_element_type=F32)
                     + jnp.dot(lo[:, sl], ones, preferred_element_type=F32))
    return jnp.concatenate(parts, axis=1)


def _rwkv_prep_kernel(z_ref, first_ref, mu_ref, w0_ref, ww2_ref, a0_ref, wa2_ref, wg2_ref, kk_ref_, ka_ref,
                      r_o, w_o, k_o, v_o, kk_o, nkka_o, g_o, carry_scr, *, tm, seq_len):
    z = z_ref[...]
    rolled = pltpu.roll(z, 1, 0)
    rows = lax.broadcasted_iota(jnp.int32, z.shape, 0)
    if seq_len >= tm:
        @pl.when(pl.program_id(1) == 0)
        def _():
            carry_scr[0:1, :] = first_ref[0]

        zprev = jnp.where(rows == 0, carry_scr[0:1, :], rolled)
        carry_scr[0:1, :] = z[tm - 1:tm, :]
    else:
        zprev = jnp.where(rows % seq_len == 0, first_ref[...], rolled)
    zs = z + (zprev - z) * mu_ref[...]
    r = zs[:, 0:A_WIDTH]
    k = zs[:, A_WIDTH:2 * A_WIDTH]
    v = zs[:, 2 * A_WIDTH:3 * A_WIDTH]
    wl = zs[:, COL_WL:COL_AL]
    al = zs[:, COL_AL:COL_GL]
    gl = zs[:, COL_GL:ZA_W]
    wpre = w0_ref[...] + jnp.dot(jnp.tanh(wl).astype(BF16), ww2_ref[...], preferred_element_type=F32)
    nw = -wpre
    softplus = jnp.maximum(nw, 0.0) + jnp.log1p(jnp.exp(-jnp.abs(nw)))
    wlog = -softplus - 0.5
    decay = jnp.exp(-jnp.exp(wlog))
    a = _sigmoid(a0_ref[...] + jnp.dot(al.astype(BF16), wa2_ref[...], preferred_element_type=F32))
    g = jnp.dot(_sigmoid(gl).astype(BF16), wg2_ref[...], preferred_element_type=F32)
    kk = k * kk_ref_[...]
    ss = _segsum64(kk * kk, _seg_ones())
    kk = kk / jnp.maximum(jnp.sqrt(ss), 1e-12)
    k2 = k * (1.0 + (a - 1.0) * ka_ref[...])
    r_o[...] = r
    w_o[...] = decay
    k_o[...] = k2
    v_o[...] = v
    kk_o[...] = kk
    nkka_o[...] = -(kk * a)
    g_o[...] = g


def _rwkv_prep(z, shift0_l, bsz, t, pw, tm=256):
    n = bsz * t
    nt = max(t // tm, 1)
    if t >= tm:
        first = shift0_l[:, None, :]
        first_spec = pl.BlockSpec((1, 1, ZA_W), lambda i, j: (i, 0, 0))
    else:
        first = jnp.pad(shift0_l[:, None, :], ((0, 0), (0, t - 1), (0, 0))).reshape(n, ZA_W)
        first_spec = pl.BlockSpec((tm, ZA_W), lambda i, j: (i * nt + j, 0))
    row = lambda w: pl.BlockSpec((1, w), lambda i, j: (0, 0))
    full = lambda a: pl.BlockSpec(a.shape, lambda i, j: (0, 0))
    out = jax.ShapeDtypeStruct((n, A_WIDTH), F32)
    return pl.pallas_call(
        functools.partial(_rwkv_prep_kernel, tm=tm, seq_len=t),
        out_shape=[out] * 7,
        grid=(n // (tm * nt), nt),
        in_specs=[
            pl.BlockSpec((tm, ZA_W), lambda i, j: (i * nt + j, 0)),
            first_spec,
            row(ZA_W), row(A_WIDTH), full(pw["ww2"]), row(A_WIDTH), full(pw["wa2"]), full(pw["wg2"]),
            row(A_WIDTH), row(A_WIDTH),
        ],
        out_specs=[pl.BlockSpec((tm, A_WIDTH), lambda i, j: (i * nt + j, 0))] * 7,
        scratch_shapes=[pltpu.VMEM((SUBLANE, ZA_W), F32)],
        compiler_params=_cp(("parallel", "arbitrary")),
        name="rwkv_prep",
    )(z, first, pw["mu"], pw["w0"], pw["ww2"], pw["a0"], pw["wa2"], pw["wg2"], pw["k_k"], pw["k_a"])


def _rwkv_scan_kernel(r_ref, w_ref, k_ref, v_ref, kk_ref, nkka_ref, s0_ref, m4_ref, y_ref, sf_ref, s_scr,
                      xa_scr, xb_scr, y8_scr, rows_scr, *, nb, tc):
    c = pl.program_id(1)

    @pl.when(c == 0)
    def _():
        s_scr[...] = s0_ref[...]

    lane = lax.broadcasted_iota(jnp.int32, (A_HEAD, LANE), 1)
    sub = lax.broadcasted_iota(jnp.int32, (A_HEAD, LANE), 0)
    lo = lane < A_HEAD
    diag = (lane & (A_HEAD - 1)) == sub
    diag_lo = jnp.logical_and(diag, lo)
    diag_hi = jnp.logical_and(diag, jnp.logical_not(lo))
    nq = A_WIDTH // SEG_W
    npair = A_WIDTH // LANE
    ngroup = xa_scr.shape[0]
    gb = nb // ngroup

    seq_refs = (r_ref, w_ref, k_ref, v_ref, kk_ref, nkka_ref)
    seq_ids = [id(ref) for ref in seq_refs]

    def step(t8, carry):
        base = pl.multiple_of(t8 * SUBLANE, SUBLANE)
        for a, ref in enumerate(seq_refs):
            for b in range(nb):
                rows_scr[a, b] = ref[b, pl.ds(base, SUBLANE), :]
        for i in range(SUBLANE):
            row = lambda ref, b, lanes: rows_scr[seq_ids.index(id(ref)), b, i:i + 1, lanes]

            def phase_a(g):
                for bl in range(gb):
                    b = g * gb + bl
                    for j in range(nq):
                        q = slice(j * SEG_W, (j + 1) * SEG_W)
                        r0 = (bl * nq + j) * A_HEAD
                        xa_scr[g, r0:r0 + A_HEAD, :] = (s_scr[b, :, q] * row(kk_ref, b, q)).astype(BF16)
                return jnp.dot(xa_scr[g], m4_ref[...], preferred_element_type=F32)

            def phase_b(g, sa_all):
                for bl in range(gb):
                    b = g * gb + bl
                    for j in range(npair):
                        sl = slice(j * LANE, (j + 1) * LANE)
                        r0 = (bl * nq + j // 2) * A_HEAD
                        half = slice((j % 2) * LANE, (j % 2 + 1) * LANE)
                        vrow = row(v_ref, b, sl)
                        v_lo = jnp.sum(jnp.where(diag_lo, vrow, 0.0), axis=1, keepdims=True)
                        v_hi = jnp.sum(jnp.where(diag_hi, vrow, 0.0), axis=1, keepdims=True)
                        vb = jnp.where(lo, v_lo, v_hi)
                        s = (s_scr[b, :, sl] * row(w_ref, b, sl)
                             + sa_all[r0:r0 + A_HEAD, half] * row(nkka_ref, b, sl) + vb * row(k_ref, b, sl))
                        s_scr[b, :, sl] = s
                        xb_scr[g, r0:r0 + A_HEAD, half] = (s * row(r_ref, b, sl)).astype(BF16)
                return jnp.dot(xb_scr[g], m4_ref[...], preferred_element_type=F32)

            def phase_c(g, y_all):
                for bl in range(gb):
                    b = g * gb + bl
                    for j in range(npair):
                        sl = slice(j * LANE, (j + 1) * LANE)
                        r0 = (bl * nq + j // 2) * A_HEAD
                        half = slice((j % 2) * LANE, (j % 2 + 1) * LANE)
                        yb = y_all[r0:r0 + A_HEAD, half]
                        y8_scr[b, i:i + 1, sl] = jnp.sum(jnp.where(diag, yb, 0.0), axis=0, keepdims=True)

            sa = [phase_a(g) for g in range(ngroup)]
            ya = [phase_b(g, sa[g]) for g in range(ngroup)]
            for g in range(ngroup):
                phase_c(g, ya[g])
        for b in range(nb):
            y_ref[b, pl.ds(base, SUBLANE), :] = y8_scr[b]
        return carry

    lax.fori_loop(0, tc // SUBLANE, step, 0)

    @pl.when(c == pl.num_programs(1) - 1)
    def _():
        sf_ref[...] = s_scr[...]


def _rwkv_scan(seqs, s0, nb, tc):
    bsz, t, _ = seqs[0].shape
    seq = pl.BlockSpec((nb, tc, A_WIDTH), lambda i, c: (i, c, 0))
    st = pl.BlockSpec((nb, A_HEAD, A_WIDTH), lambda i, c: (i, 0, 0))
    seg = jnp.arange(SEG_W, dtype=jnp.int32) // A_HEAD
    m4 = (seg[:, None] == seg[None, :]).astype(BF16)
    ngroup = 2
    xrows = (nb // ngroup) * (A_WIDTH // SEG_W) * A_HEAD
    x_scr = pltpu.VMEM((ngroup, xrows, SEG_W), BF16)
    return pl.pallas_call(
        functools.partial(_rwkv_scan_kernel, nb=nb, tc=tc),
        out_shape=[jax.ShapeDtypeStruct((bsz, t, A_WIDTH), F32), jax.ShapeDtypeStruct((bsz, A_HEAD, A_WIDTH), F32)],
        grid=(bsz // nb, t // tc),
        in_specs=[seq] * 6 + [st, pl.BlockSpec((SEG_W, SEG_W), lambda i, c: (0, 0))],
        out_specs=[seq, st],
        scratch_shapes=[pltpu.VMEM((nb, A_HEAD, A_WIDTH), F32), x_scr, x_scr,
                        pltpu.VMEM((nb, SUBLANE, A_WIDTH), F32), pltpu.VMEM((6, nb, SUBLANE, A_WIDTH), F32)],
        compiler_params=_cp(("parallel", "arbitrary")),
        name="rwkv_scan",
    )(*seqs, s0, m4)


def _retention_kernel(q_ref, k_ref, v_ref, g_ref, cos_ref, sin_ref, dm_ref, qd_ref, kd_ref, gc_ref, r0_ref,
                      o_ref, rn_ref, r_scr):
    c = pl.program_id(1)
    nbr = r_scr.shape[0]
    chunk = cos_ref.shape[0]

    @pl.when(c == 0)
    def _():
        r_scr[...] = r0_ref[...]

    cos = cos_ref[...]
    sin = sin_ref[...]

    def rot(x):
        return x * cos + pltpu.roll(x, B_QK_HEAD // 2, 1) * sin

    for bi in range(nbr):
        rows = slice(bi * chunk, (bi + 1) * chunk)
        for h in range(B_HEADS):
            qs = slice(h * B_QK_HEAD, (h + 1) * B_QK_HEAD)
            vs = slice(h * B_V_HEAD, (h + 1) * B_V_HEAD)
            q = rot(q_ref[rows, qs])
            k = rot(k_ref[rows, qs]) * (B_QK_HEAD ** -0.5)
            v = v_ref[rows, vs].astype(BF16)
            rs = r_scr[bi, h]
            qb = q.astype(BF16)
            s = lax.dot_general(qb, k.astype(BF16), (((1,), (1,)), ((), ())),
                                preferred_element_type=F32) * dm_ref[h]
            inner = jnp.dot(s.astype(BF16), v, preferred_element_type=F32)
            cross = jnp.dot(qb, rs.astype(BF16), preferred_element_type=F32) * qd_ref[h]
            kd = (k * kd_ref[h]).astype(BF16)
            r_scr[bi, h] = gc_ref[h] * rs + lax.dot_general(kd, v, (((0,), (0,)), ((), ())),
                                                            preferred_element_type=F32)
            o = inner + cross
            o = o * lax.rsqrt(jnp.mean(o * o, axis=-1, keepdims=True) + NORM_EPS)
            gate = g_ref[rows, vs]
            o_ref[rows, vs] = gate * _sigmoid(gate) * o

    @pl.when(c == pl.num_programs(1) - 1)
    def _():
        rn_ref[...] = r_scr[...]


def _retention(z, r0, tabs, bsz, t, chunk):
    cos, sin, dm, qd, kd, gc = tabs
    nc = t // chunk
    nbr = 4 if nc == 1 else 1
    seq = lambda w, col: pl.BlockSpec((nbr * chunk, w), lambda b, c: (b * nc + c, col // w))
    tab3 = lambda a: pl.BlockSpec(a.shape, lambda b, c: (0, 0, 0))
    st = pl.BlockSpec((nbr, B_HEADS, B_QK_HEAD, B_V_HEAD), lambda b, c: (b, 0, 0, 0))
    return pl.pallas_call(
        _retention_kernel,
        out_shape=[jax.ShapeDtypeStruct((bsz * t, B_V_WIDTH), F32),
                   jax.ShapeDtypeStruct((bsz, B_HEADS, B_QK_HEAD, B_V_HEAD), F32)],
        grid=(bsz // nbr, nc),
        in_specs=[
            seq(B_QK_WIDTH, COL_QB), seq(B_QK_WIDTH, COL_KB), seq(B_V_WIDTH, COL_VB), seq(B_V_WIDTH, COL_GB),
            pl.BlockSpec((chunk, B_QK_HEAD), lambda b, c: (c, 0)),
            pl.BlockSpec((chunk, B_QK_HEAD), lambda b, c: (c, 0)),
            tab3(dm), tab3(qd), tab3(kd), tab3(gc), st,
        ],
        out_specs=[pl.BlockSpec((nbr * chunk, B_V_WIDTH), lambda b, c: (b * nc + c, 0)), st],
        scratch_shapes=[pltpu.VMEM((nbr, B_HEADS, B_QK_HEAD, B_V_HEAD), F32)],
        compiler_params=_cp(("parallel", "arbitrary")),
        name="retention",
    )(z, z, z, z, cos, sin, dm, qd, kd, gc, r0)


def _retention_tables(pos, chunk):
    half = B_QK_HEAD // 2
    inv = jnp.power(ROT_BASE, -jnp.linspace(0.0, 1.0, half, dtype=F32))
    ang = pos[:, None] * inv[None, :]
    cos = jnp.cos(ang)
    sin = jnp.sin(ang)
    cos2 = jnp.concatenate([cos, cos], axis=-1)
    sin2 = jnp.concatenate([-sin, sin], axis=-1)
    log_g = jnp.log(1.0 - jnp.power(2.0, -5.0 - jnp.arange(B_HEADS, dtype=F32)))
    idx = jnp.arange(chunk, dtype=F32)
    diff = idx[:, None] - idx[None, :]
    dmask = jnp.where(diff >= 0, jnp.exp(log_g[:, None, None] * jnp.maximum(diff, 0.0)), 0.0)
    q_dec = jnp.exp(log_g[:, None] * (idx[None, :] + 1.0))
    k_dec = jnp.exp(log_g[:, None] * (chunk - 1.0 - idx[None, :]))
    g_c = jnp.exp(log_g * chunk)
    qd = jnp.broadcast_to(q_dec[:, :, None], (B_HEADS, chunk, B_V_HEAD))
    kd = jnp.broadcast_to(k_dec[:, :, None], (B_HEADS, chunk, B_QK_HEAD))
    gc = jnp.broadcast_to(g_c[:, None, None], (B_HEADS, B_QK_HEAD, B_V_HEAD))
    return cos2, sin2, dmask, qd, kd, gc


def _merge1_kernel(y_ref, r_ref, k_ref, v_ref, g_ref, ob_ref, gma_ref, gmb_ref, lw_ref, lb_ref, rk_ref, wpa_ref,
                   wpb_ref, m_ref):
    ones = _seg_ones()
    y = y_ref[...]
    mean = _segsum64(y, ones) * (1.0 / A_HEAD)
    yc = y - mean
    var = _segsum64(yc * yc, ones) * (1.0 / A_HEAD)
    yn = yc * lax.rsqrt(var + A_GN_EPS) * lw_ref[...] + lb_ref[...]
    bonus = _segsum64(r_ref[...] * k_ref[...] * rk_ref[...], ones) * v_ref[...]
    oa = ((yn + bonus) * g_ref[...]).astype(BF16)
    pa = jnp.dot(oa, wpa_ref[...], preferred_element_type=F32)
    pb = jnp.dot(ob_ref[...].astype(BF16), wpb_ref[...], preferred_element_type=F32)
    m_ref[...] = (_sigmoid(gma_ref[...]) * pa + _sigmoid(gmb_ref[...]) * pb).astype(m_ref.dtype)


def _merge1(y, r, k2, v, g, ob, z, pw, tm=256):
    n = y.shape[0]
    a_blk = pl.BlockSpec((tm, A_WIDTH), lambda i: (i, 0))
    row = pl.BlockSpec((1, A_WIDTH), lambda i: (0, 0))
    wide = lambda col: pl.BlockSpec((tm, D_MODEL), lambda i: (i, col // D_MODEL))
    resident = lambda a: pl.BlockSpec(a.shape, lambda i: (0, 0), pipeline_mode=pl.Buffered(1))
    return pl.pallas_call(
        _merge1_kernel,
        out_shape=jax.ShapeDtypeStruct((n, D_MODEL), BF16),
        grid=(n // tm,),
        in_specs=[
            a_blk, a_blk, a_blk, a_blk, a_blk,
            pl.BlockSpec((tm, B_V_WIDTH), lambda i: (i, 0)),
            wide(COL_GMA), wide(COL_GMB),
            row, row, row,
            resident(pw["wpa"]), resident(pw["wpb"]),
        ],
        out_specs=pl.BlockSpec((tm, D_MODEL), lambda i: (i, 0)),
        compiler_params=_cp(("parallel",)),
        name="merge1",
    )(y, r, k2, v, g, ob, z, z, pw["lnx_w"], pw["lnx_b"], pw["r_k"], pw["wpa"], pw["wpb"])


def _merge2_kernel(m_ref, x_ref, w_ref, g_ref, x1_ref, xn_ref):
    x1 = x_ref[...] + jnp.dot(m_ref[...], w_ref[...], preferred_element_type=F32)
    x1_ref[...] = x1
    ms = jnp.mean(x1 * x1, axis=-1, keepdims=True)
    xn_ref[...] = (x1 * lax.rsqrt(ms + NORM_EPS) * g_ref[...]).astype(BF16)


def _merge2(m, x, w_out, norm2, tm=512):
    n = x.shape[0]
    blk = pl.BlockSpec((tm, D_MODEL), lambda i: (i, 0))
    return pl.pallas_call(
        _merge2_kernel,
        out_shape=[jax.ShapeDtypeStruct((n, D_MODEL), F32), jax.ShapeDtypeStruct((n, D_MODEL), BF16)],
        grid=(n // tm,),
        in_specs=[blk, blk, pl.BlockSpec((D_MODEL, D_MODEL), lambda i: (0, 0)),
                  pl.BlockSpec((1, D_MODEL), lambda i: (0, 0))],
        out_specs=[blk, blk],
        compiler_params=_cp(("parallel",)),
        name="merge2",
    )(m, x, w_out, norm2)


_PAIRS = [(i, j) for i in range(PEER_TOPK) for j in range(PEER_TOPK) if (i + 1) * (j + 1) <= PEER_TOPK]


def _top16_rows(s):
    rows = []
    cur = s
    for i in range(PEER_TOPK):
        m = jnp.max(cur, axis=0, keepdims=True)
        rows.append(m)
        if i + 1 < PEER_TOPK:
            cur = jnp.where(cur == m, -jnp.inf, cur)
    return rows


def _peer_topk_kernel(xn_ref, wq_ref, keys_ref, s1_ref, s2_ref, st_ref, top_scr, cand_scr):
    q = jnp.dot(xn_ref[...], wq_ref[...], preferred_element_type=F32)
    for h in range(PEER_HEADS):
        for p in range(2):
            c0 = (h * 2 + p) * N_KEYS
            qhp = q[:, c0:c0 + N_KEYS].astype(BF16)
            st = lax.dot_general(keys_ref[h, p], qhp, (((1,), (1,)), ((), ())), preferred_element_type=F32)
            (s1_ref if p == 0 else s2_ref)[h] = st
            for i, row in enumerate(_top16_rows(st)):
                top_scr[p, i, h:h + 1, :] = row
    for n, (i, j) in enumerate(_PAIRS):
        cand_scr[n] = top_scr[0, i] + top_scr[1, j]
    m = None
    for it in range(PEER_TOPK):
        m = cand_scr[0]
        for n in range(1, len(_PAIRS)):
            m = jnp.maximum(m, cand_scr[n])
        if it + 1 < PEER_TOPK:
            for n in range(len(_PAIRS)):
                cnd = cand_scr[n]
                cand_scr[n] = jnp.where(cnd == m, -jnp.inf, cnd)
    tau = m
    top = top_scr[0, 0] + top_scr[1, 0]
    zsum = jnp.zeros_like(tau)
    for (i, j) in _PAIRS:
        cnd = top_scr[0, i] + top_scr[1, j]
        zsum = zsum + jnp.where(cnd >= tau, jnp.exp(cnd - top), 0.0)
    st_ref[0] = tau
    st_ref[1] = top_scr[0, 0]
    st_ref[2] = top_scr[1, 0]
    st_ref[3] = 1.0 / zsum


def _peer_topk(xn, wq, keys, tm=512):
    n = xn.shape[0]
    s_shape = jax.ShapeDtypeStruct((PEER_HEADS, N_KEYS, n), F32)
    s_spec = pl.BlockSpec((PEER_HEADS, N_KEYS, tm), lambda i: (0, 0, i))
    return pl.pallas_call(
        _peer_topk_kernel,
        out_shape=[s_shape, s_shape, jax.ShapeDtypeStruct((4, PEER_HEADS, n), F32)],
        grid=(n // tm,),
        in_specs=[pl.BlockSpec((tm, D_MODEL), lambda i: (i, 0)),
                  pl.BlockSpec((D_MODEL, D_MODEL), lambda i: (0, 0)),
                  pl.BlockSpec(keys.shape, lambda i: (0, 0, 0, 0))],
        out_specs=[s_spec, s_spec, pl.BlockSpec((4, PEER_HEADS, tm), lambda i: (0, 0, i))],
        scratch_shapes=[pltpu.VMEM((2, PEER_TOPK, PEER_HEADS, tm), F32),
                        pltpu.VMEM((len(_PAIRS), PEER_HEADS, tm), F32)],
        compiler_params=_cp(("parallel",)),
        name="peer_topk",
    )(xn, wq, keys)


def _peer_dense_kernel(xn_ref, s1_ref, s2_ref, st_ref, u_ref, v_ref, x1_ref, nf_ref, o_ref, acc_scr, b_scr, sc_scr,
                       w_scr, *, ec, tm):
    c = pl.program_id(1)
    n1 = ec // N_KEYS

    @pl.when(c == 0)
    def _():
        acc_scr[...] = jnp.zeros_like(acc_scr)
        for h in range(PEER_HEADS):
            b_scr[h] = jnp.exp(s2_ref[h] - st_ref[2, pl.ds(h, 1), :]) * (0.5 * st_ref[3, pl.ds(h, 1), :])

    sc_scr[...] = lax.dot_general(u_ref[...], xn_ref[...], (((1,), (1,)), ((), ())), preferred_element_type=F32)

    i1_base = pl.multiple_of(c * n1, SUBLANE)
    for ii in range(n1):
        for tcol in range(tm // LANE):
            cs = slice(tcol * LANE, (tcol + 1) * LANE)
            g = jnp.zeros((N_KEYS, LANE), F32)
            for h in range(PEER_HEADS):
                s1row = s1_ref[h, pl.ds(i1_base, n1), cs][ii:ii + 1, :]
                theta = st_ref[0, pl.ds(h, 1), cs] - s1row
                arow = jnp.exp(s1row - st_ref[1, pl.ds(h, 1), cs])
                g = g + jnp.where(s2_ref[h, :, cs] >= theta, arow * b_scr[h, :, cs], 0.0)
            x = sc_scr[ii * N_KEYS:(ii + 1) * N_KEYS, cs]
            gelu2 = x * (1.0 + lax.erf(x * np.float32(np.sqrt(0.5))))
            w_scr[ii * N_KEYS:(ii + 1) * N_KEYS, cs] = (g * gelu2).astype(BF16)

    acc_scr[...] += lax.dot_general(v_ref[...], w_scr[...], (((0,), (0,)), ((), ())), preferred_element_type=F32)

    @pl.when(c == pl.num_programs(1) - 1)
    def _():
        x = x1_ref[...] + acc_scr[...].T
        ms = jnp.mean(x * x, axis=-1, keepdims=True)
        o_ref[...] = x * lax.rsqrt(ms + NORM_EPS) * nf_ref[...]


def _peer_dense(xn_bf, s1, s2, stats, u_bf, v_bf, x1, norm_f, tm=512, ec=1024):
    assert ec // N_KEYS == SUBLANE
    n = xn_bf.shape[0]
    s_spec = pl.BlockSpec((PEER_HEADS, N_KEYS, tm), lambda i, c: (0, 0, i))
    once = dict(pipeline_mode=pl.Buffered(1))
    return pl.pallas_call(
        functools.partial(_peer_dense_kernel, ec=ec, tm=tm),
        out_shape=jax.ShapeDtypeStruct((n, D_MODEL), F32),
        grid=(n // tm, u_bf.shape[0] // ec),
        in_specs=[
            pl.BlockSpec((tm, D_MODEL), lambda i, c: (i, 0)),
            s_spec, s_spec,
            pl.BlockSpec((4, PEER_HEADS, tm), lambda i, c: (0, 0, i)),
            pl.BlockSpec((ec, D_MODEL), lambda i, c: (c, 0)),
            pl.BlockSpec((ec, D_MODEL), lambda i, c: (c, 0)),
            pl.BlockSpec((tm, D_MODEL), lambda i, c: (i, 0), **once),
            pl.BlockSpec((1, D_MODEL), lambda i, c: (0, 0)),
        ],
        out_specs=pl.BlockSpec((tm, D_MODEL), lambda i, c: (i, 0), **once),
        scratch_shapes=[pltpu.VMEM((D_MODEL, tm), F32), pltpu.VMEM((PEER_HEADS, N_KEYS, tm), F32),
                        pltpu.VMEM((ec, tm), F32), pltpu.VMEM((ec, tm), BF16)],
        compiler_params=_cp(("parallel", "arbitrary")),
        name="peer_dense",
    )(xn_bf, s1, s2, stats, u_bf, v_bf, x1, norm_f)


def _pad_cols(a, w):
    return jnp.pad(a, ((0, 0), (0, w - a.shape[1])))


def _za_layout(a):
    r3 = a[:, :3 * A_WIDTH]
    wl = _pad_cols(a[:, 3 * A_WIDTH:3 * A_WIDTH + A_DECAY_LORA], COL_AL - COL_WL)
    al = _pad_cols(a[:, 3 * A_WIDTH + A_DECAY_LORA:3 * A_WIDTH + A_DECAY_LORA + A_ICL_LORA], COL_GL - COL_AL)
    gl = _pad_cols(a[:, 3 * A_WIDTH + A_DECAY_LORA + A_ICL_LORA:SHIFT_W], ZA_W - COL_GL)
    return jnp.concatenate([r3, wl, al, gl], axis=1)


def _za_unlayout(a):
    return jnp.concatenate([a[:, :3 * A_WIDTH], a[:, COL_WL:COL_WL + A_DECAY_LORA], a[:, COL_AL:COL_AL + A_ICL_LORA],
                            a[:, COL_GL:COL_GL + A_GATE_LORA]], axis=1)


def _pad_rows(a, n):
    return jnp.pad(a, ((0, n - a.shape[0]), (0, 0)))


def _prepare_weights(lw):
    (norm1, w_in, mu_shift, w0, w_w2, a0, w_a2, w_g2, k_k, k_a, r_k, lnx_w, lnx_b, w_pa, w_pb, w_out, norm2,
     peer_wq, peer_keys, peer_u, peer_v) = lw
    row = lambda a: a[None, :]
    w_in_bf = w_in.astype(BF16)
    wcat = jnp.concatenate([
        _za_layout(w_in_bf[:, :SHIFT_W]), jnp.zeros((D_MODEL, COL_QB - ZA_W), BF16), w_in_bf[:, SHIFT_W:]], axis=1)
    return dict(
        norm1=row(norm1), wcat=wcat, mu=_za_layout(row(mu_shift)), w0=row(w0),
        ww2=_pad_rows(w_w2, COL_AL - COL_WL).astype(BF16), a0=row(a0),
        wa2=_pad_rows(w_a2, COL_GL - COL_AL).astype(BF16), wg2=_pad_rows(w_g2, ZA_W - COL_GL).astype(BF16),
        k_k=row(k_k), k_a=row(k_a), r_k=row(r_k), lnx_w=row(lnx_w), lnx_b=row(lnx_b),
        wpa=w_pa.astype(BF16), wpb=w_pb.astype(BF16), wout=w_out.astype(BF16), norm2=row(norm2),
        wq=peer_wq.astype(BF16), keys=peer_keys.astype(BF16), u=peer_u.astype(BF16), v=peer_v.astype(BF16))


def _trunk(x, shift0, wkv0, ret0, pos, pw, norm_f, scan_tc):
    bsz, t, d = x.shape
    n = bsz * t
    z = _inproj(x.reshape(n, d), pw["norm1"], pw["wcat"])
    shift_new = _za_unlayout(z.reshape(bsz, t, W_TOT)[:, -1, :ZA_W])

    r, w, k2, v, kk, nkka, g = _rwkv_prep(z, _za_layout(shift0), bsz, t, pw)
    s0 = jnp.swapaxes(wkv0, 1, 2).reshape(bsz, A_HEAD, A_WIDTH)
    y, sf = _rwkv_scan([a.reshape(bsz, t, A_WIDTH) for a in (r, w, k2, v, kk, nkka)], s0, nb=4, tc=scan_tc)
    wkv_new = jnp.swapaxes(sf.reshape(bsz, A_HEAD, A_HEADS, A_HEAD), 1, 2)

    chunk = RET_CHUNK if t % RET_CHUNK == 0 else t
    ob, ret_new = _retention(z, ret0, _retention_tables(pos, chunk), bsz, t, chunk)

    m = _merge1(y.reshape(n, A_WIDTH), r, k2, v, g, ob, z, pw)
    x1, xn = _merge2(m, x.reshape(n, d), pw["wout"], pw["norm2"])
    s1, s2, stats = _peer_topk(xn, pw["wq"], pw["keys"])
    y = _peer_dense(xn, s1, s2, stats, pw["u"], pw["v"], x1, norm_f[None, :])
    return y.reshape(bsz, t, d), shift_new[None], wkv_new[None], ret_new[None]


def kernel(x_prompt, x_sample, state_shift, state_wkv, state_ret, norm1, w_in, mu_shift, w0, w_w2, a0, w_a2, w_g2, k_k, k_a, r_k, lnx_w, lnx_b, w_pa, w_pb, w_out, norm2, peer_wq, peer_keys, peer_u, peer_v, norm_f):
    weights = (norm1, w_in, mu_shift, w0, w_w2, a0, w_a2, w_g2, k_k, k_a, r_k, lnx_w, lnx_b, w_pa, w_pb, w_out,
               norm2, peer_wq, peer_keys, peer_u, peer_v)
    assert norm1.shape[0] == 1
    pw = _prepare_weights([w[0] for w in weights])
    bp, tp, _ = x_prompt.shape
    ts = x_sample.shape[1]
    shift0 = jnp.zeros((bp, SHIFT_W), F32)
    wkv0 = jnp.zeros((bp, A_HEADS, A_HEAD, A_HEAD), F32)
    ret0 = jnp.zeros((bp, B_HEADS, B_QK_HEAD, B_V_HEAD), F32)
    y_p, shift_p, wkv_p, ret_p = _trunk(x_prompt, shift0, wkv0, ret0, jnp.arange(tp, dtype=F32), pw, norm_f,
                                        scan_tc=128)
    y_s, shift_s, wkv_s, ret_s = _trunk(x_sample, state_shift[0], state_wkv[0], state_ret[0],
                                        PAST_LEN + jnp.arange(ts, dtype=F32), pw, norm_f, scan_tc=ts)
    return (y_p, y_s, shift_p, wkv_p, ret_p, shift_s, wkv_s, ret_s)
```

```python
import functools

import jax
import jax.numpy as jnp
import numpy as np
from jax import lax
from jax.experimental import pallas as pl
from jax.experimental.pallas import tpu as pltpu

F32 = jnp.float32
BF16 = jnp.bfloat16

D_MODEL = 2048
A_HEAD = 64
A_WIDTH = 1024
A_HEADS = 16
A_DECAY_LORA = 64
A_ICL_LORA = 64
A_GATE_LORA = 160
A_GN_EPS = 64e-5
B_HEADS = 8
B_QK_HEAD = 128
B_V_HEAD = 256
B_QK_WIDTH = 1024
B_V_WIDTH = 2048
RET_CHUNK = 128
ROT_BASE = 10000.0
N_KEYS = 128
N_EXPERTS = N_KEYS * N_KEYS
PEER_HEADS = 8
PEER_TOPK = 16
NORM_EPS = 1e-6
PAST_LEN = 16384
SHIFT_W = 3 * A_WIDTH + A_DECAY_LORA + A_ICL_LORA + A_GATE_LORA

LANE = 128
SUBLANE = 8
SEG_W = 256
ZA_W = 3584
COL_WL = 3072
COL_AL = 3200
COL_GL = 3328
COL_QB = 4096
COL_KB = 5120
COL_VB = 6144
COL_GB = 8192
COL_GMA = 10240
COL_GMB = 12288
W_TOT = 14336
VMEM_LIMIT = 56 * 1024 * 1024


def _cp(sem):
    return pltpu.CompilerParams(dimension_semantics=sem, vmem_limit_bytes=VMEM_LIMIT)


def _sigmoid(x):
    return 1.0 / (1.0 + jnp.exp(-x))


def _inproj_kernel(x_ref, g_ref, w_ref, o_ref, hn_ref):
    @pl.when(pl.program_id(1) == 0)
    def _():
        x = x_ref[...]
        ms = jnp.mean(x * x, axis=-1, keepdims=True)
        hn_ref[...] = (x * lax.rsqrt(ms + NORM_EPS) * g_ref[...]).astype(BF16)

    o_ref[...] = jnp.dot(hn_ref[...], w_ref[...], preferred_element_type=F32)


def _inproj(x, g, w, tm=1024, tn=1024):
    n, d = x.shape
    wt = w.shape[1]
    return pl.pallas_call(
        _inproj_kernel,
        out_shape=jax.ShapeDtypeStruct((n, wt), F32),
        grid=(n // tm, wt // tn),
        in_specs=[
            pl.BlockSpec((tm, d), lambda i, j: (i, 0)),
            pl.BlockSpec((1, d), lambda i, j: (0, 0)),
            pl.BlockSpec((d, tn), lambda i, j: (0, j)),
        ],
        out_specs=pl.BlockSpec((tm, tn), lambda i, j: (i, j)),
        scratch_shapes=[pltpu.VMEM((tm, d), BF16)],
        compiler_params=_cp(("parallel", "arbitrary")),
        name="inproj",
    )(x, g, w)


def _seg_ones():
    r = lax.broadcasted_iota(jnp.int32, (SEG_W, SEG_W), 0) // A_HEAD
    c = lax.broadcasted_iota(jnp.int32, (SEG_W, SEG_W), 1) // A_HEAD
    return (r == c).astype(BF16)


def _segsum64(x, ones):
    def top8(v):
        bits = lax.bitcast_convert_type(v, jnp.uint32) & jnp.uint32(0xFFFF0000)
        return lax.bitcast_convert_type(bits, F32)

    hi = top8(x)
    r1 = x - hi
    mid = top8(r1)
    lo = r1 - mid
    hi, mid, lo = hi.astype(BF16), mid.astype(BF16), lo.astype(BF16)
    parts = []
    for j in range(x.shape[1] // SEG_W):
        sl = slice(j * SEG_W, (j + 1) * SEG_W)
        parts.append(jnp.dot(hi[:, sl], ones, preferred_element_type=F32)
                     + jnp.dot(mid[:, sl], ones, preferred_element_type=F32)
                     + jnp.dot(lo[:, sl], ones, preferred_element_type=F32))
    return jnp.concatenate(parts, axis=1)


def _rwkv_prep_kernel(z_ref, first_ref, mu_ref, w0_ref, ww2_ref, a0_ref, wa2_ref, wg2_ref, kk_ref_, ka_ref,
                      r_o, w_o, k_o, v_o, kk_o, nkka_o, g_o, carry_scr, *, tm, seq_len):
    z = z_ref[...]
    rolled = pltpu.roll(z, 1, 0)
    rows = lax.broadcasted_iota(jnp.int32, z.shape, 0)
    if seq_len >= tm:
        @pl.when(pl.program_id(1) == 0)
        def _():
            carry_scr[0:1, :] = first_ref[0]

        zprev = jnp.where(rows == 0, carry_scr[0:1, :], rolled)
        carry_scr[0:1, :] = z[tm - 1:tm, :]
    else:
        zprev = jnp.where(rows % seq_len == 0, first_ref[...], rolled)
    zs = z + (zprev - z) * mu_ref[...]
    r = zs[:, 0:A_WIDTH]
    k = zs[:, A_WIDTH:2 * A_WIDTH]
    v = zs[:, 2 * A_WIDTH:3 * A_WIDTH]
    wl = zs[:, COL_WL:COL_AL]
    al = zs[:, COL_AL:COL_GL]
    gl = zs[:, COL_GL:ZA_W]
    wpre = w0_ref[...] + jnp.dot(jnp.tanh(wl).astype(BF16), ww2_ref[...], preferred_element_type=F32)
    nw = -wpre
    softplus = jnp.maximum(nw, 0.0) + jnp.log1p(jnp.exp(-jnp.abs(nw)))
    wlog = -softplus - 0.5
    decay = jnp.exp(-jnp.exp(wlog))
    a = _sigmoid(a0_ref[...] + jnp.dot(al.astype(BF16), wa2_ref[...], preferred_element_type=F32))
    g = jnp.dot(_sigmoid(gl).astype(BF16), wg2_ref[...], preferred_element_type=F32)
    kk = k * kk_ref_[...]
    ss = _segsum64(kk * kk, _seg_ones())
    kk = kk / jnp.maximum(jnp.sqrt(ss), 1e-12)
    k2 = k * (1.0 + (a - 1.0) * ka_ref[...])
    r_o[...] = r
    w_o[...] = decay
    k_o[...] = k2
    v_o[...] = v
    kk_o[...] = kk
    nkka_o[...] = -(kk * a)
    g_o[...] = g


def _rwkv_prep(z, shift0_l, bsz, t, pw, tm=256):
    n = bsz * t
    nt = max(t // tm, 1)
    if t >= tm:
        first = shift0_l[:, None, :]
        first_spec = pl.BlockSpec((1, 1, ZA_W), lambda i, j: (i, 0, 0))
    else:
        first = jnp.pad(shift0_l[:, None, :], ((0, 0), (0, t - 1), (0, 0))).reshape(n, ZA_W)
        first_spec = pl.BlockSpec((tm, ZA_W), lambda i, j: (i * nt + j, 0))
    row = lambda w: pl.BlockSpec((1, w), lambda i, j: (0, 0))
    full = lambda a: pl.BlockSpec(a.shape, lambda i, j: (0, 0))
    out = jax.ShapeDtypeStruct((n, A_WIDTH), F32)
    return pl.pallas_call(
        functools.partial(_rwkv_prep_kernel, tm=tm, seq_len=t),
        out_shape=[out] * 7,
        grid=(n // (tm * nt), nt),
        in_specs=[
            pl.BlockSpec((tm, ZA_W), lambda i, j: (i * nt + j, 0)),
            first_spec,
            row(ZA_W), row(A_WIDTH), full(pw["ww2"]), row(A_WIDTH), full(pw["wa2"]), full(pw["wg2"]),
            row(A_WIDTH), row(A_WIDTH),
        ],
        out_specs=[pl.BlockSpec((tm, A_WIDTH), lambda i, j: (i * nt + j, 0))] * 7,
        scratch_shapes=[pltpu.VMEM((SUBLANE, ZA_W), F32)],
        compiler_params=_cp(("parallel", "arbitrary")),
        name="rwkv_prep",
    )(z, first, pw["mu"], pw["w0"], pw["ww2"], pw["a0"], pw["wa2"], pw["wg2"], pw["k_k"], pw["k_a"])


def _rwkv_scan_kernel(r_ref, w_ref, k_ref, v_ref, kk_ref, nkka_ref, s0_ref, m4_ref, y_ref, sf_ref, s_scr,
                      xa_scr, xb_scr, y8_scr, rows_scr, *, nb, tc):
    c = pl.program_id(1)

    @pl.when(c == 0)
    def _():
        s_scr[...] = s0_ref[...]

    lane = lax.broadcasted_iota(jnp.int32, (A_HEAD, LANE), 1)
    sub = lax.broadcasted_iota(jnp.int32, (A_HEAD, LANE), 0)
    lo = lane < A_HEAD
    diag = (lane & (A_HEAD - 1)) == sub
    diag_lo = jnp.logical_and(diag, lo)
    diag_hi = jnp.logical_and(diag, jnp.logical_not(lo))
    nq = A_WIDTH // SEG_W
    npair = A_WIDTH // LANE
    ngroup = xa_scr.shape[0]
    gb = nb // ngroup

    seq_refs = (r_ref, w_ref, k_ref, v_ref, kk_ref, nkka_ref)
    seq_ids = [id(ref) for ref in seq_refs]

    def step(t8, carry):
        base = pl.multiple_of(t8 * SUBLANE, SUBLANE)
        for a, ref in enumerate(seq_refs):
            for b in range(nb):
                rows_scr[a, b] = ref[b, pl.ds(base, SUBLANE), :]
        for i in range(SUBLANE):
            row = lambda ref, b, lanes: rows_scr[seq_ids.index(id(ref)), b, i:i + 1, lanes]

            def phase_a(g):
                for bl in range(gb):
                    b = g * gb + bl
                    for j in range(nq):
                        q = slice(j * SEG_W, (j + 1) * SEG_W)
                        r0 = (bl * nq + j) * A_HEAD
                        xa_scr[g, r0:r0 + A_HEAD, :] = (s_scr[b, :, q] * row(kk_ref, b, q)).astype(BF16)
                return jnp.dot(xa_scr[g], m4_ref[...], preferred_element_type=F32)

            def phase_b(g, sa_all):
                for bl in range(gb):
                    b = g * gb + bl
                    for j in range(npair):
                        sl = slice(j * LANE, (j + 1) * LANE)
                        r0 = (bl * nq + j // 2) * A_HEAD
                        half = slice((j % 2) * LANE, (j % 2 + 1) * LANE)
                        vrow = row(v_ref, b, sl)
                        v_lo = jnp.sum(jnp.where(diag_lo, vrow, 0.0), axis=1, keepdims=True)
                        v_hi = jnp.sum(jnp.where(diag_hi, vrow, 0.0), axis=1, keepdims=True)
                        vb = jnp.where(lo, v_lo, v_hi)
                        s = (s_scr[b, :, sl] * row(w_ref, b, sl)
                             + sa_all[r0:r0 + A_HEAD, half] * row(nkka_ref, b, sl) + vb * row(k_ref, b, sl))
                        s_scr[b, :, sl] = s
                        xb_scr[g, r0:r0 + A_HEAD, half] = (s * row(r_ref, b, sl)).astype(BF16)
                return jnp.dot(xb_scr[g], m4_ref[...], preferred_element_type=F32)

            def phase_c(g, y_all):
                for bl in range(gb):
                    b = g * gb + bl
                    for j in range(npair):
                        sl = slice(j * LANE, (j + 1) * LANE)
                        r0 = (bl * nq + j // 2) * A_HEAD
                        half = slice((j % 2) * LANE, (j % 2 + 1) * LANE)
                        yb = y_all[r0:r0 + A_HEAD, half]
                        y8_scr[b, i:i + 1, sl] = jnp.sum(jnp.where(diag, yb, 0.0), axis=0, keepdims=True)

            sa = [phase_a(g) for g in range(ngroup)]
            ya = [phase_b(g, sa[g]) for g in range(ngroup)]
            for g in range(ngroup):
                phase_c(g, ya[g])
        for b in range(nb):
            y_ref[b, pl.ds(base, SUBLANE), :] = y8_scr[b]
        return carry

    lax.fori_loop(0, tc // SUBLANE, step, 0)

    @pl.when(c == pl.num_programs(1) - 1)
    def _():
        sf_ref[...] = s_scr[...]


def _rwkv_scan(seqs, s0, nb, tc):
    bsz, t, _ = seqs[0].shape
    seq = pl.BlockSpec((nb, tc, A_WIDTH), lambda i, c: (i, c, 0))
    st = pl.BlockSpec((nb, A_HEAD, A_WIDTH), lambda i, c: (i, 0, 0))
    seg = jnp.arange(SEG_W, dtype=jnp.int32) // A_HEAD
    m4 = (seg[:, None] == seg[None, :]).astype(BF16)
    ngroup = 2
    xrows = (nb // ngroup) * (A_WIDTH // SEG_W) * A_HEAD
    x_scr = pltpu.VMEM((ngroup, xrows, SEG_W), BF16)
    return pl.pallas_call(
        functools.partial(_rwkv_scan_kernel, nb=nb, tc=tc),
        out_shape=[jax.ShapeDtypeStruct((bsz, t, A_WIDTH), F32), jax.ShapeDtypeStruct((bsz, A_HEAD, A_WIDTH), F32)],
        grid=(bsz // nb, t // tc),
        in_specs=[seq] * 6 + [st, pl.BlockSpec((SEG_W, SEG_W), lambda i, c: (0, 0))],
        out_specs=[seq, st],
        scratch_shapes=[pltpu.VMEM((nb, A_HEAD, A_WIDTH), F32), x_scr, x_scr,
                        pltpu.VMEM((nb, SUBLANE, A_WIDTH), F32), pltpu.VMEM((6, nb, SUBLANE, A_WIDTH), F32)],
        compiler_params=_cp(("parallel", "arbitrary")),
        name="rwkv_scan",
    )(*seqs, s0, m4)


def _retention_kernel(q_ref, k_ref, v_ref, g_ref, cos_ref, sin_ref, dm_ref, qd_ref, kd_ref, gc_ref, r0_ref,
                      o_ref, rn_ref, r_scr):
    c = pl.program_id(1)
    nbr = r_scr.shape[0]
    chunk = cos_ref.shape[0]

    @pl.when(c == 0)
    def _():
        r_scr[...] = r0_ref[...]

    cos = cos_ref[...]
    sin = sin_ref[...]

    def rot(x):
        return x * cos + pltpu.roll(x, B_QK_HEAD // 2, 1) * sin

    for bi in range(nbr):
        rows = slice(bi * chunk, (bi + 1) * chunk)
        for h in range(B_HEADS):
            qs = slice(h * B_QK_HEAD, (h + 1) * B_QK_HEAD)
            vs = slice(h * B_V_HEAD, (h + 1) * B_V_HEAD)
            q = rot(q_ref[rows, qs])
            k = rot(k_ref[rows, qs]) * (B_QK_HEAD ** -0.5)
            v = v_ref[rows, vs].astype(BF16)
            rs = r_scr[bi, h]
            qb = q.astype(BF16)
            s = lax.dot_general(qb, k.astype(BF16), (((1,), (1,)), ((), ())),
                                preferred_element_type=F32) * dm_ref[h]
            inner = jnp.dot(s.astype(BF16), v, preferred_element_type=F32)
            cross = jnp.dot(qb, rs.astype(BF16), preferred_element_type=F32) * qd_ref[h]
            kd = (k * kd_ref[h]).astype(BF16)
            r_scr[bi, h] = gc_ref[h] * rs + lax.dot_general(kd, v, (((0,), (0,)), ((), ())),
                                                            preferred_element_type=F32)
            o = inner + cross
            o = o * lax.rsqrt(jnp.mean(o * o, axis=-1, keepdims=True) + NORM_EPS)
            gate = g_ref[rows, vs]
            o_ref[rows, vs] = (gate * _sigmoid(gate) * o).astype(o_ref.dtype)

    @pl.when(c == pl.num_programs(1) - 1)
    def _():
        rn_ref[...] = r_scr[...]


def _retention(z, r0, tabs, bsz, t, chunk):
    cos, sin, dm, qd, kd, gc = tabs
    nc = t // chunk
    nbr = 4 if nc == 1 else 1
    seq = lambda w, col: pl.BlockSpec((nbr * chunk, w), lambda b, c: (b * nc + c, col // w))
    tab3 = lambda a: pl.BlockSpec(a.shape, lambda b, c: (0, 0, 0))
    st = pl.BlockSpec((nbr, B_HEADS, B_QK_HEAD, B_V_HEAD), lambda b, c: (b, 0, 0, 0))
    return pl.pallas_call(
        _retention_kernel,
        out_shape=[jax.ShapeDtypeStruct((bsz * t, B_V_WIDTH), BF16),
                   jax.ShapeDtypeStruct((bsz, B_HEADS, B_QK_HEAD, B_V_HEAD), F32)],
        grid=(bsz // nbr, nc),
        in_specs=[
            seq(B_QK_WIDTH, COL_QB), seq(B_QK_WIDTH, COL_KB), seq(B_V_WIDTH, COL_VB), seq(B_V_WIDTH, COL_GB),
            pl.BlockSpec((chunk, B_QK_HEAD), lambda b, c: (c, 0)),
            pl.BlockSpec((chunk, B_QK_HEAD), lambda b, c: (c, 0)),
            tab3(dm), tab3(qd), tab3(kd), tab3(gc), st,
        ],
        out_specs=[pl.BlockSpec((nbr * chunk, B_V_WIDTH), lambda b, c: (b * nc + c, 0)), st],
        scratch_shapes=[pltpu.VMEM((nbr, B_HEADS, B_QK_HEAD, B_V_HEAD), F32)],
        compiler_params=_cp(("parallel", "arbitrary")),
        name="retention",
    )(z, z, z, z, cos, sin, dm, qd, kd, gc, r0)


def _retention_tables(pos, chunk):
    half = B_QK_HEAD // 2
    inv = jnp.power(ROT_BASE, -jnp.linspace(0.0, 1.0, half, dtype=F32))
    ang = pos[:, None] * inv[None, :]
    cos = jnp.cos(ang)
    sin = jnp.sin(ang)
    cos2 = jnp.concatenate([cos, cos], axis=-1)
    sin2 = jnp.concatenate([-sin, sin], axis=-1)
    log_g = jnp.log(1.0 - jnp.power(2.0, -5.0 - jnp.arange(B_HEADS, dtype=F32)))
    idx = jnp.arange(chunk, dtype=F32)
    diff = idx[:, None] - idx[None, :]
    dmask = jnp.where(diff >= 0, jnp.exp(log_g[:, None, None] * jnp.maximum(diff, 0.0)), 0.0)
    q_dec = jnp.exp(log_g[:, None] * (idx[None, :] + 1.0))
    k_dec = jnp.exp(log_g[:, None] * (chunk - 1.0 - idx[None, :]))
    g_c = jnp.exp(log_g * chunk)
    qd = jnp.broadcast_to(q_dec[:, :, None], (B_HEADS, chunk, B_V_HEAD))
    kd = jnp.broadcast_to(k_dec[:, :, None], (B_HEADS, chunk, B_QK_HEAD))
    gc = jnp.broadcast_to(g_c[:, None, None], (B_HEADS, B_QK_HEAD, B_V_HEAD))
    return cos2, sin2, dmask, qd, kd, gc


def _merge1_kernel(y_ref, r_ref, k_ref, v_ref, g_ref, ob_ref, gma_ref, gmb_ref, lw_ref, lb_ref, rk_ref, wpa_ref,
                   wpb_ref, m_ref):
    ones = _seg_ones()
    y = y_ref[...]
    mean = _segsum64(y, ones) * (1.0 / A_HEAD)
    yc = y - mean
    var = _segsum64(yc * yc, ones) * (1.0 / A_HEAD)
    yn = yc * lax.rsqrt(var + A_GN_EPS) * lw_ref[...] + lb_ref[...]
    bonus = _segsum64(r_ref[...] * k_ref[...] * rk_ref[...], ones) * v_ref[...]
    oa = ((yn + bonus) * g_ref[...]).astype(BF16)
    pa = jnp.dot(oa, wpa_ref[...], preferred_element_type=F32)
    pb = jnp.dot(ob_ref[...], wpb_ref[...], preferred_element_type=F32)
    m_ref[...] = (_sigmoid(gma_ref[...]) * pa + _sigmoid(gmb_ref[...]) * pb).astype(m_ref.dtype)


def _merge1(y, r, k2, v, g, ob, z, pw, tm=256):
    n = y.shape[0]
    a_blk = pl.BlockSpec((tm, A_WIDTH), lambda i: (i, 0))
    row = pl.BlockSpec((1, A_WIDTH), lambda i: (0, 0))
    wide = lambda col: pl.BlockSpec((tm, D_MODEL), lambda i: (i, col // D_MODEL))
    resident = lambda a: pl.BlockSpec(a.shape, lambda i: (0, 0), pipeline_mode=pl.Buffered(1))
    return pl.pallas_call(
        _merge1_kernel,
        out_shape=jax.ShapeDtypeStruct((n, D_MODEL), BF16),
        grid=(n // tm,),
        in_specs=[
            a_blk, a_blk, a_blk, a_blk, a_blk,
            pl.BlockSpec((tm, B_V_WIDTH), lambda i: (i, 0)),
            wide(COL_GMA), wide(COL_GMB),
            row, row, row,
            resident(pw["wpa"]), resident(pw["wpb"]),
        ],
        out_specs=pl.BlockSpec((tm, D_MODEL), lambda i: (i, 0)),
        compiler_params=_cp(("parallel",)),
        name="merge1",
    )(y, r, k2, v, g, ob, z, z, pw["lnx_w"], pw["lnx_b"], pw["r_k"], pw["wpa"], pw["wpb"])


def _merge2_kernel(m_ref, x_ref, w_ref, g_ref, x1_ref, xn_ref):
    x1 = x_ref[...] + jnp.dot(m_ref[...], w_ref[...], preferred_element_type=F32)
    x1_ref[...] = x1
    ms = jnp.mean(x1 * x1, axis=-1, keepdims=True)
    xn_ref[...] = (x1 * lax.rsqrt(ms + NORM_EPS) * g_ref[...]).astype(BF16)


def _merge2(m, x, w_out, norm2, tm=512):
    n = x.shape[0]
    blk = pl.BlockSpec((tm, D_MODEL), lambda i: (i, 0))
    return pl.pallas_call(
        _merge2_kernel,
        out_shape=[jax.ShapeDtypeStruct((n, D_MODEL), F32), jax.ShapeDtypeStruct((n, D_MODEL), BF16)],
        grid=(n // tm,),
        in_specs=[blk, blk, pl.BlockSpec((D_MODEL, D_MODEL), lambda i: (0, 0)),
                  pl.BlockSpec((1, D_MODEL), lambda i: (0, 0))],
        out_specs=[blk, blk],
        compiler_params=_cp(("parallel",)),
        name="merge2",
    )(m, x, w_out, norm2)


_PAIRS = [(i, j) for i in range(PEER_TOPK) for j in range(PEER_TOPK) if (i + 1) * (j + 1) <= PEER_TOPK]


def _top16_rows(s):
    rows = []
    cur = s
    for i in range(PEER_TOPK):
        m = jnp.max(cur, axis=0, keepdims=True)
        rows.append(m)
        if i + 1 < PEER_TOPK:
            cur = jnp.where(cur == m, -jnp.inf, cur)
    return rows


def _peer_topk_kernel(xn_ref, wq_ref, keys_ref, s1_ref, s2_ref, st_ref, top_scr, cand_scr):
    q = jnp.dot(xn_ref[...], wq_ref[...], preferred_element_type=F32)
    for h in range(PEER_HEADS):
        for p in range(2):
            c0 = (h * 2 + p) * N_KEYS
            qhp = q[:, c0:c0 + N_KEYS].astype(BF16)
            st = lax.dot_general(keys_ref[h, p], qhp, (((1,), (1,)), ((), ())), preferred_element_type=F32)
            (s1_ref if p == 0 else s2_ref)[h] = st
            for i, row in enumerate(_top16_rows(st)):
                top_scr[p, i, h:h + 1, :] = row
    for n, (i, j) in enumerate(_PAIRS):
        cand_scr[n] = top_scr[0, i] + top_scr[1, j]
    m = None
    for it in range(PEER_TOPK):
        m = cand_scr[0]
        for n in range(1, len(_PAIRS)):
            m = jnp.maximum(m, cand_scr[n])
        if it + 1 < PEER_TOPK:
            for n in range(len(_PAIRS)):
                cnd = cand_scr[n]
                cand_scr[n] = jnp.where(cnd == m, -jnp.inf, cnd)
    tau = m
    top = top_scr[0, 0] + top_scr[1, 0]
    zsum = jnp.zeros_like(tau)
    for (i, j) in _PAIRS:
        cnd = top_scr[0, i] + top_scr[1, j]
        zsum = zsum + jnp.where(cnd >= tau, jnp.exp(cnd - top), 0.0)
    st_ref[0] = tau
    st_ref[1] = top_scr[0, 0]
    st_ref[2] = top_scr[1, 0]
    st_ref[3] = 1.0 / zsum


def _peer_topk(xn, wq, keys, tm=512):
    n = xn.shape[0]
    s_shape = jax.ShapeDtypeStruct((PEER_HEADS, N_KEYS, n), F32)
    s_spec = pl.BlockSpec((PEER_HEADS, N_KEYS, tm), lambda i: (0, 0, i))
    return pl.pallas_call(
        _peer_topk_kernel,
        out_shape=[s_shape, s_shape, jax.ShapeDtypeStruct((4, PEER_HEADS, n), F32)],
        grid=(n // tm,),
        in_specs=[pl.BlockSpec((tm, D_MODEL), lambda i: (i, 0)),
                  pl.BlockSpec((D_MODEL, D_MODEL), lambda i: (0, 0)),
                  pl.BlockSpec(keys.shape, lambda i: (0, 0, 0, 0))],
        out_specs=[s_spec, s_spec, pl.BlockSpec((4, PEER_HEADS, tm), lambda i: (0, 0, i))],
        scratch_shapes=[pltpu.VMEM((2, PEER_TOPK, PEER_HEADS, tm), F32),
                        pltpu.VMEM((len(_PAIRS), PEER_HEADS, tm), F32)],
        compiler_params=_cp(("parallel",)),
        name="peer_topk",
    )(xn, wq, keys)


def _peer_dense_kernel(xn_ref, s1_ref, s2_ref, st_ref, u_ref, v_ref, x1_ref, nf_ref, o_ref, acc_scr, b_scr, sc_scr,
                       w_scr, *, ec, tm):
    c = pl.program_id(1)
    n1 = ec // N_KEYS

    @pl.when(c == 0)
    def _():
        acc_scr[...] = jnp.zeros_like(acc_scr)
        for h in range(PEER_HEADS):
            b_scr[h] = jnp.exp(s2_ref[h] - st_ref[2, pl.ds(h, 1), :]) * (0.5 * st_ref[3, pl.ds(h, 1), :])

    sc_scr[...] = lax.dot_general(u_ref[...], xn_ref[...], (((1,), (1,)), ((), ())), preferred_element_type=F32)

    i1_base = pl.multiple_of(c * n1, SUBLANE)
    for ii in range(n1):
        for tcol in range(tm // LANE):
            cs = slice(tcol * LANE, (tcol + 1) * LANE)
            g = jnp.zeros((N_KEYS, LANE), F32)
            for h in range(PEER_HEADS):
                s1row = s1_ref[h, pl.ds(i1_base, n1), cs][ii:ii + 1, :]
                theta = st_ref[0, pl.ds(h, 1), cs] - s1row
                arow = jnp.exp(s1row - st_ref[1, pl.ds(h, 1), cs])
                g = g + jnp.where(s2_ref[h, :, cs] >= theta, arow * b_scr[h, :, cs], 0.0)
            x = sc_scr[ii * N_KEYS:(ii + 1) * N_KEYS, cs]
            gelu2 = x * (1.0 + lax.erf(x * np.float32(np.sqrt(0.5))))
            w_scr[ii * N_KEYS:(ii + 1) * N_KEYS, cs] = (g * gelu2).astype(BF16)

    acc_scr[...] += lax.dot_general(v_ref[...], w_scr[...], (((0,), (0,)), ((), ())), preferred_element_type=F32)

    @pl.when(c == pl.num_programs(1) - 1)
    def _():
        x = x1_ref[...] + acc_scr[...].T
        ms = jnp.mean(x * x, axis=-1, keepdims=True)
        o_ref[...] = x * lax.rsqrt(ms + NORM_EPS) * nf_ref[...]


def _peer_dense(xn_bf, s1, s2, stats, u_bf, v_bf, x1, norm_f, tm=512, ec=1024):
    assert ec // N_KEYS == SUBLANE
    n = xn_bf.shape[0]
    s_spec = pl.BlockSpec((PEER_HEADS, N_KEYS, tm), lambda i, c: (0, 0, i))
    once = dict(pipeline_mode=pl.Buffered(1))
    return pl.pallas_call(
        functools.partial(_peer_dense_kernel, ec=ec, tm=tm),
        out_shape=jax.ShapeDtypeStruct((n, D_MODEL), F32),
        grid=(n // tm, u_bf.shape[0] // ec),
        in_specs=[
            pl.BlockSpec((tm, D_MODEL), lambda i, c: (i, 0)),
            s_spec, s_spec,
            pl.BlockSpec((4, PEER_HEADS, tm), lambda i, c: (0, 0, i)),
            pl.BlockSpec((ec, D_MODEL), lambda i, c: (c, 0)),
            pl.BlockSpec((ec, D_MODEL), lambda i, c: (c, 0)),
            pl.BlockSpec((tm, D_MODEL), lambda i, c: (i, 0), **once),
            pl.BlockSpec((1, D_MODEL), lambda i, c: (0, 0)),
        ],
        out_specs=pl.BlockSpec((tm, D_MODEL), lambda i, c: (i, 0), **once),
        scratch_shapes=[pltpu.VMEM((D_MODEL, tm), F32), pltpu.VMEM((PEER_HEADS, N_KEYS, tm), F32),
                        pltpu.VMEM((ec, tm), F32), pltpu.VMEM((ec, tm), BF16)],
        compiler_params=_cp(("parallel", "arbitrary")),
        name="peer_dense",
    )(xn_bf, s1, s2, stats, u_bf, v_bf, x1, norm_f)


def _pad_cols(a, w):
    return jnp.pad(a, ((0, 0), (0, w - a.shape[1])))


def _za_layout(a):
    r3 = a[:, :3 * A_WIDTH]
    wl = _pad_cols(a[:, 3 * A_WIDTH:3 * A_WIDTH + A_DECAY_LORA], COL_AL - COL_WL)
    al = _pad_cols(a[:, 3 * A_WIDTH + A_DECAY_LORA:3 * A_WIDTH + A_DECAY_LORA + A_ICL_LORA], COL_GL - COL_AL)
    gl = _pad_cols(a[:, 3 * A_WIDTH + A_DECAY_LORA + A_ICL_LORA:SHIFT_W], ZA_W - COL_GL)
    return jnp.concatenate([r3, wl, al, gl], axis=1)


def _za_unlayout(a):
    return jnp.concatenate([a[:, :3 * A_WIDTH], a[:, COL_WL:COL_WL + A_DECAY_LORA], a[:, COL_AL:COL_AL + A_ICL_LORA],
                            a[:, COL_GL:COL_GL + A_GATE_LORA]], axis=1)


def _pad_rows(a, n):
    return jnp.pad(a, ((0, n - a.shape[0]), (0, 0)))


def _prepare_weights(lw):
    (norm1, w_in, mu_shift, w0, w_w2, a0, w_a2, w_g2, k_k, k_a, r_k, lnx_w, lnx_b, w_pa, w_pb, w_out, norm2,
     peer_wq, peer_keys, peer_u, peer_v) = lw
    row = lambda a: a[None, :]
    w_in_bf = w_in.astype(BF16)
    wcat = jnp.concatenate([
        _za_layout(w_in_bf[:, :SHIFT_W]), jnp.zeros((D_MODEL, COL_QB - ZA_W), BF16), w_in_bf[:, SHIFT_W:]], axis=1)
    return dict(
        norm1=row(norm1), wcat=wcat, mu=_za_layout(row(mu_shift)), w0=row(w0),
        ww2=_pad_rows(w_w2, COL_AL - COL_WL).astype(BF16), a0=row(a0),
        wa2=_pad_rows(w_a2, COL_GL - COL_AL).astype(BF16), wg2=_pad_rows(w_g2, ZA_W - COL_GL).astype(BF16),
        k_k=row(k_k), k_a=row(k_a), r_k=row(r_k), lnx_w=row(lnx_w), lnx_b=row(lnx_b),
        wpa=w_pa.astype(BF16), wpb=w_pb.astype(BF16), wout=w_out.astype(BF16), norm2=row(norm2),
        wq=peer_wq.astype(BF16), keys=peer_keys.astype(BF16), u=peer_u.astype(BF16), v=peer_v.astype(BF16))


def _trunk(x, shift0, wkv0, ret0, pos, pw, norm_f, scan_tc):
    bsz, t, d = x.shape
    n = bsz * t
    z = _inproj(x.reshape(n, d), pw["norm1"], pw["wcat"])
    shift_new = _za_unlayout(z.reshape(bsz, t, W_TOT)[:, -1, :ZA_W])

    r, w, k2, v, kk, nkka, g = _rwkv_prep(z, _za_layout(shift0), bsz, t, pw)
    s0 = jnp.swapaxes(wkv0, 1, 2).reshape(bsz, A_HEAD, A_WIDTH)
    y, sf = _rwkv_scan([a.reshape(bsz, t, A_WIDTH) for a in (r, w, k2, v, kk, nkka)], s0, nb=4, tc=scan_tc)
    wkv_new = jnp.swapaxes(sf.reshape(bsz, A_HEAD, A_HEADS, A_HEAD), 1, 2)

    chunk = RET_CHUNK if t % RET_CHUNK == 0 else t
    ob, ret_new = _retention(z, ret0, _retention_tables(pos, chunk), bsz, t, chunk)

    m = _merge1(y.reshape(n, A_WIDTH), r, k2, v, g, ob, z, pw)
    x1, xn = _merge2(m, x.reshape(n, d), pw["wout"], pw["norm2"])
    s1, s2, stats = _peer_topk(xn, pw["wq"], pw["keys"])
    y = _peer_dense(xn, s1, s2, stats, pw["u"], pw["v"], x1, norm_f[None, :])
    return y.reshape(bsz, t, d), shift_new[None], wkv_new[None], ret_new[None]


def kernel(x_prompt, x_sample, state_shift, state_wkv, state_ret, norm1, w_in, mu_shift, w0, w_w2, a0, w_a2, w_g2, k_k, k_a, r_k, lnx_w, lnx_b, w_pa, w_pb, w_out, norm2, peer_wq, peer_keys, peer_u, peer_v, norm_f):
    weights = (norm1, w_in, mu_shift, w0, w_w2, a0, w_a2, w_g2, k_k, k_a, r_k, lnx_w, lnx_b, w_pa, w_pb, w_out,
               norm2, peer_wq, peer_keys, peer_u, peer_v)
    assert norm1.shape[0] == 1
    pw = _prepare_weights([w[0] for w in weights])
    bp, tp, _ = x_prompt.shape
    ts = x_sample.shape[1]
    shift0 = jnp.zeros((bp, SHIFT_W), F32)
    wkv0 = jnp.zeros((bp, A_HEADS, A_HEAD, A_HEAD), F32)
    ret0 = jnp.zeros((bp, B_HEADS, B_QK_HEAD, B_V_HEAD), F32)
    y_p, shift_p, wkv_p, ret_p = _trunk(x_prompt, shift0, wkv0, ret0, jnp.arange(tp, dtype=F32), pw, norm_f,
                                        scan_tc=128)
    y_s, shift_s, wkv_s, ret_s = _trunk(x_sample, state_shift[0], state_wkv[0], state_ret[0],
                                        PAST_LEN + jnp.arange(ts, dtype=F32), pw, norm_f, scan_tc=ts)
    return (y_p, y_s, shift_p, wkv_p, ret_p, shift_s, wkv_s, ret_s)
```

```python
import functools

import jax
import jax.numpy as jnp
import numpy as np
from jax import lax
from jax.experimental import pallas as pl
from jax.experimental.pallas import tpu as pltpu

F32 = jnp.float32
BF16 = jnp.bfloat16

D_MODEL = 2048
A_HEAD = 64
A_WIDTH = 1024
A_HEADS = 16
A_DECAY_LORA = 64
A_ICL_LORA = 64
A_GATE_LORA = 160
A_GN_EPS = 64e-5
B_HEADS = 8
B_QK_HEAD = 128
B_V_HEAD = 256
B_QK_WIDTH = 1024
B_V_WIDTH = 2048
RET_CHUNK = 128
ROT_BASE = 10000.0
N_KEYS = 128
N_EXPERTS = N_KEYS * N_KEYS
PEER_HEADS = 8
PEER_TOPK = 16
NORM_EPS = 1e-6
PAST_LEN = 16384
SHIFT_W = 3 * A_WIDTH + A_DECAY_LORA + A_ICL_LORA + A_GATE_LORA

LANE = 128
SUBLANE = 8
SEG_W = 256
ZA_W = 3584
COL_WL = 3072
COL_AL = 3200
COL_GL = 3328
COL_QB = 0
COL_KB = 1024
COL_VB = 2048
COL_GB = 4096
COL_GMA = 6144
COL_GMB = 8192
COL_ZA = 10240
ZA_SPLIT = 2048
W_TOT = 13824
VMEM_LIMIT = 56 * 1024 * 1024


def _cp(sem):
    return pltpu.CompilerParams(dimension_semantics=sem, vmem_limit_bytes=VMEM_LIMIT)


def _sigmoid(x):
    return 1.0 / (1.0 + jnp.exp(-x))


def _inproj_kernel(x_ref, g_ref, w_ref, o_ref, hn_ref):
    @pl.when(pl.program_id(1) == 0)
    def _():
        x = x_ref[...]
        ms = jnp.mean(x * x, axis=-1, keepdims=True)
        hn_ref[...] = (x * lax.rsqrt(ms + NORM_EPS) * g_ref[...]).astype(BF16)

    o_ref[...] = jnp.dot(hn_ref[...], w_ref[...], preferred_element_type=F32)


def _inproj(x, g, w, tm=1024, tn=1536):
    n, d = x.shape
    wt = w.shape[1]
    return pl.pallas_call(
        _inproj_kernel,
        out_shape=jax.ShapeDtypeStruct((n, wt), F32),
        grid=(n // tm, wt // tn),
        in_specs=[
            pl.BlockSpec((tm, d), lambda i, j: (i, 0)),
            pl.BlockSpec((1, d), lambda i, j: (0, 0)),
            pl.BlockSpec((d, tn), lambda i, j: (0, j)),
        ],
        out_specs=pl.BlockSpec((tm, tn), lambda i, j: (i, j)),
        scratch_shapes=[pltpu.VMEM((tm, d), BF16)],
        compiler_params=_cp(("parallel", "arbitrary")),
        name="inproj",
    )(x, g, w)


def _seg_ones():
    r = lax.broadcasted_iota(jnp.int32, (SEG_W, SEG_W), 0) // A_HEAD
    c = lax.broadcasted_iota(jnp.int32, (SEG_W, SEG_W), 1) // A_HEAD
    return (r == c).astype(BF16)


def _segsum64(x, ones):
    def top8(v):
        bits = lax.bitcast_convert_type(v, jnp.uint32) & jnp.uint32(0xFFFF0000)
        return lax.bitcast_convert_type(bits, F32)

    hi = top8(x)
    r1 = x - hi
    mid = top8(r1)
    lo = r1 - mid
    hi, mid, lo = hi.astype(BF16), mid.astype(BF16), lo.astype(BF16)
    parts = []
    for j in range(x.shape[1] // SEG_W):
        sl = slice(j * SEG_W, (j + 1) * SEG_W)
        parts.append(jnp.dot(hi[:, sl], ones, preferred_element_type=F32)
                     + jnp.dot(mid[:, sl], ones, preferred_element_type=F32)
                     + jnp.dot(lo[:, sl], ones, preferred_element_type=F32))
    return jnp.concatenate(parts, axis=1)


def _rwkv_prep_kernel(z1_ref, z2_ref, first_ref, mu_ref, w0_ref, ww2_ref, a0_ref, wa2_ref, wg2_ref, kk_ref_, ka_ref,
                      r_o, w_o, k_o, v_o, kk_o, nkka_o, g_o, carry_scr, *, tm, seq_len):
    z = jnp.concatenate([z1_ref[...], z2_ref[...]], axis=1)
    rolled = pltpu.roll(z, 1, 0)
    rows = lax.broadcasted_iota(jnp.int32, z.shape, 0)
    if seq_len >= tm:
        @pl.when(pl.program_id(1) == 0)
        def _():
            carry_scr[0:1, :] = first_ref[0]

        zprev = jnp.where(rows == 0, carry_scr[0:1, :], rolled)
        carry_scr[0:1, :] = z[tm - 1:tm, :]
    else:
        zprev = jnp.where(rows % seq_len == 0, first_ref[...], rolled)
    zs = z + (zprev - z) * mu_ref[...]
    r = zs[:, 0:A_WIDTH]
    k = zs[:, A_WIDTH:2 * A_WIDTH]
    v = zs[:, 2 * A_WIDTH:3 * A_WIDTH]
    wl = zs[:, COL_WL:COL_AL]
    al = zs[:, COL_AL:COL_GL]
    gl = zs[:, COL_GL:ZA_W]
    wpre = w0_ref[...] + jnp.dot(jnp.tanh(wl).astype(BF16), ww2_ref[...], preferred_element_type=F32)
    nw = -wpre
    softplus = jnp.maximum(nw, 0.0) + jnp.log1p(jnp.exp(-jnp.abs(nw)))
    wlog = -softplus - 0.5
    decay = jnp.exp(-jnp.exp(wlog))
    a = _sigmoid(a0_ref[...] + jnp.dot(al.astype(BF16), wa2_ref[...], preferred_element_type=F32))
    g = jnp.dot(_sigmoid(gl).astype(BF16), wg2_ref[...], preferred_element_type=F32)
    kk = k * kk_ref_[...]
    ss = _segsum64(kk * kk, _seg_ones())
    kk = kk / jnp.maximum(jnp.sqrt(ss), 1e-12)
    k2 = k * (1.0 + (a - 1.0) * ka_ref[...])
    r_o[...] = r
    w_o[...] = decay
    k_o[...] = k2
    v_o[...] = v
    kk_o[...] = kk
    nkka_o[...] = -(kk * a)
    g_o[...] = g


def _rwkv_prep(z, shift0_l, bsz, t, pw, tm=256):
    n = bsz * t
    nt = max(t // tm, 1)
    if t >= tm:
        first = shift0_l[:, None, :]
        first_spec = pl.BlockSpec((1, 1, ZA_W), lambda i, j: (i, 0, 0))
    else:
        first = jnp.pad(shift0_l[:, None, :], ((0, 0), (0, t - 1), (0, 0))).reshape(n, ZA_W)
        first_spec = pl.BlockSpec((tm, ZA_W), lambda i, j: (i * nt + j, 0))
    row = lambda w: pl.BlockSpec((1, w), lambda i, j: (0, 0))
    full = lambda a: pl.BlockSpec(a.shape, lambda i, j: (0, 0))
    out = jax.ShapeDtypeStruct((n, A_WIDTH), F32)
    return pl.pallas_call(
        functools.partial(_rwkv_prep_kernel, tm=tm, seq_len=t),
        out_shape=[out] * 7,
        grid=(n // (tm * nt), nt),
        in_specs=[
            pl.BlockSpec((tm, ZA_SPLIT), lambda i, j: (i * nt + j, COL_ZA // ZA_SPLIT)),
            pl.BlockSpec((tm, ZA_W - ZA_SPLIT), lambda i, j: (i * nt + j, (COL_ZA + ZA_SPLIT) // (ZA_W - ZA_SPLIT))),
            first_spec,
            row(ZA_W), row(A_WIDTH), full(pw["ww2"]), row(A_WIDTH), full(pw["wa2"]), full(pw["wg2"]),
            row(A_WIDTH), row(A_WIDTH),
        ],
        out_specs=[pl.BlockSpec((tm, A_WIDTH), lambda i, j: (i * nt + j, 0))] * 7,
        scratch_shapes=[pltpu.VMEM((SUBLANE, ZA_W), F32)],
        compiler_params=_cp(("parallel", "arbitrary")),
        name="rwkv_prep",
    )(z, z, first, pw["mu"], pw["w0"], pw["ww2"], pw["a0"], pw["wa2"], pw["wg2"], pw["k_k"], pw["k_a"])


def _rwkv_scan_kernel(r_ref, w_ref, k_ref, v_ref, kk_ref, nkka_ref, s0_ref, m4_ref, y_ref, sf_ref, s_scr,
                      xa_scr, xb_scr, y8_scr, rows_scr, *, nb, tc):
    c = pl.program_id(1)

    @pl.when(c == 0)
    def _():
        s_scr[...] = s0_ref[...]

    lane = lax.broadcasted_iota(jnp.int32, (A_HEAD, LANE), 1)
    sub = lax.broadcasted_iota(jnp.int32, (A_HEAD, LANE), 0)
    lo = lane < A_HEAD
    diag = (lane & (A_HEAD - 1)) == sub
    diag_lo = jnp.logical_and(diag, lo)
    diag_hi = jnp.logical_and(diag, jnp.logical_not(lo))
    nq = A_WIDTH // SEG_W
    npair = A_WIDTH // LANE
    ngroup = xa_scr.shape[0]
    gb = nb // ngroup

    seq_refs = (r_ref, w_ref, k_ref, v_ref, kk_ref, nkka_ref)
    seq_ids = [id(ref) for ref in seq_refs]

    def step(t8, carry):
        base = pl.multiple_of(t8 * SUBLANE, SUBLANE)
        for a, ref in enumerate(seq_refs):
            for b in range(nb):
                rows_scr[a, b] = ref[b, pl.ds(base, SUBLANE), :]
        for i in range(SUBLANE):
            row = lambda ref, b, lanes: rows_scr[seq_ids.index(id(ref)), b, i:i + 1, lanes]

            def phase_a(g):
                for bl in range(gb):
                    b = g * gb + bl
                    for j in range(nq):
                        q = slice(j * SEG_W, (j + 1) * SEG_W)
                        r0 = (bl * nq + j) * A_HEAD
                        xa_scr[g, r0:r0 + A_HEAD, :] = (s_scr[b, :, q] * row(kk_ref, b, q)).astype(BF16)
                return jnp.dot(xa_scr[g], m4_ref[...], preferred_element_type=F32)

            def phase_b(g, sa_all):
                for bl in range(gb):
                    b = g * gb + bl
                    for j in range(npair):
                        sl = slice(j * LANE, (j + 1) * LANE)
                        r0 = (bl * nq + j // 2) * A_HEAD
                        half = slice((j % 2) * LANE, (j % 2 + 1) * LANE)
                        vrow = row(v_ref, b, sl)
                        v_lo = jnp.sum(jnp.where(diag_lo, vrow, 0.0), axis=1, keepdims=True)
                        v_hi = jnp.sum(jnp.where(diag_hi, vrow, 0.0), axis=1, keepdims=True)
                        vb = jnp.where(lo, v_lo, v_hi)
                        s = (s_scr[b, :, sl] * row(w_ref, b, sl)
                             + sa_all[r0:r0 + A_HEAD, half] * row(nkka_ref, b, sl) + vb * row(k_ref, b, sl))
                        s_scr[b, :, sl] = s
                        xb_scr[g, r0:r0 + A_HEAD, half] = (s * row(r_ref, b, sl)).astype(BF16)
                return jnp.dot(xb_scr[g], m4_ref[...], preferred_element_type=F32)

            def phase_c(g, y_all):
                for bl in range(gb):
                    b = g * gb + bl
                    for j in range(npair):
                        sl = slice(j * LANE, (j + 1) * LANE)
                        r0 = (bl * nq + j // 2) * A_HEAD
                        half = slice((j % 2) * LANE, (j % 2 + 1) * LANE)
                        yb = y_all[r0:r0 + A_HEAD, half]
                        y8_scr[b, i:i + 1, sl] = jnp.sum(jnp.where(diag, yb, 0.0), axis=0, keepdims=True)

            sa = [phase_a(g) for g in range(ngroup)]
            ya = [phase_b(g, sa[g]) for g in range(ngroup)]
            for g in range(ngroup):
                phase_c(g, ya[g])
        for b in range(nb):
            y_ref[b, pl.ds(base, SUBLANE), :] = y8_scr[b]
        return carry

    lax.fori_loop(0, tc // SUBLANE, step, 0)

    @pl.when(c == pl.num_programs(1) - 1)
    def _():
        sf_ref[...] = s_scr[...]


def _rwkv_scan(seqs, s0, nb, tc):
    bsz, t, _ = seqs[0].shape
    seq = pl.BlockSpec((nb, tc, A_WIDTH), lambda i, c: (i, c, 0))
    st = pl.BlockSpec((nb, A_HEAD, A_WIDTH), lambda i, c: (i, 0, 0))
    seg = jnp.arange(SEG_W, dtype=jnp.int32) // A_HEAD
    m4 = (seg[:, None] == seg[None, :]).astype(BF16)
    ngroup = 2
    xrows = (nb // ngroup) * (A_WIDTH // SEG_W) * A_HEAD
    x_scr = pltpu.VMEM((ngroup, xrows, SEG_W), BF16)
    return pl.pallas_call(
        functools.partial(_rwkv_scan_kernel, nb=nb, tc=tc),
        out_shape=[jax.ShapeDtypeStruct((bsz, t, A_WIDTH), F32), jax.ShapeDtypeStruct((bsz, A_HEAD, A_WIDTH), F32)],
        grid=(bsz // nb, t // tc),
        in_specs=[seq] * 6 + [st, pl.BlockSpec((SEG_W, SEG_W), lambda i, c: (0, 0))],
        out_specs=[seq, st],
        scratch_shapes=[pltpu.VMEM((nb, A_HEAD, A_WIDTH), F32), x_scr, x_scr,
                        pltpu.VMEM((nb, SUBLANE, A_WIDTH), F32), pltpu.VMEM((6, nb, SUBLANE, A_WIDTH), F32)],
        compiler_params=_cp(("parallel", "arbitrary")),
        name="rwkv_scan",
    )(*seqs, s0, m4)


def _retention_kernel(q_ref, k_ref, v_ref, g_ref, cos_ref, sin_ref, dm_ref, qd_ref, kd_ref, gc_ref, r0_ref,
                      o_ref, rn_ref, r_scr):
    c = pl.program_id(1)
    nbr = r_scr.shape[0]
    chunk = cos_ref.shape[0]

    @pl.when(c == 0)
    def _():
        r_scr[...] = r0_ref[...]

    cos = cos_ref[...]
    sin = sin_ref[...]

    def rot(x):
        return x * cos + pltpu.roll(x, B_QK_HEAD // 2, 1) * sin

    for bi in range(nbr):
        rows = slice(bi * chunk, (bi + 1) * chunk)
        for h in range(B_HEADS):
            qs = slice(h * B_QK_HEAD, (h + 1) * B_QK_HEAD)
            vs = slice(h * B_V_HEAD, (h + 1) * B_V_HEAD)
            q = rot(q_ref[rows, qs])
            k = rot(k_ref[rows, qs]) * (B_QK_HEAD ** -0.5)
            v = v_ref[rows, vs].astype(BF16)
            rs = r_scr[bi, h]
            qb = q.astype(BF16)
            s = lax.dot_general(qb, k.astype(BF16), (((1,), (1,)), ((), ())),
                                preferred_element_type=F32) * dm_ref[h]
            inner = jnp.dot(s.astype(BF16), v, preferred_element_type=F32)
            cross = jnp.dot(qb, rs.astype(BF16), preferred_element_type=F32) * qd_ref[h]
            kd = (k * kd_ref[h]).astype(BF16)
            r_scr[bi, h] = gc_ref[h] * rs + lax.dot_general(kd, v, (((0,), (0,)), ((), ())),
                                                            preferred_element_type=F32)
            o = inner + cross
            o = o * lax.rsqrt(jnp.mean(o * o, axis=-1, keepdims=True) + NORM_EPS)
            gate = g_ref[rows, vs]
            o_ref[rows, vs] = gate * _sigmoid(gate) * o

    @pl.when(c == pl.num_programs(1) - 1)
    def _():
        rn_ref[...] = r_scr[...]


def _retention(z, r0, tabs, bsz, t, chunk):
    cos, sin, dm, qd, kd, gc = tabs
    nc = t // chunk
    nbr = 4 if nc == 1 else 1
    seq = lambda w, col: pl.BlockSpec((nbr * chunk, w), lambda b, c: (b * nc + c, col // w))
    tab3 = lambda a: pl.BlockSpec(a.shape, lambda b, c: (0, 0, 0))
    st = pl.BlockSpec((nbr, B_HEADS, B_QK_HEAD, B_V_HEAD), lambda b, c: (b, 0, 0, 0))
    return pl.pallas_call(
        _retention_kernel,
        out_shape=[jax.ShapeDtypeStruct((bsz * t, B_V_WIDTH), F32),
                   jax.ShapeDtypeStruct((bsz, B_HEADS, B_QK_HEAD, B_V_HEAD), F32)],
        grid=(bsz // nbr, nc),
        in_specs=[
            seq(B_QK_WIDTH, COL_QB), seq(B_QK_WIDTH, COL_KB), seq(B_V_WIDTH, COL_VB), seq(B_V_WIDTH, COL_GB),
            pl.BlockSpec((chunk, B_QK_HEAD), lambda b, c: (c, 0)),
            pl.BlockSpec((chunk, B_QK_HEAD), lambda b, c: (c, 0)),
            tab3(dm), tab3(qd), tab3(kd), tab3(gc), st,
        ],
        out_specs=[pl.BlockSpec((nbr * chunk, B_V_WIDTH), lambda b, c: (b * nc + c, 0)), st],
        scratch_shapes=[pltpu.VMEM((nbr, B_HEADS, B_QK_HEAD, B_V_HEAD), F32)],
        compiler_params=_cp(("parallel", "arbitrary")),
        name="retention",
    )(z, z, z, z, cos, sin, dm, qd, kd, gc, r0)


def _retention_tables(pos, chunk):
    half = B_QK_HEAD // 2
    inv = jnp.power(ROT_BASE, -jnp.linspace(0.0, 1.0, half, dtype=F32))
    ang = pos[:, None] * inv[None, :]
    cos = jnp.cos(ang)
    sin = jnp.sin(ang)
    cos2 = jnp.concatenate([cos, cos], axis=-1)
    sin2 = jnp.concatenate([-sin, sin], axis=-1)
    log_g = jnp.log(1.0 - jnp.power(2.0, -5.0 - jnp.arange(B_HEADS, dtype=F32)))
    idx = jnp.arange(chunk, dtype=F32)
    diff = idx[:, None] - idx[None, :]
    dmask = jnp.where(diff >= 0, jnp.exp(log_g[:, None, None] * jnp.maximum(diff, 0.0)), 0.0)
    q_dec = jnp.exp(log_g[:, None] * (idx[None, :] + 1.0))
    k_dec = jnp.exp(log_g[:, None] * (chunk - 1.0 - idx[None, :]))
    g_c = jnp.exp(log_g * chunk)
    qd = jnp.broadcast_to(q_dec[:, :, None], (B_HEADS, chunk, B_V_HEAD))
    kd = jnp.broadcast_to(k_dec[:, :, None], (B_HEADS, chunk, B_QK_HEAD))
    gc = jnp.broadcast_to(g_c[:, None, None], (B_HEADS, B_QK_HEAD, B_V_HEAD))
    return cos2, sin2, dmask, qd, kd, gc


def _merge1_kernel(y_ref, r_ref, k_ref, v_ref, g_ref, ob_ref, gma_ref, gmb_ref, lw_ref, lb_ref, rk_ref, wpa_ref,
                   wpb_ref, m_ref):
    ones = _seg_ones()
    y = y_ref[...]
    mean = _segsum64(y, ones) * (1.0 / A_HEAD)
    yc = y - mean
    var = _segsum64(yc * yc, ones) * (1.0 / A_HEAD)
    yn = yc * lax.rsqrt(var + A_GN_EPS) * lw_ref[...] + lb_ref[...]
    bonus = _segsum64(r_ref[...] * k_ref[...] * rk_ref[...], ones) * v_ref[...]
    oa = ((yn + bonus) * g_ref[...]).astype(BF16)
    pa = jnp.dot(oa, wpa_ref[...], preferred_element_type=F32)
    pb = jnp.dot(ob_ref[...].astype(BF16), wpb_ref[...], preferred_element_type=F32)
    m_ref[...] = (_sigmoid(gma_ref[...]) * pa + _sigmoid(gmb_ref[...]) * pb).astype(m_ref.dtype)


def _merge1(y, r, k2, v, g, ob, z, pw, tm=256):
    n = y.shape[0]
    a_blk = pl.BlockSpec((tm, A_WIDTH), lambda i: (i, 0))
    row = pl.BlockSpec((1, A_WIDTH), lambda i: (0, 0))
    wide = lambda col: pl.BlockSpec((tm, D_MODEL), lambda i: (i, col // D_MODEL))
    resident = lambda a: pl.BlockSpec(a.shape, lambda i: (0, 0), pipeline_mode=pl.Buffered(1))
    return pl.pallas_call(
        _merge1_kernel,
        out_shape=jax.ShapeDtypeStruct((n, D_MODEL), BF16),
        grid=(n // tm,),
        in_specs=[
            a_blk, a_blk, a_blk, a_blk, a_blk,
            pl.BlockSpec((tm, B_V_WIDTH), lambda i: (i, 0)),
            wide(COL_GMA), wide(COL_GMB),
            row, row, row,
            resident(pw["wpa"]), resident(pw["wpb"]),
        ],
        out_specs=pl.BlockSpec((tm, D_MODEL), lambda i: (i, 0)),
        compiler_params=_cp(("parallel",)),
        name="merge1",
    )(y, r, k2, v, g, ob, z, z, pw["lnx_w"], pw["lnx_b"], pw["r_k"], pw["wpa"], pw["wpb"])


def _merge2_kernel(m_ref, x_ref, w_ref, g_ref, x1_ref, xn_ref):
    x1 = x_ref[...] + jnp.dot(m_ref[...], w_ref[...], preferred_element_type=F32)
    x1_ref[...] = x1
    ms = jnp.mean(x1 * x1, axis=-1, keepdims=True)
    xn_ref[...] = (x1 * lax.rsqrt(ms + NORM_EPS) * g_ref[...]).astype(BF16)


def _merge2(m, x, w_out, norm2, tm=512):
    n = x.shape[0]
    blk = pl.BlockSpec((tm, D_MODEL), lambda i: (i, 0))
    return pl.pallas_call(
        _merge2_kernel,
        out_shape=[jax.ShapeDtypeStruct((n, D_MODEL), F32), jax.ShapeDtypeStruct((n, D_MODEL), BF16)],
        grid=(n // tm,),
        in_specs=[blk, blk, pl.BlockSpec((D_MODEL, D_MODEL), lambda i: (0, 0)),
                  pl.BlockSpec((1, D_MODEL), lambda i: (0, 0))],
        out_specs=[blk, blk],
        compiler_params=_cp(("parallel",)),
        name="merge2",
    )(m, x, w_out, norm2)


_PAIRS = [(i, j) for i in range(PEER_TOPK) for j in range(PEER_TOPK) if (i + 1) * (j + 1) <= PEER_TOPK]


def _top16_rows(s):
    rows = []
    cur = s
    for i in range(PEER_TOPK):
        m = jnp.max(cur, axis=0, keepdims=True)
        rows.append(m)
        if i + 1 < PEER_TOPK:
            cur = jnp.where(cur == m, -jnp.inf, cur)
    return rows


def _peer_topk_kernel(xn_ref, wq_ref, keys_ref, s1_ref, s2_ref, st_ref, top_scr, cand_scr):
    q = jnp.dot(xn_ref[...], wq_ref[...], preferred_element_type=F32)
    for h in range(PEER_HEADS):
        for p in range(2):
            c0 = (h * 2 + p) * N_KEYS
            qhp = q[:, c0:c0 + N_KEYS].astype(BF16)
            st = lax.dot_general(keys_ref[h, p], qhp, (((1,), (1,)), ((), ())), preferred_element_type=F32)
            (s1_ref if p == 0 else s2_ref)[h] = st
            for i, row in enumerate(_top16_rows(st)):
                top_scr[p, i, h:h + 1, :] = row
    for n, (i, j) in enumerate(_PAIRS):
        cand_scr[n] = top_scr[0, i] + top_scr[1, j]
    m = None
    for it in range(PEER_TOPK):
        m = cand_scr[0]
        for n in range(1, len(_PAIRS)):
            m = jnp.maximum(m, cand_scr[n])
        if it + 1 < PEER_TOPK:
            for n in range(len(_PAIRS)):
                cnd = cand_scr[n]
                cand_scr[n] = jnp.where(cnd == m, -jnp.inf, cnd)
    tau = m
    top = top_scr[0, 0] + top_scr[1, 0]
    zsum = jnp.zeros_like(tau)
    for (i, j) in _PAIRS:
        cnd = top_scr[0, i] + top_scr[1, j]
        zsum = zsum + jnp.where(cnd >= tau, jnp.exp(cnd - top), 0.0)
    st_ref[0] = tau
    st_ref[1] = top_scr[0, 0]
    st_ref[2] = top_scr[1, 0]
    st_ref[3] = 1.0 / zsum


def _peer_topk(xn, wq, keys, tm=512):
    n = xn.shape[0]
    s_shape = jax.ShapeDtypeStruct((PEER_HEADS, N_KEYS, n), F32)
    s_spec = pl.BlockSpec((PEER_HEADS, N_KEYS, tm), lambda i: (0, 0, i))
    return pl.pallas_call(
        _peer_topk_kernel,
        out_shape=[s_shape, s_shape, jax.ShapeDtypeStruct((4, PEER_HEADS, n), F32)],
        grid=(n // tm,),
        in_specs=[pl.BlockSpec((tm, D_MODEL), lambda i: (i, 0)),
                  pl.BlockSpec((D_MODEL, D_MODEL), lambda i: (0, 0)),
                  pl.BlockSpec(keys.shape, lambda i: (0, 0, 0, 0))],
        out_specs=[s_spec, s_spec, pl.BlockSpec((4, PEER_HEADS, tm), lambda i: (0, 0, i))],
        scratch_shapes=[pltpu.VMEM((2, PEER_TOPK, PEER_HEADS, tm), F32),
                        pltpu.VMEM((len(_PAIRS), PEER_HEADS, tm), F32)],
        compiler_params=_cp(("parallel",)),
        name="peer_topk",
    )(xn, wq, keys)


def _peer_dense_kernel(xn_ref, s1_ref, s2_ref, st_ref, u_ref, v_ref, x1_ref, nf_ref, o_ref, acc_scr, b_scr, sc_scr,
                       w_scr, *, ec, tm):
    c = pl.program_id(1)
    n1 = ec // N_KEYS

    @pl.when(c == 0)
    def _():
        acc_scr[...] = jnp.zeros_like(acc_scr)
        for h in range(PEER_HEADS):
            b_scr[h] = jnp.exp(s2_ref[h] - st_ref[2, pl.ds(h, 1), :]) * (0.5 * st_ref[3, pl.ds(h, 1), :])

    sc_scr[...] = lax.dot_general(u_ref[...], xn_ref[...], (((1,), (1,)), ((), ())), preferred_element_type=F32)

    i1_base = pl.multiple_of(c * n1, SUBLANE)
    for ii in range(n1):
        for tcol in range(tm // LANE):
            cs = slice(tcol * LANE, (tcol + 1) * LANE)
            g = jnp.zeros((N_KEYS, LANE), F32)
            for h in range(PEER_HEADS):
                s1row = s1_ref[h, pl.ds(i1_base, n1), cs][ii:ii + 1, :]
                theta = st_ref[0, pl.ds(h, 1), cs] - s1row
                arow = jnp.exp(s1row - st_ref[1, pl.ds(h, 1), cs])
                g = g + jnp.where(s2_ref[h, :, cs] >= theta, arow * b_scr[h, :, cs], 0.0)
            x = sc_scr[ii * N_KEYS:(ii + 1) * N_KEYS, cs]
            gelu2 = x * (1.0 + lax.erf(x * np.float32(np.sqrt(0.5))))
            w_scr[ii * N_KEYS:(ii + 1) * N_KEYS, cs] = (g * gelu2).astype(BF16)

    acc_scr[...] += lax.dot_general(v_ref[...], w_scr[...], (((0,), (0,)), ((), ())), preferred_element_type=F32)

    @pl.when(c == pl.num_programs(1) - 1)
    def _():
        x = x1_ref[...] + acc_scr[...].T
        ms = jnp.mean(x * x, axis=-1, keepdims=True)
        o_ref[...] = x * lax.rsqrt(ms + NORM_EPS) * nf_ref[...]


def _peer_dense(xn_bf, s1, s2, stats, u_bf, v_bf, x1, norm_f, tm=512, ec=1024):
    assert ec // N_KEYS == SUBLANE
    n = xn_bf.shape[0]
    s_spec = pl.BlockSpec((PEER_HEADS, N_KEYS, tm), lambda i, c: (0, 0, i))
    once = dict(pipeline_mode=pl.Buffered(1))
    return pl.pallas_call(
        functools.partial(_peer_dense_kernel, ec=ec, tm=tm),
        out_shape=jax.ShapeDtypeStruct((n, D_MODEL), F32),
        grid=(n // tm, u_bf.shape[0] // ec),
        in_specs=[
            pl.BlockSpec((tm, D_MODEL), lambda i, c: (i, 0)),
            s_spec, s_spec,
            pl.BlockSpec((4, PEER_HEADS, tm), lambda i, c: (0, 0, i)),
            pl.BlockSpec((ec, D_MODEL), lambda i, c: (c, 0)),
            pl.BlockSpec((ec, D_MODEL), lambda i, c: (c, 0)),
            pl.BlockSpec((tm, D_MODEL), lambda i, c: (i, 0), **once),
            pl.BlockSpec((1, D_MODEL), lambda i, c: (0, 0)),
        ],
        out_specs=pl.BlockSpec((tm, D_MODEL), lambda i, c: (i, 0), **once),
        scratch_shapes=[pltpu.VMEM((D_MODEL, tm), F32), pltpu.VMEM((PEER_HEADS, N_KEYS, tm), F32),
                        pltpu.VMEM((ec, tm), F32), pltpu.VMEM((ec, tm), BF16)],
        compiler_params=_cp(("parallel", "arbitrary")),
        name="peer_dense",
    )(xn_bf, s1, s2, stats, u_bf, v_bf, x1, norm_f)


def _pad_cols(a, w):
    return jnp.pad(a, ((0, 0), (0, w - a.shape[1])))


def _za_layout(a):
    r3 = a[:, :3 * A_WIDTH]
    wl = _pad_cols(a[:, 3 * A_WIDTH:3 * A_WIDTH + A_DECAY_LORA], COL_AL - COL_WL)
    al = _pad_cols(a[:, 3 * A_WIDTH + A_DECAY_LORA:3 * A_WIDTH + A_DECAY_LORA + A_ICL_LORA], COL_GL - COL_AL)
    gl = _pad_cols(a[:, 3 * A_WIDTH + A_DECAY_LORA + A_ICL_LORA:SHIFT_W], ZA_W - COL_GL)
    return jnp.concatenate([r3, wl, al, gl], axis=1)


def _za_unlayout(a):
    return jnp.concatenate([a[:, :3 * A_WIDTH], a[:, COL_WL:COL_WL + A_DECAY_LORA], a[:, COL_AL:COL_AL + A_ICL_LORA],
                            a[:, COL_GL:COL_GL + A_GATE_LORA]], axis=1)


def _pad_rows(a, n):
    return jnp.pad(a, ((0, n - a.shape[0]), (0, 0)))


def _prepare_weights(lw):
    (norm1, w_in, mu_shift, w0, w_w2, a0, w_a2, w_g2, k_k, k_a, r_k, lnx_w, lnx_b, w_pa, w_pb, w_out, norm2,
     peer_wq, peer_keys, peer_u, peer_v) = lw
    row = lambda a: a[None, :]
    w_in_bf = w_in.astype(BF16)
    wcat = jnp.concatenate([w_in_bf[:, SHIFT_W:], _za_layout(w_in_bf[:, :SHIFT_W])], axis=1)
    return dict(
        norm1=row(norm1), wcat=wcat, mu=_za_layout(row(mu_shift)), w0=row(w0),
        ww2=_pad_rows(w_w2, COL_AL - COL_WL).astype(BF16), a0=row(a0),
        wa2=_pad_rows(w_a2, COL_GL - COL_AL).astype(BF16), wg2=_pad_rows(w_g2, ZA_W - COL_GL).astype(BF16),
        k_k=row(k_k), k_a=row(k_a), r_k=row(r_k), lnx_w=row(lnx_w), lnx_b=row(lnx_b),
        wpa=w_pa.astype(BF16), wpb=w_pb.astype(BF16), wout=w_out.astype(BF16), norm2=row(norm2),
        wq=peer_wq.astype(BF16), keys=peer_keys.astype(BF16), u=peer_u.astype(BF16), v=peer_v.astype(BF16))


def _trunk(x, shift0, wkv0, ret0, pos, pw, norm_f, scan_tc):
    bsz, t, d = x.shape
    n = bsz * t
    z = _inproj(x.reshape(n, d), pw["norm1"], pw["wcat"])
    shift_new = _za_unlayout(z.reshape(bsz, t, W_TOT)[:, -1, COL_ZA:])

    r, w, k2, v, kk, nkka, g = _rwkv_prep(z, _za_layout(shift0), bsz, t, pw)
    s0 = jnp.swapaxes(wkv0, 1, 2).reshape(bsz, A_HEAD, A_WIDTH)
    y, sf = _rwkv_scan([a.reshape(bsz, t, A_WIDTH) for a in (r, w, k2, v, kk, nkka)], s0, nb=4, tc=scan_tc)
    wkv_new = jnp.swapaxes(sf.reshape(bsz, A_HEAD, A_HEADS, A_HEAD), 1, 2)

    chunk = RET_CHUNK if t % RET_CHUNK == 0 else t
    ob, ret_new = _retention(z, ret0, _retention_tables(pos, chunk), bsz, t, chunk)

    m = _merge1(y.reshape(n, A_WIDTH), r, k2, v, g, ob, z, pw)
    x1, xn = _merge2(m, x.reshape(n, d), pw["wout"], pw["norm2"])
    s1, s2, stats = _peer_topk(xn, pw["wq"], pw["keys"])
    y = _peer_dense(xn, s1, s2, stats, pw["u"], pw["v"], x1, norm_f[None, :])
    return y.reshape(bsz, t, d), shift_new[None], wkv_new[None], ret_new[None]


def kernel(x_prompt, x_sample, state_shift, state_wkv, state_ret, norm1, w_in, mu_shift, w0, w_w2, a0, w_a2, w_g2, k_k, k_a, r_k, lnx_w, lnx_b, w_pa, w_pb, w_out, norm2, peer_wq, peer_keys, peer_u, peer_v, norm_f):
    weights = (norm1, w_in, mu_shift, w0, w_w2, a0, w_a2, w_g2, k_k, k_a, r_k, lnx_w, lnx_b, w_pa, w_pb, w_out,
               norm2, peer_wq, peer_keys, peer_u, peer_v)
    assert norm1.shape[0] == 1
    pw = _prepare_weights([w[0] for w in weights])
    bp, tp, _ = x_prompt.shape
    ts = x_sample.shape[1]
    shift0 = jnp.zeros((bp, SHIFT_W), F32)
    wkv0 = jnp.zeros((bp, A_HEADS, A_HEAD, A_HEAD), F32)
    ret0 = jnp.zeros((bp, B_HEADS, B_QK_HEAD, B_V_HEAD), F32)
    y_p, shift_p, wkv_p, ret_p = _trunk(x_prompt, shift0, wkv0, ret0, jnp.arange(tp, dtype=F32), pw, norm_f,
                                        scan_tc=128)
    y_s, shift_s, wkv_s, ret_s = _trunk(x_sample, state_shift[0], state_wkv[0], state_ret[0],
                                        PAST_LEN + jnp.arange(ts, dtype=F32), pw, norm_f, scan_tc=ts)
    return (y_p, y_s, shift_p, wkv_p, ret_p, shift_s, wkv_s, ret_s)
```

```python
import functools

import jax
import jax.numpy as jnp
import numpy as np
from jax import lax
from jax.experimental import pallas as pl
from jax.experimental.pallas import tpu as pltpu

F32 = jnp.float32
BF16 = jnp.bfloat16

D_MODEL = 2048
A_HEAD = 64
A_WIDTH = 1024
A_HEADS = 16
A_DECAY_LORA = 64
A_ICL_LORA = 64
A_GATE_LORA = 160
A_GN_EPS = 64e-5
B_HEADS = 8
B_QK_HEAD = 128
B_V_HEAD = 256
B_QK_WIDTH = 1024
B_V_WIDTH = 2048
RET_CHUNK = 128
ROT_BASE = 10000.0
N_KEYS = 128
N_EXPERTS = N_KEYS * N_KEYS
PEER_HEADS = 8
PEER_TOPK = 16
NORM_EPS = 1e-6
PAST_LEN = 16384
SHIFT_W = 3 * A_WIDTH + A_DECAY_LORA + A_ICL_LORA + A_GATE_LORA

LANE = 128
SUBLANE = 8
SEG_W = 256
ZA_W = 3584
COL_WL = 3072
COL_AL = 3200
COL_GL = 3328
COL_QB = 0
COL_KB = 1024
COL_VB = 2048
COL_GB = 4096
COL_GMA = 6144
COL_GMB = 8192
COL_ZA = 10240
ZA_SPLIT = 2048
W_TOT = 13824
VMEM_LIMIT = 56 * 1024 * 1024


def _cp(sem):
    return pltpu.CompilerParams(dimension_semantics=sem, vmem_limit_bytes=VMEM_LIMIT)


def _sigmoid(x):
    return 1.0 / (1.0 + jnp.exp(-x))


def _inproj_kernel(x_ref, g_ref, w_ref, o_ref, hn_ref):
    @pl.when(pl.program_id(1) == 0)
    def _():
        x = x_ref[...]
        ms = jnp.mean(x * x, axis=-1, keepdims=True)
        hn_ref[...] = (x * lax.rsqrt(ms + NORM_EPS) * g_ref[...]).astype(BF16)

    o_ref[...] = jnp.dot(hn_ref[...], w_ref[...], preferred_element_type=F32)


def _inproj(x, g, w, tm=1024, tn=1536):
    n, d = x.shape
    wt = w.shape[1]
    return pl.pallas_call(
        _inproj_kernel,
        out_shape=jax.ShapeDtypeStruct((n, wt), F32),
        grid=(n // tm, wt // tn),
        in_specs=[
            pl.BlockSpec((tm, d), lambda i, j: (i, 0)),
            pl.BlockSpec((1, d), lambda i, j: (0, 0)),
            pl.BlockSpec((d, tn), lambda i, j: (0, j)),
        ],
        out_specs=pl.BlockSpec((tm, tn), lambda i, j: (i, j)),
        scratch_shapes=[pltpu.VMEM((tm, d), BF16)],
        compiler_params=_cp(("parallel", "arbitrary")),
        name="inproj",
    )(x, g, w)


def _seg_ones():
    r = lax.broadcasted_iota(jnp.int32, (SEG_W, SEG_W), 0) // A_HEAD
    c = lax.broadcasted_iota(jnp.int32, (SEG_W, SEG_W), 1) // A_HEAD
    return (r == c).astype(BF16)


def _segsum64(x, ones):
    def top8(v):
        bits = lax.bitcast_convert_type(v, jnp.uint32) & jnp.uint32(0xFFFF0000)
        return lax.bitcast_convert_type(bits, F32)

    hi = top8(x)
    r1 = x - hi
    mid = top8(r1)
    lo = r1 - mid
    hi, mid, lo = hi.astype(BF16), mid.astype(BF16), lo.astype(BF16)
    parts = []
    for j in range(x.shape[1] // SEG_W):
        sl = slice(j * SEG_W, (j + 1) * SEG_W)
        parts.append(jnp.dot(hi[:, sl], ones, preferred_element_type=F32)
                     + jnp.dot(mid[:, sl], ones, preferred_element_type=F32)
                     + jnp.dot(lo[:, sl], ones, preferred_element_type=F32))
    return jnp.concatenate(parts, axis=1)


def _rwkv_prep_kernel(z1_ref, z2_ref, first_ref, mu_ref, w0_ref, ww2_ref, a0_ref, wa2_ref, wg2_ref, kk_ref_, ka_ref,
                      r_o, w_o, k_o, v_o, kk_o, nkka_o, g_o, carry_scr, *, tm, seq_len):
    z = jnp.concatenate([z1_ref[...], z2_ref[...]], axis=1)
    rolled = pltpu.roll(z, 1, 0)
    rows = lax.broadcasted_iota(jnp.int32, z.shape, 0)
    if seq_len >= tm:
        @pl.when(pl.program_id(1) == 0)
        def _():
            carry_scr[0:1, :] = first_ref[0]

        zprev = jnp.where(rows == 0, carry_scr[0:1, :], rolled)
        carry_scr[0:1, :] = z[tm - 1:tm, :]
    else:
        zprev = jnp.where(rows % seq_len == 0, first_ref[...], rolled)
    zs = z + (zprev - z) * mu_ref[...]
    r = zs[:, 0:A_WIDTH]
    k = zs[:, A_WIDTH:2 * A_WIDTH]
    v = zs[:, 2 * A_WIDTH:3 * A_WIDTH]
    wl = zs[:, COL_WL:COL_AL]
    al = zs[:, COL_AL:COL_GL]
    gl = zs[:, COL_GL:ZA_W]
    wpre = w0_ref[...] + jnp.dot(jnp.tanh(wl).astype(BF16), ww2_ref[...], preferred_element_type=F32)
    nw = -wpre
    softplus = jnp.maximum(nw, 0.0) + jnp.log1p(jnp.exp(-jnp.abs(nw)))
    wlog = -softplus - 0.5
    decay = jnp.exp(-jnp.exp(wlog))
    a = _sigmoid(a0_ref[...] + jnp.dot(al.astype(BF16), wa2_ref[...], preferred_element_type=F32))
    g = jnp.dot(_sigmoid(gl).astype(BF16), wg2_ref[...], preferred_element_type=F32)
    kk = k * kk_ref_[...]
    ss = _segsum64(kk * kk, _seg_ones())
    kk = kk / jnp.maximum(jnp.sqrt(ss), 1e-12)
    k2 = k * (1.0 + (a - 1.0) * ka_ref[...])
    r_o[...] = r
    w_o[...] = decay
    k_o[...] = k2
    v_o[...] = v
    kk_o[...] = kk
    nkka_o[...] = -(kk * a)
    g_o[...] = g


def _rwkv_prep(z, shift0_l, bsz, t, pw, tm=256):
    n = bsz * t
    nt = max(t // tm, 1)
    if t >= tm:
        first = shift0_l[:, None, :]
        first_spec = pl.BlockSpec((1, 1, ZA_W), lambda i, j: (i, 0, 0))
    else:
        first = jnp.pad(shift0_l[:, None, :], ((0, 0), (0, t - 1), (0, 0))).reshape(n, ZA_W)
        first_spec = pl.BlockSpec((tm, ZA_W), lambda i, j: (i * nt + j, 0))
    row = lambda w: pl.BlockSpec((1, w), lambda i, j: (0, 0))
    full = lambda a: pl.BlockSpec(a.shape, lambda i, j: (0, 0))
    out = jax.ShapeDtypeStruct((n, A_WIDTH), F32)
    return pl.pallas_call(
        functools.partial(_rwkv_prep_kernel, tm=tm, seq_len=t),
        out_shape=[out] * 7,
        grid=(n // (tm * nt), nt),
        in_specs=[
            pl.BlockSpec((tm, ZA_SPLIT), lambda i, j: (i * nt + j, COL_ZA // ZA_SPLIT)),
            pl.BlockSpec((tm, ZA_W - ZA_SPLIT), lambda i, j: (i * nt + j, (COL_ZA + ZA_SPLIT) // (ZA_W - ZA_SPLIT))),
            first_spec,
            row(ZA_W), row(A_WIDTH), full(pw["ww2"]), row(A_WIDTH), full(pw["wa2"]), full(pw["wg2"]),
            row(A_WIDTH), row(A_WIDTH),
        ],
        out_specs=[pl.BlockSpec((tm, A_WIDTH), lambda i, j: (i * nt + j, 0))] * 7,
        scratch_shapes=[pltpu.VMEM((SUBLANE, ZA_W), F32)],
        compiler_params=_cp(("parallel", "arbitrary")),
        name="rwkv_prep",
    )(z, z, first, pw["mu"], pw["w0"], pw["ww2"], pw["a0"], pw["wa2"], pw["wg2"], pw["k_k"], pw["k_a"])


def _rwkv_scan_kernel(r_ref, w_ref, k_ref, v_ref, kk_ref, nkka_ref, s0_ref, m4_ref, y_ref, sf_ref, s_scr,
                      xa_scr, xb_scr, y8_scr, rows_scr, *, nb, tc):
    c = pl.program_id(1)

    @pl.when(c == 0)
    def _():
        for b in range(nb):
            for h in range(A_HEADS):
                s_scr[b, :, h * A_HEAD:(h + 1) * A_HEAD] = s0_ref[b, h]

    lane = lax.broadcasted_iota(jnp.int32, (A_HEAD, LANE), 1)
    sub = lax.broadcasted_iota(jnp.int32, (A_HEAD, LANE), 0)
    lo = lane < A_HEAD
    diag = (lane & (A_HEAD - 1)) == sub
    diag_lo = jnp.logical_and(diag, lo)
    diag_hi = jnp.logical_and(diag, jnp.logical_not(lo))
    nq = A_WIDTH // SEG_W
    npair = A_WIDTH // LANE
    ngroup = xa_scr.shape[0]
    gb = nb // ngroup

    seq_refs = (r_ref, w_ref, k_ref, v_ref, kk_ref, nkka_ref)
    seq_ids = [id(ref) for ref in seq_refs]

    def step(t8, carry):
        base = pl.multiple_of(t8 * SUBLANE, SUBLANE)
        for a, ref in enumerate(seq_refs):
            for b in range(nb):
                rows_scr[a, b] = ref[b, pl.ds(base, SUBLANE), :]
        for i in range(SUBLANE):
            row = lambda ref, b, lanes: rows_scr[seq_ids.index(id(ref)), b, i:i + 1, lanes]

            def phase_a(g):
                for bl in range(gb):
                    b = g * gb + bl
                    for j in range(nq):
                        q = slice(j * SEG_W, (j + 1) * SEG_W)
                        r0 = (bl * nq + j) * A_HEAD
                        xa_scr[g, r0:r0 + A_HEAD, :] = (s_scr[b, :, q] * row(kk_ref, b, q)).astype(BF16)
                return jnp.dot(xa_scr[g], m4_ref[...], preferred_element_type=F32)

            def phase_b(g, sa_all):
                for bl in range(gb):
                    b = g * gb + bl
                    for j in range(npair):
                        sl = slice(j * LANE, (j + 1) * LANE)
                        r0 = (bl * nq + j // 2) * A_HEAD
                        half = slice((j % 2) * LANE, (j % 2 + 1) * LANE)
                        vrow = row(v_ref, b, sl)
                        v_lo = jnp.sum(jnp.where(diag_lo, vrow, 0.0), axis=1, keepdims=True)
                        v_hi = jnp.sum(jnp.where(diag_hi, vrow, 0.0), axis=1, keepdims=True)
                        vb = jnp.where(lo, v_lo, v_hi)
                        s = (s_scr[b, :, sl] * row(w_ref, b, sl)
                             + sa_all[r0:r0 + A_HEAD, half] * row(nkka_ref, b, sl) + vb * row(k_ref, b, sl))
                        s_scr[b, :, sl] = s
                        xb_scr[g, r0:r0 + A_HEAD, half] = (s * row(r_ref, b, sl)).astype(BF16)
                return jnp.dot(xb_scr[g], m4_ref[...], preferred_element_type=F32)

            def phase_c(g, y_all):
                for bl in range(gb):
                    b = g * gb + bl
                    for j in range(npair):
                        sl = slice(j * LANE, (j + 1) * LANE)
                        r0 = (bl * nq + j // 2) * A_HEAD
                        half = slice((j % 2) * LANE, (j % 2 + 1) * LANE)
                        yb = y_all[r0:r0 + A_HEAD, half]
                        y8_scr[b, i:i + 1, sl] = jnp.sum(jnp.where(diag, yb, 0.0), axis=0, keepdims=True)

            sa = [phase_a(g) for g in range(ngroup)]
            ya = [phase_b(g, sa[g]) for g in range(ngroup)]
            for g in range(ngroup):
                phase_c(g, ya[g])
        for b in range(nb):
            y_ref[b, pl.ds(base, SUBLANE), :] = y8_scr[b]
        return carry

    lax.fori_loop(0, tc // SUBLANE, step, 0)

    @pl.when(c == pl.num_programs(1) - 1)
    def _():
        for b in range(nb):
            for h in range(A_HEADS):
                sf_ref[b, h] = s_scr[b, :, h * A_HEAD:(h + 1) * A_HEAD]


def _rwkv_scan(seqs, s0, nb, tc):
    bsz, t, _ = seqs[0].shape
    seq = pl.BlockSpec((nb, tc, A_WIDTH), lambda i, c: (i, c, 0))
    st = pl.BlockSpec((nb, A_HEADS, A_HEAD, A_HEAD), lambda i, c: (i, 0, 0, 0))
    seg = jnp.arange(SEG_W, dtype=jnp.int32) // A_HEAD
    m4 = (seg[:, None] == seg[None, :]).astype(BF16)
    ngroup = 2
    xrows = (nb // ngroup) * (A_WIDTH // SEG_W) * A_HEAD
    x_scr = pltpu.VMEM((ngroup, xrows, SEG_W), BF16)
    return pl.pallas_call(
        functools.partial(_rwkv_scan_kernel, nb=nb, tc=tc),
        out_shape=[jax.ShapeDtypeStruct((bsz, t, A_WIDTH), F32),
                   jax.ShapeDtypeStruct((bsz, A_HEADS, A_HEAD, A_HEAD), F32)],
        grid=(bsz // nb, t // tc),
        in_specs=[seq] * 6 + [st, pl.BlockSpec((SEG_W, SEG_W), lambda i, c: (0, 0))],
        out_specs=[seq, st],
        scratch_shapes=[pltpu.VMEM((nb, A_HEAD, A_WIDTH), F32), x_scr, x_scr,
                        pltpu.VMEM((nb, SUBLANE, A_WIDTH), F32), pltpu.VMEM((6, nb, SUBLANE, A_WIDTH), F32)],
        compiler_params=_cp(("parallel", "arbitrary")),
        name="rwkv_scan",
    )(*seqs, s0, m4)


def _retention_kernel(q_ref, k_ref, v_ref, g_ref, cos_ref, sin_ref, dm_ref, qd_ref, kd_ref, gc_ref, r0_ref,
                      o_ref, rn_ref, r_scr):
    c = pl.program_id(1)
    nbr = r_scr.shape[0]
    chunk = cos_ref.shape[0]

    @pl.when(c == 0)
    def _():
        r_scr[...] = r0_ref[...]

    cos = cos_ref[...]
    sin = sin_ref[...]

    def rot(x):
        return x * cos + pltpu.roll(x, B_QK_HEAD // 2, 1) * sin

    for bi in range(nbr):
        rows = slice(bi * chunk, (bi + 1) * chunk)
        for h in range(B_HEADS):
            qs = slice(h * B_QK_HEAD, (h + 1) * B_QK_HEAD)
            vs = slice(h * B_V_HEAD, (h + 1) * B_V_HEAD)
            q = rot(q_ref[rows, qs])
            k = rot(k_ref[rows, qs]) * (B_QK_HEAD ** -0.5)
            v = v_ref[rows, vs].astype(BF16)
            rs = r_scr[bi, h]
            qb = q.astype(BF16)
            s = lax.dot_general(qb, k.astype(BF16), (((1,), (1,)), ((), ())),
                                preferred_element_type=F32) * dm_ref[h]
            inner = jnp.dot(s.astype(BF16), v, preferred_element_type=F32)
            cross = jnp.dot(qb, rs.astype(BF16), preferred_element_type=F32) * qd_ref[h]
            kd = (k * kd_ref[h]).astype(BF16)
            r_scr[bi, h] = gc_ref[h] * rs + lax.dot_general(kd, v, (((0,), (0,)), ((), ())),
                                                            preferred_element_type=F32)
            o = inner + cross
            o = o * lax.rsqrt(jnp.mean(o * o, axis=-1, keepdims=True) + NORM_EPS)
            gate = g_ref[rows, vs]
            o_ref[rows, vs] = gate * _sigmoid(gate) * o

    @pl.when(c == pl.num_programs(1) - 1)
    def _():
        rn_ref[...] = r_scr[...]


def _retention(z, r0, tabs, bsz, t, chunk):
    cos, sin, dm, qd, kd, gc = tabs
    nc = t // chunk
    nbr = 4 if nc == 1 else 1
    seq = lambda w, col: pl.BlockSpec((nbr * chunk, w), lambda b, c: (b * nc + c, col // w))
    tab3 = lambda a: pl.BlockSpec(a.shape, lambda b, c: (0, 0, 0))
    st = pl.BlockSpec((nbr, B_HEADS, B_QK_HEAD, B_V_HEAD), lambda b, c: (b, 0, 0, 0))
    return pl.pallas_call(
        _retention_kernel,
        out_shape=[jax.ShapeDtypeStruct((bsz * t, B_V_WIDTH), F32),
                   jax.ShapeDtypeStruct((bsz, B_HEADS, B_QK_HEAD, B_V_HEAD), F32)],
        grid=(bsz // nbr, nc),
        in_specs=[
            seq(B_QK_WIDTH, COL_QB), seq(B_QK_WIDTH, COL_KB), seq(B_V_WIDTH, COL_VB), seq(B_V_WIDTH, COL_GB),
            pl.BlockSpec((chunk, B_QK_HEAD), lambda b, c: (c, 0)),
            pl.BlockSpec((chunk, B_QK_HEAD), lambda b, c: (c, 0)),
            tab3(dm), tab3(qd), tab3(kd), tab3(gc), st,
        ],
        out_specs=[pl.BlockSpec((nbr * chunk, B_V_WIDTH), lambda b, c: (b * nc + c, 0)), st],
        scratch_shapes=[pltpu.VMEM((nbr, B_HEADS, B_QK_HEAD, B_V_HEAD), F32)],
        compiler_params=_cp(("parallel", "arbitrary")),
        name="retention",
    )(z, z, z, z, cos, sin, dm, qd, kd, gc, r0)


def _retention_tables(pos, chunk):
    half = B_QK_HEAD // 2
    inv = jnp.power(ROT_BASE, -jnp.linspace(0.0, 1.0, half, dtype=F32))
    ang = pos[:, None] * inv[None, :]
    cos = jnp.cos(ang)
    sin = jnp.sin(ang)
    cos2 = jnp.concatenate([cos, cos], axis=-1)
    sin2 = jnp.concatenate([-sin, sin], axis=-1)
    log_g = jnp.log(1.0 - jnp.power(2.0, -5.0 - jnp.arange(B_HEADS, dtype=F32)))
    idx = jnp.arange(chunk, dtype=F32)
    diff = idx[:, None] - idx[None, :]
    dmask = jnp.where(diff >= 0, jnp.exp(log_g[:, None, None] * jnp.maximum(diff, 0.0)), 0.0)
    q_dec = jnp.exp(log_g[:, None] * (idx[None, :] + 1.0))
    k_dec = jnp.exp(log_g[:, None] * (chunk - 1.0 - idx[None, :]))
    g_c = jnp.exp(log_g * chunk)
    qd = jnp.broadcast_to(q_dec[:, :, None], (B_HEADS, chunk, B_V_HEAD))
    kd = jnp.broadcast_to(k_dec[:, :, None], (B_HEADS, chunk, B_QK_HEAD))
    gc = jnp.broadcast_to(g_c[:, None, None], (B_HEADS, B_QK_HEAD, B_V_HEAD))
    return cos2, sin2, dmask, qd, kd, gc


def _merge1_kernel(y_ref, r_ref, k_ref, v_ref, g_ref, ob_ref, gma_ref, gmb_ref, lw_ref, lb_ref, rk_ref, wpa_ref,
                   wpb_ref, m_ref):
    ones = _seg_ones()
    y = y_ref[...]
    mean = _segsum64(y, ones) * (1.0 / A_HEAD)
    yc = y - mean
    var = _segsum64(yc * yc, ones) * (1.0 / A_HEAD)
    yn = yc * lax.rsqrt(var + A_GN_EPS) * lw_ref[...] + lb_ref[...]
    bonus = _segsum64(r_ref[...] * k_ref[...] * rk_ref[...], ones) * v_ref[...]
    oa = ((yn + bonus) * g_ref[...]).astype(BF16)
    pa = jnp.dot(oa, wpa_ref[...], preferred_element_type=F32)
    pb = jnp.dot(ob_ref[...].astype(BF16), wpb_ref[...], preferred_element_type=F32)
    m_ref[...] = (_sigmoid(gma_ref[...]) * pa + _sigmoid(gmb_ref[...]) * pb).astype(m_ref.dtype)


def _merge1(y, r, k2, v, g, ob, z, pw, tm=256):
    n = y.shape[0]
    a_blk = pl.BlockSpec((tm, A_WIDTH), lambda i: (i, 0))
    row = pl.BlockSpec((1, A_WIDTH), lambda i: (0, 0))
    wide = lambda col: pl.BlockSpec((tm, D_MODEL), lambda i: (i, col // D_MODEL))
    resident = lambda a: pl.BlockSpec(a.shape, lambda i: (0, 0), pipeline_mode=pl.Buffered(1))
    return pl.pallas_call(
        _merge1_kernel,
        out_shape=jax.ShapeDtypeStruct((n, D_MODEL), BF16),
        grid=(n // tm,),
        in_specs=[
            a_blk, a_blk, a_blk, a_blk, a_blk,
            pl.BlockSpec((tm, B_V_WIDTH), lambda i: (i, 0)),
            wide(COL_GMA), wide(COL_GMB),
            row, row, row,
            resident(pw["wpa"]), resident(pw["wpb"]),
        ],
        out_specs=pl.BlockSpec((tm, D_MODEL), lambda i: (i, 0)),
        compiler_params=_cp(("parallel",)),
        name="merge1",
    )(y, r, k2, v, g, ob, z, z, pw["lnx_w"], pw["lnx_b"], pw["r_k"], pw["wpa"], pw["wpb"])


def _merge2_kernel(m_ref, x_ref, w_ref, g_ref, x1_ref, xn_ref):
    x1 = x_ref[...] + jnp.dot(m_ref[...], w_ref[...], preferred_element_type=F32)
    x1_ref[...] = x1
    ms = jnp.mean(x1 * x1, axis=-1, keepdims=True)
    xn_ref[...] = (x1 * lax.rsqrt(ms + NORM_EPS) * g_ref[...]).astype(BF16)


def _merge2(m, x, w_out, norm2, tm=512):
    n = x.shape[0]
    blk = pl.BlockSpec((tm, D_MODEL), lambda i: (i, 0))
    return pl.pallas_call(
        _merge2_kernel,
        out_shape=[jax.ShapeDtypeStruct((n, D_MODEL), F32), jax.ShapeDtypeStruct((n, D_MODEL), BF16)],
        grid=(n // tm,),
        in_specs=[blk, blk, pl.BlockSpec((D_MODEL, D_MODEL), lambda i: (0, 0)),
                  pl.BlockSpec((1, D_MODEL), lambda i: (0, 0))],
        out_specs=[blk, blk],
        compiler_params=_cp(("parallel",)),
        name="merge2",
    )(m, x, w_out, norm2)


_PAIRS = [(i, j) for i in range(PEER_TOPK) for j in range(PEER_TOPK) if (i + 1) * (j + 1) <= PEER_TOPK]


def _top16_rows(s):
    rows = []
    cur = s
    for i in range(PEER_TOPK):
        m = jnp.max(cur, axis=0, keepdims=True)
        rows.append(m)
        if i + 1 < PEER_TOPK:
            cur = jnp.where(cur == m, -jnp.inf, cur)
    return rows


def _peer_topk_kernel(xn_ref, wq_ref, keys_ref, s1_ref, s2_ref, st_ref, top_scr, cand_scr):
    q = jnp.dot(xn_ref[...], wq_ref[...], preferred_element_type=F32)
    for h in range(PEER_HEADS):
        for p in range(2):
            c0 = (h * 2 + p) * N_KEYS
            qhp = q[:, c0:c0 + N_KEYS].astype(BF16)
            st = lax.dot_general(keys_ref[h, p], qhp, (((1,), (1,)), ((), ())), preferred_element_type=F32)
            (s1_ref if p == 0 else s2_ref)[h] = st
            for i, row in enumerate(_top16_rows(st)):
                top_scr[p, i, h:h + 1, :] = row
    for n, (i, j) in enumerate(_PAIRS):
        cand_scr[n] = top_scr[0, i] + top_scr[1, j]
    m = None
    for it in range(PEER_TOPK):
        m = cand_scr[0]
        for n in range(1, len(_PAIRS)):
            m = jnp.maximum(m, cand_scr[n])
        if it + 1 < PEER_TOPK:
            for n in range(len(_PAIRS)):
                cnd = cand_scr[n]
                cand_scr[n] = jnp.where(cnd == m, -jnp.inf, cnd)
    tau = m
    top = top_scr[0, 0] + top_scr[1, 0]
    zsum = jnp.zeros_like(tau)
    for (i, j) in _PAIRS:
        cnd = top_scr[0, i] + top_scr[1, j]
        zsum = zsum + jnp.where(cnd >= tau, jnp.exp(cnd - top), 0.0)
    st_ref[0] = tau
    st_ref[1] = top_scr[0, 0]
    st_ref[2] = top_scr[1, 0]
    st_ref[3] = 1.0 / zsum


def _peer_topk(xn, wq, keys, tm=512):
    n = xn.shape[0]
    s_shape = jax.ShapeDtypeStruct((PEER_HEADS, N_KEYS, n), F32)
    s_spec = pl.BlockSpec((PEER_HEADS, N_KEYS, tm), lambda i: (0, 0, i))
    return pl.pallas_call(
        _peer_topk_kernel,
        out_shape=[s_shape, s_shape, jax.ShapeDtypeStruct((4, PEER_HEADS, n), F32)],
        grid=(n // tm,),
        in_specs=[pl.BlockSpec((tm, D_MODEL), lambda i: (i, 0)),
                  pl.BlockSpec((D_MODEL, D_MODEL), lambda i: (0, 0)),
                  pl.BlockSpec(keys.shape, lambda i: (0, 0, 0, 0))],
        out_specs=[s_spec, s_spec, pl.BlockSpec((4, PEER_HEADS, tm), lambda i: (0, 0, i))],
        scratch_shapes=[pltpu.VMEM((2, PEER_TOPK, PEER_HEADS, tm), F32),
                        pltpu.VMEM((len(_PAIRS), PEER_HEADS, tm), F32)],
        compiler_params=_cp(("parallel",)),
        name="peer_topk",
    )(xn, wq, keys)


def _peer_dense_kernel(xn_ref, s1_ref, s2_ref, st_ref, u_ref, v_ref, x1_ref, nf_ref, o_ref, acc_scr, b_scr, sc_scr,
                       w_scr, *, ec, tm):
    c = pl.program_id(1)
    n1 = ec // N_KEYS

    @pl.when(c == 0)
    def _():
        acc_scr[...] = jnp.zeros_like(acc_scr)
        for h in range(PEER_HEADS):
            b_scr[h] = jnp.exp(s2_ref[h] - st_ref[2, pl.ds(h, 1), :]) * (0.5 * st_ref[3, pl.ds(h, 1), :])

    sc_scr[...] = lax.dot_general(u_ref[...], xn_ref[...], (((1,), (1,)), ((), ())), preferred_element_type=F32)

    i1_base = pl.multiple_of(c * n1, SUBLANE)
    for ii in range(n1):
        for tcol in range(tm // LANE):
            cs = slice(tcol * LANE, (tcol + 1) * LANE)
            g = jnp.zeros((N_KEYS, LANE), F32)
            for h in range(PEER_HEADS):
                s1row = s1_ref[h, pl.ds(i1_base, n1), cs][ii:ii + 1, :]
                theta = st_ref[0, pl.ds(h, 1), cs] - s1row
                arow = jnp.exp(s1row - st_ref[1, pl.ds(h, 1), cs])
                g = g + jnp.where(s2_ref[h, :, cs] >= theta, arow * b_scr[h, :, cs], 0.0)
            x = sc_scr[ii * N_KEYS:(ii + 1) * N_KEYS, cs]
            gelu2 = x * (1.0 + lax.erf(x * np.float32(np.sqrt(0.5))))
            w_scr[ii * N_KEYS:(ii + 1) * N_KEYS, cs] = (g * gelu2).astype(BF16)

    acc_scr[...] += lax.dot_general(v_ref[...], w_scr[...], (((0,), (0,)), ((), ())), preferred_element_type=F32)

    @pl.when(c == pl.num_programs(1) - 1)
    def _():
        x = x1_ref[...] + acc_scr[...].T
        ms = jnp.mean(x * x, axis=-1, keepdims=True)
        o_ref[...] = x * lax.rsqrt(ms + NORM_EPS) * nf_ref[...]


def _peer_dense(xn_bf, s1, s2, stats, u_bf, v_bf, x1, norm_f, tm=512, ec=1024):
    assert ec // N_KEYS == SUBLANE
    n = xn_bf.shape[0]
    s_spec = pl.BlockSpec((PEER_HEADS, N_KEYS, tm), lambda i, c: (0, 0, i))
    once = dict(pipeline_mode=pl.Buffered(1))
    return pl.pallas_call(
        functools.partial(_peer_dense_kernel, ec=ec, tm=tm),
        out_shape=jax.ShapeDtypeStruct((n, D_MODEL), F32),
        grid=(n // tm, u_bf.shape[0] // ec),
        in_specs=[
            pl.BlockSpec((tm, D_MODEL), lambda i, c: (i, 0)),
            s_spec, s_spec,
            pl.BlockSpec((4, PEER_HEADS, tm), lambda i, c: (0, 0, i)),
            pl.BlockSpec((ec, D_MODEL), lambda i, c: (c, 0)),
            pl.BlockSpec((ec, D_MODEL), lambda i, c: (c, 0)),
            pl.BlockSpec((tm, D_MODEL), lambda i, c: (i, 0), **once),
            pl.BlockSpec((1, D_MODEL), lambda i, c: (0, 0)),
        ],
        out_specs=pl.BlockSpec((tm, D_MODEL), lambda i, c: (i, 0), **once),
        scratch_shapes=[pltpu.VMEM((D_MODEL, tm), F32), pltpu.VMEM((PEER_HEADS, N_KEYS, tm), F32),
                        pltpu.VMEM((ec, tm), F32), pltpu.VMEM((ec, tm), BF16)],
        compiler_params=_cp(("parallel", "arbitrary")),
        name="peer_dense",
    )(xn_bf, s1, s2, stats, u_bf, v_bf, x1, norm_f)


def _pad_cols(a, w):
    return jnp.pad(a, ((0, 0), (0, w - a.shape[1])))


def _za_layout(a):
    r3 = a[:, :3 * A_WIDTH]
    wl = _pad_cols(a[:, 3 * A_WIDTH:3 * A_WIDTH + A_DECAY_LORA], COL_AL - COL_WL)
    al = _pad_cols(a[:, 3 * A_WIDTH + A_DECAY_LORA:3 * A_WIDTH + A_DECAY_LORA + A_ICL_LORA], COL_GL - COL_AL)
    gl = _pad_cols(a[:, 3 * A_WIDTH + A_DECAY_LORA + A_ICL_LORA:SHIFT_W], ZA_W - COL_GL)
    return jnp.concatenate([r3, wl, al, gl], axis=1)


def _za_unlayout(a):
    return jnp.concatenate([a[:, :3 * A_WIDTH], a[:, COL_WL:COL_WL + A_DECAY_LORA], a[:, COL_AL:COL_AL + A_ICL_LORA],
                            a[:, COL_GL:COL_GL + A_GATE_LORA]], axis=1)


def _pad_rows(a, n):
    return jnp.pad(a, ((0, n - a.shape[0]), (0, 0)))


def _prepare_weights(lw):
    (norm1, w_in, mu_shift, w0, w_w2, a0, w_a2, w_g2, k_k, k_a, r_k, lnx_w, lnx_b, w_pa, w_pb, w_out, norm2,
     peer_wq, peer_keys, peer_u, peer_v) = lw
    row = lambda a: a[None, :]
    w_in_bf = w_in.astype(BF16)
    wcat = jnp.concatenate([w_in_bf[:, SHIFT_W:], _za_layout(w_in_bf[:, :SHIFT_W])], axis=1)
    return dict(
        norm1=row(norm1), wcat=wcat, mu=_za_layout(row(mu_shift)), w0=row(w0),
        ww2=_pad_rows(w_w2, COL_AL - COL_WL).astype(BF16), a0=row(a0),
        wa2=_pad_rows(w_a2, COL_GL - COL_AL).astype(BF16), wg2=_pad_rows(w_g2, ZA_W - COL_GL).astype(BF16),
        k_k=row(k_k), k_a=row(k_a), r_k=row(r_k), lnx_w=row(lnx_w), lnx_b=row(lnx_b),
        wpa=w_pa.astype(BF16), wpb=w_pb.astype(BF16), wout=w_out.astype(BF16), norm2=row(norm2),
        wq=peer_wq.astype(BF16), keys=peer_keys.astype(BF16), u=peer_u.astype(BF16), v=peer_v.astype(BF16))


def _trunk(x, shift0, wkv0, ret0, pos, pw, norm_f, scan_tc):
    bsz, t, d = x.shape
    n = bsz * t
    z = _inproj(x.reshape(n, d), pw["norm1"], pw["wcat"])
    shift_new = _za_unlayout(z.reshape(bsz, t, W_TOT)[:, -1, COL_ZA:])

    r, w, k2, v, kk, nkka, g = _rwkv_prep(z, _za_layout(shift0), bsz, t, pw)
    y, wkv_new = _rwkv_scan([a.reshape(bsz, t, A_WIDTH) for a in (r, w, k2, v, kk, nkka)], wkv0, nb=4, tc=scan_tc)

    chunk = RET_CHUNK if t % RET_CHUNK == 0 else t
    ob, ret_new = _retention(z, ret0, _retention_tables(pos, chunk), bsz, t, chunk)

    m = _merge1(y.reshape(n, A_WIDTH), r, k2, v, g, ob, z, pw)
    x1, xn = _merge2(m, x.reshape(n, d), pw["wout"], pw["norm2"])
    s1, s2, stats = _peer_topk(xn, pw["wq"], pw["keys"])
    y = _peer_dense(xn, s1, s2, stats, pw["u"], pw["v"], x1, norm_f[None, :])
    return y.reshape(bsz, t, d), shift_new[None], wkv_new[None], ret_new[None]


def kernel(x_prompt, x_sample, state_shift, state_wkv, state_ret, norm1, w_in, mu_shift, w0, w_w2, a0, w_a2, w_g2, k_k, k_a, r_k, lnx_w, lnx_b, w_pa, w_pb, w_out, norm2, peer_wq, peer_keys, peer_u, peer_v, norm_f):
    weights = (norm1, w_in, mu_shift, w0, w_w2, a0, w_a2, w_g2, k_k, k_a, r_k, lnx_w, lnx_b, w_pa, w_pb, w_out,
               norm2, peer_wq, peer_keys, peer_u, peer_v)
    assert norm1.shape[0] == 1
    pw = _prepare_weights([w[0] for w in weights])
    bp, tp, _ = x_prompt.shape
    ts = x_sample.shape[1]
    shift0 = jnp.zeros((bp, SHIFT_W), F32)
    wkv0 = jnp.zeros((bp, A_HEADS, A_HEAD, A_HEAD), F32)
    ret0 = jnp.zeros((bp, B_HEADS, B_QK_HEAD, B_V_HEAD), F32)
    y_p, shift_p, wkv_p, ret_p = _trunk(x_prompt, shift0, wkv0, ret0, jnp.arange(tp, dtype=F32), pw, norm_f,
                                        scan_tc=128)
    y_s, shift_s, wkv_s, ret_s = _trunk(x_sample, state_shift[0], state_wkv[0], state_ret[0],
                                        PAST_LEN + jnp.arange(ts, dtype=F32), pw, norm_f, scan_tc=ts)
    return (y_p, y_s, shift_p, wkv_p, ret_p, shift_s, wkv_s, ret_s)
```
